```python
import jax, jax.numpy as jnp
from jax import lax
import numpy as np

D_MODEL = 2048
BATCH = 2
SEQ = 4096
DEPTH = 1

D_MIX = D_MODEL
ATTN_HEADS = 8
HEAD_DIM = 128
D_ATTN = ATTN_HEADS * HEAD_DIM
SGU_GROUPS = 8
SGU_GROUP_DIM = (D_MIX - D_ATTN) // SGU_GROUPS
D_SGU = SGU_GROUPS * SGU_GROUP_DIM
D_IN = 3 * D_ATTN + 2 * D_SGU
MOBA_BLOCK = 256
MOBA_TOPK = 3
Q_CHUNK = 32
SGU_CHUNK = 128
D_FF = 5632
EPS = 1e-6
NEG_INF = -1e30

kernel_name = 'hymba_moba_sgu_macaron'


def rmsnorm(x, g):
    xf = x.astype(jnp.float32)
    y = xf * lax.rsqrt(jnp.mean(xf * xf, axis=-1, keepdims=True) + EPS)
    return (y * g.astype(jnp.float32)).astype(x.dtype)


def swiglu(h, w_gate, w_up, w_down):
    return (jax.nn.silu(h @ w_gate) * (h @ w_up)) @ w_down


def pad_seq(a, axis, multiple):
    s = a.shape[axis]
    s_pad = -(-s // multiple) * multiple
    widths = [(0, 0)] * a.ndim
    widths[axis] = (0, s_pad - s)
    return jnp.pad(a, widths), s_pad


def moba_attention(q, k, v):
    B, H, S, Dh = q.shape
    q, S_pad = pad_seq(q, 2, MOBA_BLOCK)
    k, _ = pad_seq(k, 2, MOBA_BLOCK)
    v, _ = pad_seq(v, 2, MOBA_BLOCK)
    nb = S_pad // MOBA_BLOCK
    kb = k.reshape(B, H, nb, MOBA_BLOCK, Dh)
    vb = v.reshape(B, H, nb, MOBA_BLOCK, Dh)
    k_mean = jnp.mean(kb.astype(jnp.float32), axis=3)
    q_blk = jnp.arange(S_pad) // MOBA_BLOCK
    gate = jnp.einsum('bhsd,bhnd->bhsn', q.astype(jnp.float32), k_mean)
    fully_past = jnp.arange(nb)[None, :] < q_blk[:, None]
    gate = jnp.where(fully_past, gate, NEG_INF)
    topk = min(MOBA_TOPK, nb)
    _, sel_idx = lax.top_k(gate, topk)
    sel_valid = jnp.arange(topk)[None, :] < q_blk[:, None]
    gather = jax.vmap(jax.vmap(lambda blocks, ids: blocks[ids]))
    scale = HEAD_DIM ** -0.5

    def chunk(c):
        start = c * Q_CHUNK
        q_c = lax.dynamic_slice_in_dim(q, start, Q_CHUNK, axis=2)
        idx_c = lax.dynamic_slice_in_dim(sel_idx, start, Q_CHUNK, axis=2)
        valid_c = lax.dynamic_slice_in_dim(sel_valid, start, Q_CHUNK, axis=0)
        own = start // MOBA_BLOCK
        k_own = lax.dynamic_index_in_dim(kb, own, axis=2, keepdims=False)
        v_own = lax.dynamic_index_in_dim(vb, own, axis=2, keepdims=False)
        q_pos = start + jnp.arange(Q_CHUNK)
        k_pos = own * MOBA_BLOCK + jnp.arange(MOBA_BLOCK)
        s_own = jnp.einsum('bhqd,bhkd->bhqk', q_c, k_own).astype(jnp.float32) * scale
        s_own = jnp.where(k_pos[None, :] <= q_pos[:, None], s_own, NEG_INF)
        k_sel = gather(kb, idx_c)
        v_sel = gather(vb, idx_c)
        s_sel = jnp.einsum('bhqd,bhqjkd->bhqjk', q_c, k_sel).astype(jnp.float32) * scale
        s_sel = jnp.where(valid_c[:, :, None], s_sel, NEG_INF)
        s_all = jnp.concatenate(
            [s_own, s_sel.reshape(B, H, Q_CHUNK, topk * MOBA_BLOCK)], axis=-1)
        p = jax.nn.softmax(s_all, axis=-1).astype(v.dtype)
        p_own = p[..., :MOBA_BLOCK]
        p_sel = p[..., MOBA_BLOCK:].reshape(B, H, Q_CHUNK, topk, MOBA_BLOCK)
        return (jnp.einsum('bhqk,bhkd->bhqd', p_own, v_own)
                + jnp.einsum('bhqjk,bhqjkd->bhqd', p_sel, v_sel))

    out = lax.map(chunk, jnp.arange(S_pad // Q_CHUNK))
    out = out.transpose(1, 2, 0, 3, 4).reshape(B, H, S_pad, Dh)
    return out[:, :, :S]


def spatial_gating(z, ln_g, ln_b, w_s, b_s):
    B, S, _ = z.shape
    u, g = jnp.split(z, 2, axis=-1)
    g = g.reshape(B, S, SGU_GROUPS, SGU_GROUP_DIM)
    gf = g.astype(jnp.float32)
    mu = jnp.mean(gf, axis=-1, keepdims=True)
    var = jnp.mean(jnp.square(gf - mu), axis=-1, keepdims=True)
    gf = (gf - mu) * lax.rsqrt(var + EPS)
    g = (gf * ln_g.reshape(SGU_GROUPS, SGU_GROUP_DIM)
         + ln_b.reshape(SGU_GROUPS, SGU_GROUP_DIM)).astype(z.dtype)
    g, S_pad = pad_seq(g, 1, SGU_CHUNK)
    gc = g.reshape(B, S_pad // SGU_CHUNK, SGU_CHUNK, SGU_GROUPS, SGU_GROUP_DIM)
    w = jnp.tril(w_s)
    mixed = jnp.einsum('gts,bnsgc->bntgc', w, gc) + b_s.T[None, None, :, :, None]
    mixed = mixed.reshape(B, S_pad, D_SGU)[:, :S]
    return u * mixed


def setup_inputs(seed: int = 0) -> dict:
    key = jax.random.key(seed)
    ks = jax.random.split(key, 20)
    n = lambda k, shape, s: jax.random.normal(k, shape, jnp.float32) * s
    gain = lambda k, shape: 1.0 + 0.05 * jax.random.normal(k, shape, jnp.float32)
    L = DEPTH
    return {
        'x': n(ks[0], (BATCH, SEQ, D_MODEL), 1.0),
        'ffn1_norm': gain(ks[1], (L, D_MODEL)),
        'ffn1_w_gate': n(ks[2], (L, D_MODEL, D_FF), D_MODEL ** -0.5),
        'ffn1_w_up': n(ks[3], (L, D_MODEL, D_FF), D_MODEL ** -0.5),
        'ffn1_w_down': n(ks[4], (L, D_FF, D_MODEL), D_FF ** -0.5),
        'mix_norm': gain(ks[5], (L, D_MODEL)),
        'w_in': n(ks[6], (L, D_MODEL, D_IN), D_MODEL ** -0.5),
        'q_norm': gain(ks[7], (L, HEAD_DIM)),
        'k_norm': gain(ks[8], (L, HEAD_DIM)),
        'sgu_ln_gain': gain(ks[9], (L, D_SGU)),
        'sgu_ln_bias': n(ks[10], (L, D_SGU), 0.02),
        'sgu_w_spatial': n(ks[11], (L, SGU_GROUPS, SGU_CHUNK, SGU_CHUNK), 0.5 * SGU_CHUNK ** -0.5),
        'sgu_b_spatial': 1.0 + 0.1 * jax.random.normal(ks[12], (L, SGU_GROUPS, SGU_CHUNK), jnp.float32),
        'attn_out_gain': gain(ks[13], (L, D_ATTN)),
        'sgu_out_gain': gain(ks[14], (L, D_SGU)),
        'w_out': n(ks[15], (L, D_MIX, D_MODEL), D_MIX ** -0.5),
        'ffn2_norm': gain(ks[16], (L, D_MODEL)),
        'ffn2_w_gate': n(ks[17], (L, D_MODEL, D_FF), D_MODEL ** -0.5),
        'ffn2_w_up': n(ks[18], (L, D_MODEL, D_FF), D_MODEL ** -0.5),
        'ffn2_w_down': n(ks[19], (L, D_FF, D_MODEL), D_FF ** -0.5),
    }


def reference(x, ffn1_norm, ffn1_w_gate, ffn1_w_up, ffn1_w_down, mix_norm, w_in,
              q_norm, k_norm, sgu_ln_gain, sgu_ln_bias, sgu_w_spatial, sgu_b_spatial,
              attn_out_gain, sgu_out_gain, w_out, ffn2_norm, ffn2_w_gate, ffn2_w_up,
              ffn2_w_down):
    B, S, _ = x.shape
    for l in range(DEPTH):
        h = rmsnorm(x, ffn1_norm[l])
        x = x + 0.5 * swiglu(h, ffn1_w_gate[l], ffn1_w_up[l], ffn1_w_down[l])
        h = rmsnorm(x, mix_norm[l])
        proj = h @ w_in[l]
        q = proj[..., :D_ATTN]
        k = proj[..., D_ATTN:2 * D_ATTN]
        v = proj[..., 2 * D_ATTN:3 * D_ATTN]
        z = proj[..., 3 * D_ATTN:]
        to_heads = lambda t: t.reshape(B, S, ATTN_HEADS, HEAD_DIM).transpose(0, 2, 1, 3)
        q = rmsnorm(to_heads(q), q_norm[l])
        k = rmsnorm(to_heads(k), k_norm[l])
        v = to_heads(v)
        attn = moba_attention(q, k, v).transpose(0, 2, 1, 3).reshape(B, S, D_ATTN)
        sgu = spatial_gating(jax.nn.gelu(z), sgu_ln_gain[l], sgu_ln_bias[l],
                             sgu_w_spatial[l], sgu_b_spatial[l])
        mixed = jnp.concatenate(
            [rmsnorm(attn, attn_out_gain[l]), rmsnorm(sgu, sgu_out_gain[l])], axis=-1)
        x = x + mixed @ w_out[l]
        h = rmsnorm(x, ffn2_norm[l])
        x = x + 0.5 * swiglu(h, ffn2_w_gate[l], ffn2_w_up[l], ffn2_w_down[l])
    return x
```

```python
import functools

import jax
import jax.numpy as jnp
from jax import lax
from jax.experimental import pallas as pl
from jax.experimental.pallas import tpu as pltpu

D_MODEL = 2048
D_FF = 5632
ATTN_HEADS = 8
HEAD_DIM = 128
D_ATTN = ATTN_HEADS * HEAD_DIM
SGU_GROUPS = 8
SGU_GROUP_DIM = 128
D_SGU = SGU_GROUPS * SGU_GROUP_DIM
MOBA_BLOCK = 256
MOBA_TOPK = 3
SGU_CHUNK = 128
EPS = 1e-6
NEG_INF = -1e30

LANES = 128
AUG_DIM = 2 * HEAD_DIM
VMEM_LIMIT_BYTES = 56 * 1024 * 1024

FFN_TM = 512
FFN_TF = 512
PROJ_TM = 512
ATTN_KV_CHUNK = 512

F32 = jnp.float32
BF16 = jnp.bfloat16


def _rmsnorm(x, gain):
    return x * lax.rsqrt(jnp.mean(x * x, axis=-1, keepdims=True) + EPS) * gain


def _gelu_tanh(x):
    cdf = 0.5 * (1.0 + jnp.tanh(0.7978845608028654 * (x + 0.044715 * (x * x * x))))
    return x * cdf


def _dot(a, b):
    return jnp.dot(a, b, preferred_element_type=F32)


def _dot_nt(a, b, precision=None):
    return lax.dot_general(a, b, (((1,), (1,)), ((), ())), precision=precision,
                           preferred_element_type=F32)


def _params(*semantics):
    return pltpu.CompilerParams(dimension_semantics=semantics, vmem_limit_bytes=VMEM_LIMIT_BYTES)


def _resident(shape):
    return pl.BlockSpec(shape, lambda *_: (0,) * len(shape), pipeline_mode=pl.Buffered(1))


def _ffn_body(*refs, emit_norm, n_ff_steps):
    if emit_norm:
        x_ref, g_ref, wg_ref, wu_ref, wd_ref, g2_ref, o_ref, h2_ref, h_scr = refs
    else:
        x_ref, g_ref, wg_ref, wu_ref, wd_ref, o_ref, h_scr = refs
    j = pl.program_id(1)

    @pl.when(j == 0)
    def _init():
        x = x_ref[...]
        h_scr[...] = _rmsnorm(x, g_ref[...]).astype(BF16)
        o_ref[...] = x

    h = h_scr[...]
    gate = _dot(h, wg_ref[...])
    up = _dot(h, wu_ref[...])
    act = (gate / (1.0 + jnp.exp(-gate))) * up * 0.5
    o_ref[...] += _dot(act.astype(BF16), wd_ref[...])

    if emit_norm:
        @pl.when(j == n_ff_steps - 1)
        def _fin():
            h2_ref[...] = _rmsnorm(o_ref[...], g2_ref[...]).astype(BF16)


def _ffn(x, gain, w_gate, w_up, w_down, next_gain=None):
    tokens = x.shape[0]
    emit_norm = next_gain is not None
    n_ff_steps = D_FF // FFN_TF
    row = pl.BlockSpec((FFN_TM, D_MODEL), lambda i, j: (i, 0))
    vec = pl.BlockSpec((1, D_MODEL), lambda i, j: (0, 0))
    in_specs = [row, vec,
                pl.BlockSpec((D_MODEL, FFN_TF), lambda i, j: (0, j)),
                pl.BlockSpec((D_MODEL, FFN_TF), lambda i, j: (0, j)),
                pl.BlockSpec((FFN_TF, D_MODEL), lambda i, j: (j, 0))]
    args = [x, gain, w_gate, w_up, w_down]
    out_shape = jax.ShapeDtypeStruct((tokens, D_MODEL), F32)
    out_specs = row
    if emit_norm:
        in_specs.append(vec)
        args.append(next_gain)
        out_shape = (out_shape, jax.ShapeDtypeStruct((tokens, D_MODEL), BF16))
        out_specs = (row, row)
    return pl.pallas_call(
        functools.partial(_ffn_body, emit_norm=emit_norm, n_ff_steps=n_ff_steps),
        grid=(tokens // FFN_TM, n_ff_steps),
        in_specs=in_specs, out_specs=out_specs, out_shape=out_shape,
        scratch_shapes=[pltpu.VMEM((FFN_TM, D_MODEL), BF16)],
        compiler_params=_params("parallel", "arbitrary"),
        name="ffn_norm" if emit_norm else "ffn",
    )(*args)


def _qkv_body(h_ref, w_ref, gq_ref, gk_ref, qa_ref, ka_ref, v_ref, km_scr, *, tiles_per_batch):
    tm = PROJ_TM
    blocks_per_batch = tiles_per_batch * (tm // MOBA_BLOCK)
    i = pl.program_id(0)
    n0 = (i % tiles_per_batch) * (tm // MOBA_BLOCK)

    @pl.when(i == 0)
    def _init():
        km_scr[...] = jnp.zeros_like(km_scr)

    h = h_ref[...]
    q = _dot(h, w_ref[:, 0:D_ATTN])
    k = _dot(h, w_ref[:, D_ATTN:2 * D_ATTN])
    v_ref[...] = _dot(h, w_ref[:, 2 * D_ATTN:3 * D_ATTN]).astype(BF16)

    scale = HEAD_DIM ** -0.5
    blk_rows = lax.broadcasted_iota(jnp.int32, (blocks_per_batch, HEAD_DIM), 0)
    n_iota = lax.broadcasted_iota(jnp.int32, (blocks_per_batch, tm), 0)
    lane_tok = lax.broadcasted_iota(jnp.int32, (blocks_per_batch, tm), 1)
    q_blk = n0 + lane_tok // MOBA_BLOCK
    valid = n_iota < q_blk
    tok_blk = n0 + lax.broadcasted_iota(jnp.int32, (tm, LANES), 0) // MOBA_BLOCK
    onehot = jnp.where(lax.broadcasted_iota(jnp.int32, (tm, LANES), 1) == tok_blk, 1.0, 0.0).astype(BF16)
    pad_rows = jnp.zeros((LANES - blocks_per_batch, tm), F32)

    for hh in range(ATTN_HEADS):
        sl = slice(hh * HEAD_DIM, (hh + 1) * HEAD_DIM)
        qn = _rmsnorm(q[:, sl], gq_ref[...])
        kn = _rmsnorm(k[:, sl], gk_ref[...])

        kmean = km_scr[:, sl]
        for b in range(tm // MOBA_BLOCK):
            km = jnp.mean(kn[b * MOBA_BLOCK:(b + 1) * MOBA_BLOCK], axis=0, keepdims=True)
            kmean = jnp.where(blk_rows == n0 + b, km, kmean)
        km_scr[:, sl] = kmean

        gate = _dot_nt(kmean, qn, precision=lax.Precision.HIGHEST)
        cnt = jnp.zeros((blocks_per_batch, tm), jnp.int32)
        for m in range(blocks_per_batch - 1):
            gm = gate[m:m + 1, :]
            beats = (gm > gate) | ((gm == gate) & (m < n_iota))
            cnt = cnt + jnp.where(beats & (m < q_blk), 1, 0)
        sel = (cnt < MOBA_TOPK) & valid
        bias_t = jnp.concatenate([jnp.where(sel, 0.0, NEG_INF), pad_rows], axis=0)
        bias = bias_t.T

        a0 = hh * AUG_DIM
        qa_ref[:, a0:a0 + HEAD_DIM] = (qn * scale).astype(BF16)
        qa_ref[:, a0 + HEAD_DIM:a0 + AUG_DIM] = bias.astype(BF16)
        ka_ref[:, a0:a0 + HEAD_DIM] = kn.astype(BF16)
        ka_ref[:, a0 + HEAD_DIM:a0 + AUG_DIM] = onehot


def _qkv(h, w_qkv, q_gain, k_gain, seq):
    tokens = h.shape[0]
    tm = PROJ_TM
    tiles_per_batch = seq // tm
    blocks_per_batch = seq // MOBA_BLOCK
    assert blocks_per_batch <= LANES and blocks_per_batch % 8 == 0
    row = lambda width: pl.BlockSpec((tm, width), lambda i: (i, 0))
    return pl.pallas_call(
        functools.partial(_qkv_body, tiles_per_batch=tiles_per_batch),
        grid=(tokens // tm,),
        in_specs=[row(D_MODEL), _resident((D_MODEL, 3 * D_ATTN)),
                  _resident((1, HEAD_DIM)), _resident((1, HEAD_DIM))],
        out_specs=(row(ATTN_HEADS * AUG_DIM), row(ATTN_HEADS * AUG_DIM), row(D_ATTN)),
        out_shape=(jax.ShapeDtypeStruct((tokens, ATTN_HEADS * AUG_DIM), BF16),
                   jax.ShapeDtypeStruct((tokens, ATTN_HEADS * AUG_DIM), BF16),
                   jax.ShapeDtypeStruct((tokens, D_ATTN), BF16)),
        scratch_shapes=[pltpu.VMEM((blocks_per_batch, D_ATTN), F32)],
        compiler_params=_params("arbitrary"),
        name="qkv_gate",
    )(h, w_qkv, q_gain, k_gain)


def _sgu_body(h_ref, w_ref, lng_ref, lnb_ref, ws_ref, bs_ref, og_ref, o_ref, sg_scr):
    tm = PROJ_TM
    h = h_ref[...]
    u = _gelu_tanh(_dot(h, w_ref[:, 0:D_SGU]))
    g = _gelu_tanh(_dot(h, w_ref[:, D_SGU:2 * D_SGU]))
    r = lax.broadcasted_iota(jnp.int32, (SGU_CHUNK, SGU_CHUNK), 0)
    c = lax.broadcasted_iota(jnp.int32, (SGU_CHUNK, SGU_CHUNK), 1)
    for gi in range(SGU_GROUPS):
        sl = slice(gi * SGU_GROUP_DIM, (gi + 1) * SGU_GROUP_DIM)
        gg = g[:, sl]
        d = gg - jnp.mean(gg, axis=-1, keepdims=True)
        var = jnp.mean(d * d, axis=-1, keepdims=True)
        gn = (d * lax.rsqrt(var + EPS) * lng_ref[:, sl] + lnb_ref[:, sl]).astype(BF16)
        w_causal = jnp.where(r >= c, ws_ref[gi], 0.0).astype(BF16)
        b_col = bs_ref[:, gi:gi + 1]
        for ci in range(tm // SGU_CHUNK):
            rows = slice(ci * SGU_CHUNK, (ci + 1) * SGU_CHUNK)
            mixed = _dot(w_causal, gn[rows]) + b_col
            sg_scr[rows, sl] = u[rows, sl] * mixed
    o_ref[...] = _rmsnorm(sg_scr[...], og_ref[...]).astype(BF16)


def _sgu(h, w_sgu, ln_gain, ln_bias, w_spatial, b_spatial_t, out_gain):
    tokens = h.shape[0]
    tm = PROJ_TM
    return pl.pallas_call(
        _sgu_body,
        grid=(tokens // tm,),
        in_specs=[pl.BlockSpec((tm, D_MODEL), lambda i: (i, 0)),
                  _resident((D_MODEL, 2 * D_SGU)),
                  _resident((1, D_SGU)), _resident((1, D_SGU)),
                  _resident((SGU_GROUPS, SGU_CHUNK, SGU_CHUNK)),
                  _resident((SGU_CHUNK, SGU_GROUPS)),
                  _resident((1, D_SGU))],
        out_specs=pl.BlockSpec((tm, D_SGU), lambda i: (i, 0)),
        out_shape=jax.ShapeDtypeStruct((tokens, D_SGU), BF16),
        scratch_shapes=[pltpu.VMEM((tm, D_SGU), F32)],
        compiler_params=_params("parallel"),
        name="sgu",
    )(h, w_sgu, ln_gain, ln_bias, w_spatial, b_spatial_t, out_gain)


def _attn_body(q_ref, k_ref, v_ref, o_ref):
    qi = pl.program_id(2)
    row0 = pl.multiple_of(qi * MOBA_BLOCK, MOBA_BLOCK)

    k_own = k_ref[pl.ds(row0, MOBA_BLOCK), :]
    s = _dot_nt(q_ref[:, 0:HEAD_DIM], k_own[:, 0:HEAD_DIM])
    r = lax.broadcasted_iota(jnp.int32, (MOBA_BLOCK, MOBA_BLOCK), 0)
    c = lax.broadcasted_iota(jnp.int32, (MOBA_BLOCK, MOBA_BLOCK), 1)
    s = jnp.where(c <= r, s, NEG_INF)
    m = jnp.max(s, axis=-1, keepdims=True)
    p = jnp.exp(s - m)
    l = jnp.sum(p, axis=-1, keepdims=True)
    acc = _dot(p.astype(BF16), v_ref[pl.ds(row0, MOBA_BLOCK), :])

    def past_chunk(ci, carry):
        m, l, acc = carry
        k0 = pl.multiple_of(ci * ATTN_KV_CHUNK, ATTN_KV_CHUNK)
        s = _dot_nt(q_ref[...], k_ref[pl.ds(k0, ATTN_KV_CHUNK), :])
        m_new = jnp.maximum(m, jnp.max(s, axis=-1, keepdims=True))
        alpha = jnp.exp(m - m_new)
        p = jnp.exp(s - m_new)
        l = alpha * l + jnp.sum(p, axis=-1, keepdims=True)
        acc = alpha * acc + _dot(p.astype(BF16), v_ref[pl.ds(k0, ATTN_KV_CHUNK), :])
        return m_new, l, acc

    blocks_per_chunk = ATTN_KV_CHUNK // MOBA_BLOCK
    n_chunks = (qi + blocks_per_chunk - 1) // blocks_per_chunk
    m, l, acc = lax.fori_loop(0, n_chunks, past_chunk, (m, l, acc))
    o_ref[...] = acc / l


def _attention(qa, ka, v, batch, seq):
    tokens = batch * seq
    q_tiles = seq // MOBA_BLOCK
    return pl.pallas_call(
        _attn_body,
        grid=(batch, ATTN_HEADS, q_tiles),
        in_specs=[pl.BlockSpec((MOBA_BLOCK, AUG_DIM), lambda b, h, i: (b * q_tiles + i, h)),
                  pl.BlockSpec((seq, AUG_DIM), lambda b, h, i: (b, h)),
                  pl.BlockSpec((seq, HEAD_DIM), lambda b, h, i: (b, h))],
        out_specs=pl.BlockSpec((MOBA_BLOCK, HEAD_DIM), lambda b, h, i: (b * q_tiles + i, h)),
        out_shape=jax.ShapeDtypeStruct((tokens, D_ATTN), F32),
        compiler_params=_params("parallel", "parallel", "arbitrary"),
        name="moba_attn",
    )(qa, ka, v)


def _outproj_body(a_ref, s_ref, x_ref, ga_ref, w_ref, o_ref):
    an = _rmsnorm(a_ref[...], ga_ref[...]).astype(BF16)
    o_ref[...] = (x_ref[...] + _dot(an, w_ref[0:D_ATTN, :])
                  + _dot(s_ref[...], w_ref[D_ATTN:D_ATTN + D_SGU, :]))


def _outproj(attn, sgu, x, attn_gain, w_out):
    tokens = x.shape[0]
    tm = PROJ_TM
    row = lambda width: pl.BlockSpec((tm, width), lambda i: (i, 0))
    return pl.pallas_call(
        _outproj_body,
        grid=(tokens // tm,),
        in_specs=[row(D_ATTN), row(D_SGU), row(D_MODEL), _resident((1, D_ATTN)),
                  _resident((D_ATTN + D_SGU, D_MODEL))],
        out_specs=row(D_MODEL),
        out_shape=jax.ShapeDtypeStruct((tokens, D_MODEL), F32),
        compiler_params=_params("parallel"),
        name="outproj",
    )(attn, sgu, x, attn_gain, w_out)


def kernel(x, ffn1_norm, ffn1_w_gate, ffn1_w_up, ffn1_w_down, mix_norm, w_in, q_norm, k_norm, sgu_ln_gain, sgu_ln_bias, sgu_w_spatial, sgu_b_spatial, attn_out_gain, sgu_out_gain, w_out, ffn2_norm, ffn2_w_gate, ffn2_w_up, ffn2_w_down):
    batch, seq, d_model = x.shape
    depth = ffn1_norm.shape[0]
    assert d_model == D_MODEL and seq % PROJ_TM == 0 and (batch * seq) % FFN_TM == 0
    xt = x.reshape(batch * seq, d_model)
    vec = lambda a: a.reshape(1, -1)
    for l in range(depth):
        xt, h = _ffn(xt, vec(ffn1_norm[l]), ffn1_w_gate[l].astype(BF16), ffn1_w_up[l].astype(BF16),
                     ffn1_w_down[l].astype(BF16), next_gain=vec(mix_norm[l]))
        w_in_bf = w_in[l].astype(BF16)
        qa, ka, v = _qkv(h, w_in_bf[:, :3 * D_ATTN], vec(q_norm[l]), vec(k_norm[l]), seq)
        sgu = _sgu(h, w_in_bf[:, 3 * D_ATTN:], vec(sgu_ln_gain[l]), vec(sgu_ln_bias[l]),
                   sgu_w_spatial[l], sgu_b_spatial[l].T, vec(sgu_out_gain[l]))
        attn = _attention(qa, ka, v, batch, seq)
        xt = _outproj(attn, sgu, xt, vec(attn_out_gain[l]), w_out[l].astype(BF16))
        xt = _ffn(xt, vec(ffn2_norm[l]), ffn2_w_gate[l].astype(BF16), ffn2_w_up[l].astype(BF16),
                  ffn2_w_down[l].astype(BF16))
    return xt.reshape(batch, seq, d_model)
```

```python
import functools

import jax
import jax.numpy as jnp
from jax import lax
from jax.experimental import pallas as pl
from jax.experimental.pallas import tpu as pltpu

D_MODEL = 2048
D_FF = 5632
ATTN_HEADS = 8
HEAD_DIM = 128
D_ATTN = ATTN_HEADS * HEAD_DIM
SGU_GROUPS = 8
SGU_GROUP_DIM = 128
D_SGU = SGU_GROUPS * SGU_GROUP_DIM
MOBA_BLOCK = 256
MOBA_TOPK = 3
SGU_CHUNK = 128
EPS = 1e-6
NEG_INF = -1e30

LANES = 128
AUG_DIM = 2 * HEAD_DIM
VMEM_LIMIT_BYTES = 56 * 1024 * 1024

FFN_TM = 512
FFN_TF = 512
PROJ_TM = 512
ATTN_TQ = 512
ATTN_KV_CHUNK = 512

F32 = jnp.float32
BF16 = jnp.bfloat16


def _rmsnorm(x, gain):
    return x * lax.rsqrt(jnp.mean(x * x, axis=-1, keepdims=True) + EPS) * gain


def _gelu_tanh(x):
    cdf = 0.5 * (1.0 + jnp.tanh(0.7978845608028654 * (x + 0.044715 * (x * x * x))))
    return x * cdf


def _dot(a, b):
    return jnp.dot(a, b, preferred_element_type=F32)


def _dot_nt(a, b, precision=None):
    return lax.dot_general(a, b, (((1,), (1,)), ((), ())), precision=precision,
                           preferred_element_type=F32)


def _params(*semantics):
    return pltpu.CompilerParams(dimension_semantics=semantics, vmem_limit_bytes=VMEM_LIMIT_BYTES)


def _resident(shape):
    return pl.BlockSpec(shape, lambda *_: (0,) * len(shape), pipeline_mode=pl.Buffered(1))


def _ffn_body(*refs, emit_norm, n_ff_steps):
    if emit_norm:
        x_ref, g_ref, wg_ref, wu_ref, wd_ref, g2_ref, o_ref, h2_ref, h_scr = refs
    else:
        x_ref, g_ref, wg_ref, wu_ref, wd_ref, o_ref, h_scr = refs
    j = pl.program_id(1)

    @pl.when(j == 0)
    def _init():
        x = x_ref[...]
        h_scr[...] = _rmsnorm(x, g_ref[...]).astype(BF16)
        o_ref[...] = x

    h = h_scr[...]
    gate = _dot(h, wg_ref[...])
    up = _dot(h, wu_ref[...])
    act = (gate / (1.0 + jnp.exp(-gate))) * up * 0.5
    o_ref[...] += _dot(act.astype(BF16), wd_ref[...])

    if emit_norm:
        @pl.when(j == n_ff_steps - 1)
        def _fin():
            h2_ref[...] = _rmsnorm(o_ref[...], g2_ref[...]).astype(BF16)


def _ffn(x, gain, w_gate, w_up, w_down, next_gain=None):
    tokens = x.shape[0]
    emit_norm = next_gain is not None
    n_ff_steps = D_FF // FFN_TF
    row = pl.BlockSpec((FFN_TM, D_MODEL), lambda i, j: (i, 0))
    vec = pl.BlockSpec((1, D_MODEL), lambda i, j: (0, 0))
    in_specs = [row, vec,
                pl.BlockSpec((D_MODEL, FFN_TF), lambda i, j: (0, j)),
                pl.BlockSpec((D_MODEL, FFN_TF), lambda i, j: (0, j)),
                pl.BlockSpec((FFN_TF, D_MODEL), lambda i, j: (j, 0))]
    args = [x, gain, w_gate, w_up, w_down]
    out_shape = jax.ShapeDtypeStruct((tokens, D_MODEL), F32)
    out_specs = row
    if emit_norm:
        in_specs.append(vec)
        args.append(next_gain)
        out_shape = (out_shape, jax.ShapeDtypeStruct((tokens, D_MODEL), BF16))
        out_specs = (row, row)
    return pl.pallas_call(
        functools.partial(_ffn_body, emit_norm=emit_norm, n_ff_steps=n_ff_steps),
        grid=(tokens // FFN_TM, n_ff_steps),
        in_specs=in_specs, out_specs=out_specs, out_shape=out_shape,
        scratch_shapes=[pltpu.VMEM((FFN_TM, D_MODEL), BF16)],
        compiler_params=_params("parallel", "arbitrary"),
        name="ffn_norm" if emit_norm else "ffn",
    )(*args)


def _qkv_body(h_ref, w_ref, gq_ref, gk_ref, qa_ref, ka_ref, v_ref, km_scr, *, tiles_per_batch):
    tm = PROJ_TM
    blocks_per_batch = tiles_per_batch * (tm // MOBA_BLOCK)
    i = pl.program_id(0)
    n0 = (i % tiles_per_batch) * (tm // MOBA_BLOCK)

    @pl.when(i == 0)
    def _init():
        km_scr[...] = jnp.zeros_like(km_scr)

    h = h_ref[...]
    q = _dot(h, w_ref[:, 0:D_ATTN])
    k = _dot(h, w_ref[:, D_ATTN:2 * D_ATTN])
    v_ref[...] = _dot(h, w_ref[:, 2 * D_ATTN:3 * D_ATTN]).astype(BF16)

    scale = HEAD_DIM ** -0.5
    blk_rows = lax.broadcasted_iota(jnp.int32, (blocks_per_batch, HEAD_DIM), 0)
    n_iota = lax.broadcasted_iota(jnp.int32, (blocks_per_batch, tm), 0)
    lane_tok = lax.broadcasted_iota(jnp.int32, (blocks_per_batch, tm), 1)
    q_blk = n0 + lane_tok // MOBA_BLOCK
    valid = n_iota < q_blk
    tok_blk = n0 + lax.broadcasted_iota(jnp.int32, (tm, LANES), 0) // MOBA_BLOCK
    onehot = jnp.where(lax.broadcasted_iota(jnp.int32, (tm, LANES), 1) == tok_blk, 1.0, 0.0).astype(BF16)
    pad_rows = jnp.zeros((LANES - blocks_per_batch, tm), F32)

    for hh in range(ATTN_HEADS):
        sl = slice(hh * HEAD_DIM, (hh + 1) * HEAD_DIM)
        qn = _rmsnorm(q[:, sl], gq_ref[...])
        kn = _rmsnorm(k[:, sl], gk_ref[...])

        kmean = km_scr[:, sl]
        for b in range(tm // MOBA_BLOCK):
            km = jnp.mean(kn[b * MOBA_BLOCK:(b + 1) * MOBA_BLOCK], axis=0, keepdims=True)
            kmean = jnp.where(blk_rows == n0 + b, km, kmean)
        km_scr[:, sl] = kmean

        gate = _dot_nt(kmean, qn, precision=lax.Precision.HIGHEST)
        cnt = jnp.zeros((blocks_per_batch, tm), jnp.int32)
        for m in range(blocks_per_batch - 1):
            gm = gate[m:m + 1, :]
            beats = (gm > gate) | ((gm == gate) & (m < n_iota))
            cnt = cnt + jnp.where(beats & (m < q_blk), 1, 0)
        sel = ((cnt < MOBA_TOPK) & valid) | (n_iota == q_blk)
        bias_t = jnp.concatenate([jnp.where(sel, 0.0, NEG_INF), pad_rows], axis=0)
        bias = bias_t.T

        a0 = hh * AUG_DIM
        qa_ref[:, a0:a0 + HEAD_DIM] = (qn * scale).astype(BF16)
        qa_ref[:, a0 + HEAD_DIM:a0 + AUG_DIM] = bias.astype(BF16)
        ka_ref[:, a0:a0 + HEAD_DIM] = kn.astype(BF16)
        ka_ref[:, a0 + HEAD_DIM:a0 + AUG_DIM] = onehot


def _qkv(h, w_qkv, q_gain, k_gain, seq):
    tokens = h.shape[0]
    tm = PROJ_TM
    tiles_per_batch = seq // tm
    blocks_per_batch = seq // MOBA_BLOCK
    assert blocks_per_batch <= LANES and blocks_per_batch % 8 == 0
    row = lambda width: pl.BlockSpec((tm, width), lambda i: (i, 0))
    return pl.pallas_call(
        functools.partial(_qkv_body, tiles_per_batch=tiles_per_batch),
        grid=(tokens // tm,),
        in_specs=[row(D_MODEL), _resident((D_MODEL, 3 * D_ATTN)),
                  _resident((1, HEAD_DIM)), _resident((1, HEAD_DIM))],
        out_specs=(row(ATTN_HEADS * AUG_DIM), row(ATTN_HEADS * AUG_DIM), row(D_ATTN)),
        out_shape=(jax.ShapeDtypeStruct((tokens, ATTN_HEADS * AUG_DIM), BF16),
                   jax.ShapeDtypeStruct((tokens, ATTN_HEADS * AUG_DIM), BF16),
                   jax.ShapeDtypeStruct((tokens, D_ATTN), BF16)),
        scratch_shapes=[pltpu.VMEM((blocks_per_batch, D_ATTN), F32)],
        compiler_params=_params("arbitrary"),
        name="qkv_gate",
    )(h, w_qkv, q_gain, k_gain)


def _sgu_body(h_ref, w_ref, lng_ref, lnb_ref, ws_ref, bs_ref, og_ref, o_ref, sg_scr):
    tm = PROJ_TM
    h = h_ref[...]
    u = _gelu_tanh(_dot(h, w_ref[:, 0:D_SGU]))
    g = _gelu_tanh(_dot(h, w_ref[:, D_SGU:2 * D_SGU]))
    r = lax.broadcasted_iota(jnp.int32, (SGU_CHUNK, SGU_CHUNK), 0)
    c = lax.broadcasted_iota(jnp.int32, (SGU_CHUNK, SGU_CHUNK), 1)
    for gi in range(SGU_GROUPS):
        sl = slice(gi * SGU_GROUP_DIM, (gi + 1) * SGU_GROUP_DIM)
        gg = g[:, sl]
        d = gg - jnp.mean(gg, axis=-1, keepdims=True)
        var = jnp.mean(d * d, axis=-1, keepdims=True)
        gn = (d * lax.rsqrt(var + EPS) * lng_ref[:, sl] + lnb_ref[:, sl]).astype(BF16)
        w_causal = jnp.where(r >= c, ws_ref[gi], 0.0).astype(BF16)
        b_col = bs_ref[:, gi:gi + 1]
        for ci in range(tm // SGU_CHUNK):
            rows = slice(ci * SGU_CHUNK, (ci + 1) * SGU_CHUNK)
            mixed = _dot(w_causal, gn[rows]) + b_col
            sg_scr[rows, sl] = u[rows, sl] * mixed
    o_ref[...] = _rmsnorm(sg_scr[...], og_ref[...]).astype(BF16)


def _sgu(h, w_sgu, ln_gain, ln_bias, w_spatial, b_spatial_t, out_gain):
    tokens = h.shape[0]
    tm = PROJ_TM
    return pl.pallas_call(
        _sgu_body,
        grid=(tokens // tm,),
        in_specs=[pl.BlockSpec((tm, D_MODEL), lambda i: (i, 0)),
                  _resident((D_MODEL, 2 * D_SGU)),
                  _resident((1, D_SGU)), _resident((1, D_SGU)),
                  _resident((SGU_GROUPS, SGU_CHUNK, SGU_CHUNK)),
                  _resident((SGU_CHUNK, SGU_GROUPS)),
                  _resident((1, D_SGU))],
        out_specs=pl.BlockSpec((tm, D_SGU), lambda i: (i, 0)),
        out_shape=jax.ShapeDtypeStruct((tokens, D_SGU), BF16),
        scratch_shapes=[pltpu.VMEM((tm, D_SGU), F32)],
        compiler_params=_params("parallel"),
        name="sgu",
    )(h, w_sgu, ln_gain, ln_bias, w_spatial, b_spatial_t, out_gain)


def _attn_tile(t, q_ref, k_ref, v_ref, o_ref, s_scr):
    tq, kc = ATTN_TQ, ATTN_KV_CHUNK
    lane_tiles = kc // LANES
    chunk = lambda c: slice(c * kc, (c + 1) * kc)

    m_part = None
    for c in range(t + 1):
        s = _dot_nt(q_ref[...], k_ref[chunk(c), :])
        if c == t:
            r = lax.broadcasted_iota(jnp.int32, (tq, kc), 0)
            col = lax.broadcasted_iota(jnp.int32, (tq, kc), 1)
            s = jnp.where(col <= r, s, NEG_INF)
        s_scr[:, chunk(c)] = s
        for j in range(lane_tiles):
            sj = s[:, j * LANES:(j + 1) * LANES]
            m_part = sj if m_part is None else jnp.maximum(m_part, sj)
    m = jnp.broadcast_to(jnp.max(m_part, axis=-1, keepdims=True), (tq, LANES))

    l_part = jnp.zeros((tq, LANES), F32)
    acc = jnp.zeros((tq, HEAD_DIM), F32)
    for c in range(t + 1):
        p_tiles = []
        for j in range(lane_tiles):
            pj = jnp.exp(s_scr[:, c * kc + j * LANES:c * kc + (j + 1) * LANES] - m)
            l_part = l_part + pj
            p_tiles.append(pj.astype(BF16))
        acc = acc + _dot(jnp.concatenate(p_tiles, axis=1), v_ref[chunk(c), :])
    o_ref[...] = acc / jnp.sum(l_part, axis=-1, keepdims=True)


def _attn_body(q_ref, k_ref, v_ref, o_ref, s_scr):
    t = pl.program_id(2)
    for ts in range(k_ref.shape[0] // ATTN_TQ):
        pl.when(t == ts)(functools.partial(_attn_tile, ts, q_ref, k_ref, v_ref, o_ref, s_scr))


def _attention(qa, ka, v, batch, seq):
    tokens = batch * seq
    q_tiles = seq // ATTN_TQ
    return pl.pallas_call(
        _attn_body,
        grid=(batch, ATTN_HEADS, q_tiles),
        in_specs=[pl.BlockSpec((ATTN_TQ, AUG_DIM), lambda b, h, i: (b * q_tiles + i, h)),
                  pl.BlockSpec((seq, AUG_DIM), lambda b, h, i: (b, h)),
                  pl.BlockSpec((seq, HEAD_DIM), lambda b, h, i: (b, h))],
        out_specs=pl.BlockSpec((ATTN_TQ, HEAD_DIM), lambda b, h, i: (b * q_tiles + i, h)),
        out_shape=jax.ShapeDtypeStruct((tokens, D_ATTN), F32),
        scratch_shapes=[pltpu.VMEM((ATTN_TQ, seq), F32)],
        compiler_params=_params("parallel", "parallel", "arbitrary"),
        name="moba_attn",
    )(qa, ka, v)


def _outproj_body(a_ref, s_ref, x_ref, ga_ref, w_ref, o_ref):
    an = _rmsnorm(a_ref[...], ga_ref[...]).astype(BF16)
    o_ref[...] = (x_ref[...] + _dot(an, w_ref[0:D_ATTN, :])
                  + _dot(s_ref[...], w_ref[D_ATTN:D_ATTN + D_SGU, :]))


def _outproj(attn, sgu, x, attn_gain, w_out):
    tokens = x.shape[0]
    tm = PROJ_TM
    row = lambda width: pl.BlockSpec((tm, width), lambda i: (i, 0))
    return pl.pallas_call(
        _outproj_body,
        grid=(tokens // tm,),
        in_specs=[row(D_ATTN), row(D_SGU), row(D_MODEL), _resident((1, D_ATTN)),
                  _resident((D_ATTN + D_SGU, D_MODEL))],
        out_specs=row(D_MODEL),
        out_shape=jax.ShapeDtypeStruct((tokens, D_MODEL), F32),
        compiler_params=_params("parallel"),
        name="outproj",
    )(attn, sgu, x, attn_gain, w_out)


def kernel(x, ffn1_norm, ffn1_w_gate, ffn1_w_up, ffn1_w_down, mix_norm, w_in, q_norm, k_norm, sgu_ln_gain, sgu_ln_bias, sgu_w_spatial, sgu_b_spatial, attn_out_gain, sgu_out_gain, w_out, ffn2_norm, ffn2_w_gate, ffn2_w_up, ffn2_w_down):
    batch, seq, d_model = x.shape
    depth = ffn1_norm.shape[0]
    assert d_model == D_MODEL and seq % PROJ_TM == 0 and (batch * seq) % FFN_TM == 0
    xt = x.reshape(batch * seq, d_model)
    vec = lambda a: a.reshape(1, -1)
    for l in range(depth):
        xt, h = _ffn(xt, vec(ffn1_norm[l]), ffn1_w_gate[l].astype(BF16), ffn1_w_up[l].astype(BF16),
                     ffn1_w_down[l].astype(BF16), next_gain=vec(mix_norm[l]))
        w_in_bf = w_in[l].astype(BF16)
        qa, ka, v = _qkv(h, w_in_bf[:, :3 * D_ATTN], vec(q_norm[l]), vec(k_norm[l]), seq)
        sgu = _sgu(h, w_in_bf[:, 3 * D_ATTN:], vec(sgu_ln_gain[l]), vec(sgu_ln_bias[l]),
                   sgu_w_spatial[l], sgu_b_spatial[l].T, vec(sgu_out_gain[l]))
        attn = _attention(qa, ka, v, batch, seq)
        xt = _outproj(attn, sgu, xt, vec(attn_out_gain[l]), w_out[l].astype(BF16))
        xt = _ffn(xt, vec(ffn2_norm[l]), ffn2_w_gate[l].astype(BF16), ffn2_w_up[l].astype(BF16),
                  ffn2_w_down[l].astype(BF16))
    return xt.reshape(batch, seq, d_model)
```

```python
import functools

import jax
import jax.numpy as jnp
from jax import lax
from jax.experimental import pallas as pl
from jax.experimental.pallas import tpu as pltpu

D_MODEL = 2048
D_FF = 5632
ATTN_HEADS = 8
HEAD_DIM = 128
D_ATTN = ATTN_HEADS * HEAD_DIM
SGU_GROUPS = 8
SGU_GROUP_DIM = 128
D_SGU = SGU_GROUPS * SGU_GROUP_DIM
MOBA_BLOCK = 256
MOBA_TOPK = 3
SGU_CHUNK = 128
EPS = 1e-6
NEG_INF = -1e30

LANES = 128
AUG_DIM = 2 * HEAD_DIM
VMEM_LIMIT_BYTES = 56 * 1024 * 1024

FFN_TM = 512
FFN_TF = 512
PROJ_TM = 512
ATTN_TQ = 512
ATTN_KV_CHUNK = 512

F32 = jnp.float32
BF16 = jnp.bfloat16


def _rmsnorm(x, gain):
    return x * lax.rsqrt(jnp.mean(x * x, axis=-1, keepdims=True) + EPS) * gain


def _gelu_tanh(x):
    cdf = 0.5 * (1.0 + jnp.tanh(0.7978845608028654 * (x + 0.044715 * (x * x * x))))
    return x * cdf


def _dot(a, b):
    return jnp.dot(a, b, preferred_element_type=F32)


def _dot_nt(a, b, precision=None):
    return lax.dot_general(a, b, (((1,), (1,)), ((), ())), precision=precision,
                           preferred_element_type=F32)


def _params(*semantics):
    return pltpu.CompilerParams(dimension_semantics=semantics, vmem_limit_bytes=VMEM_LIMIT_BYTES)


def _resident(shape):
    return pl.BlockSpec(shape, lambda *_: (0,) * len(shape), pipeline_mode=pl.Buffered(1))


def _ffn_body(*refs, emit_norm, n_ff_steps):
    if emit_norm:
        x_ref, g_ref, wg_ref, wu_ref, wd_ref, g2_ref, o_ref, h2_ref, h_scr = refs
    else:
        x_ref, g_ref, wg_ref, wu_ref, wd_ref, o_ref, h_scr = refs
    j = pl.program_id(1)

    @pl.when(j == 0)
    def _init():
        x = x_ref[...]
        h_scr[...] = _rmsnorm(x, g_ref[...]).astype(BF16)
        o_ref[...] = x

    h = h_scr[...]
    gate = _dot(h, wg_ref[...])
    up = _dot(h, wu_ref[...])
    act = (gate / (1.0 + jnp.exp(-gate))) * up * 0.5
    o_ref[...] += _dot(act.astype(BF16), wd_ref[...])

    if emit_norm:
        @pl.when(j == n_ff_steps - 1)
        def _fin():
            h2_ref[...] = _rmsnorm(o_ref[...], g2_ref[...]).astype(BF16)


def _ffn(x, gain, w_gate, w_up, w_down, next_gain=None):
    tokens = x.shape[0]
    emit_norm = next_gain is not None
    n_ff_steps = D_FF // FFN_TF
    row = pl.BlockSpec((FFN_TM, D_MODEL), lambda i, j: (i, 0))
    vec = pl.BlockSpec((1, D_MODEL), lambda i, j: (0, 0))
    in_specs = [row, vec,
                pl.BlockSpec((D_MODEL, FFN_TF), lambda i, j: (0, j)),
                pl.BlockSpec((D_MODEL, FFN_TF), lambda i, j: (0, j)),
                pl.BlockSpec((FFN_TF, D_MODEL), lambda i, j: (j, 0))]
    args = [x, gain, w_gate, w_up, w_down]
    out_shape = jax.ShapeDtypeStruct((tokens, D_MODEL), F32)
    out_specs = row
    if emit_norm:
        in_specs.append(vec)
        args.append(next_gain)
        out_shape = (out_shape, jax.ShapeDtypeStruct((tokens, D_MODEL), BF16))
        out_specs = (row, row)
    return pl.pallas_call(
        functools.partial(_ffn_body, emit_norm=emit_norm, n_ff_steps=n_ff_steps),
        grid=(tokens // FFN_TM, n_ff_steps),
        in_specs=in_specs, out_specs=out_specs, out_shape=out_shape,
        scratch_shapes=[pltpu.VMEM((FFN_TM, D_MODEL), BF16)],
        compiler_params=_params("parallel", "arbitrary"),
        name="ffn_norm" if emit_norm else "ffn",
    )(*args)


def _qkv_body(h_ref, w_ref, gq_ref, gk_ref, qa_ref, ka_ref, v_ref, km_scr, *, tiles_per_batch):
    tm = PROJ_TM
    blocks_per_batch = tiles_per_batch * (tm // MOBA_BLOCK)
    i = pl.program_id(0)
    n0 = (i % tiles_per_batch) * (tm // MOBA_BLOCK)

    @pl.when(i == 0)
    def _init():
        km_scr[...] = jnp.zeros_like(km_scr)

    h = h_ref[...]
    q = _dot(h, w_ref[:, 0:D_ATTN])
    k = _dot(h, w_ref[:, D_ATTN:2 * D_ATTN])
    v_ref[...] = _dot(h, w_ref[:, 2 * D_ATTN:3 * D_ATTN]).astype(BF16)

    scale = HEAD_DIM ** -0.5
    blk_rows = lax.broadcasted_iota(jnp.int32, (blocks_per_batch, HEAD_DIM), 0)
    n_iota = lax.broadcasted_iota(jnp.int32, (blocks_per_batch, tm), 0)
    lane_tok = lax.broadcasted_iota(jnp.int32, (blocks_per_batch, tm), 1)
    q_blk = n0 + lane_tok // MOBA_BLOCK
    valid = n_iota < q_blk
    tok_blk = n0 + lax.broadcasted_iota(jnp.int32, (tm, LANES), 0) // MOBA_BLOCK
    onehot = jnp.where(lax.broadcasted_iota(jnp.int32, (tm, LANES), 1) == tok_blk, 1.0, 0.0).astype(BF16)
    pad_rows = jnp.zeros((LANES - blocks_per_batch, tm), F32)

    for hh in range(ATTN_HEADS):
        sl = slice(hh * HEAD_DIM, (hh + 1) * HEAD_DIM)
        qn = _rmsnorm(q[:, sl], gq_ref[...])
        kn = _rmsnorm(k[:, sl], gk_ref[...])

        kmean = km_scr[:, sl]
        for b in range(tm // MOBA_BLOCK):
            km = jnp.mean(kn[b * MOBA_BLOCK:(b + 1) * MOBA_BLOCK], axis=0, keepdims=True)
            kmean = jnp.where(blk_rows == n0 + b, km, kmean)
        km_scr[:, sl] = kmean

        gate = _dot_nt(kmean, qn, precision=lax.Precision.HIGHEST)
        cnt = jnp.zeros((blocks_per_batch, tm), jnp.int32)
        for m in range(blocks_per_batch - 1):
            gm = gate[m:m + 1, :]
            beats = (gm > gate) | ((gm == gate) & (m < n_iota))
            cnt = cnt + jnp.where(beats & (m < q_blk), 1, 0)
        sel = ((cnt < MOBA_TOPK) & valid) | (n_iota == q_blk)
        bias_t = jnp.concatenate([jnp.where(sel, 0.0, NEG_INF), pad_rows], axis=0)
        bias = bias_t.T

        a0 = hh * AUG_DIM
        qa_ref[:, a0:a0 + HEAD_DIM] = (qn * scale).astype(BF16)
        qa_ref[:, a0 + HEAD_DIM:a0 + AUG_DIM] = bias.astype(BF16)
        ka_ref[:, a0:a0 + HEAD_DIM] = kn.astype(BF16)
        ka_ref[:, a0 + HEAD_DIM:a0 + AUG_DIM] = onehot


def _qkv(h, w_qkv, q_gain, k_gain, seq):
    tokens = h.shape[0]
    tm = PROJ_TM
    tiles_per_batch = seq // tm
    blocks_per_batch = seq // MOBA_BLOCK
    assert blocks_per_batch <= LANES and blocks_per_batch % 8 == 0
    row = lambda width: pl.BlockSpec((tm, width), lambda i: (i, 0))
    return pl.pallas_call(
        functools.partial(_qkv_body, tiles_per_batch=tiles_per_batch),
        grid=(tokens // tm,),
        in_specs=[row(D_MODEL), _resident((D_MODEL, 3 * D_ATTN)),
                  _resident((1, HEAD_DIM)), _resident((1, HEAD_DIM))],
        out_specs=(row(ATTN_HEADS * AUG_DIM), row(ATTN_HEADS * AUG_DIM), row(D_ATTN)),
        out_shape=(jax.ShapeDtypeStruct((tokens, ATTN_HEADS * AUG_DIM), BF16),
                   jax.ShapeDtypeStruct((tokens, ATTN_HEADS * AUG_DIM), BF16),
                   jax.ShapeDtypeStruct((tokens, D_ATTN), BF16)),
        scratch_shapes=[pltpu.VMEM((blocks_per_batch, D_ATTN), F32)],
        compiler_params=_params("arbitrary"),
        name="qkv_gate",
    )(h, w_qkv, q_gain, k_gain)


def _sgu_body(h_ref, w_ref, lng_ref, lnb_ref, ws_ref, bs_ref, og_ref, o_ref, sg_scr):
    tm = PROJ_TM
    h = h_ref[...]
    u = _gelu_tanh(_dot(h, w_ref[:, 0:D_SGU]))
    g = _gelu_tanh(_dot(h, w_ref[:, D_SGU:2 * D_SGU]))
    r = lax.broadcasted_iota(jnp.int32, (SGU_CHUNK, SGU_CHUNK), 0)
    c = lax.broadcasted_iota(jnp.int32, (SGU_CHUNK, SGU_CHUNK), 1)
    for gi in range(SGU_GROUPS):
        sl = slice(gi * SGU_GROUP_DIM, (gi + 1) * SGU_GROUP_DIM)
        gg = g[:, sl]
        d = gg - jnp.mean(gg, axis=-1, keepdims=True)
        var = jnp.mean(d * d, axis=-1, keepdims=True)
        gn = (d * lax.rsqrt(var + EPS) * lng_ref[:, sl] + lnb_ref[:, sl]).astype(BF16)
        w_causal = jnp.where(r >= c, ws_ref[gi], 0.0).astype(BF16)
        b_col = bs_ref[:, gi:gi + 1]
        for ci in range(tm // SGU_CHUNK):
            rows = slice(ci * SGU_CHUNK, (ci + 1) * SGU_CHUNK)
            mixed = _dot(w_causal, gn[rows]) + b_col
            sg_scr[rows, sl] = u[rows, sl] * mixed
    o_ref[...] = _rmsnorm(sg_scr[...], og_ref[...]).astype(BF16)


def _sgu(h, w_sgu, ln_gain, ln_bias, w_spatial, b_spatial_t, out_gain):
    tokens = h.shape[0]
    tm = PROJ_TM
    return pl.pallas_call(
        _sgu_body,
        grid=(tokens // tm,),
        in_specs=[pl.BlockSpec((tm, D_MODEL), lambda i: (i, 0)),
                  _resident((D_MODEL, 2 * D_SGU)),
                  _resident((1, D_SGU)), _resident((1, D_SGU)),
                  _resident((SGU_GROUPS, SGU_CHUNK, SGU_CHUNK)),
                  _resident((SGU_CHUNK, SGU_GROUPS)),
                  _resident((1, D_SGU))],
        out_specs=pl.BlockSpec((tm, D_SGU), lambda i: (i, 0)),
        out_shape=jax.ShapeDtypeStruct((tokens, D_SGU), BF16),
        scratch_shapes=[pltpu.VMEM((tm, D_SGU), F32)],
        compiler_params=_params("parallel"),
        name="sgu",
    )(h, w_sgu, ln_gain, ln_bias, w_spatial, b_spatial_t, out_gain)


def _attn_scores(t, q_ref, k_ref, s_ref, m_ref):
    tq, kc = ATTN_TQ, ATTN_KV_CHUNK
    m_part = None
    for c in range(t + 1):
        s = _dot_nt(q_ref[...], k_ref[c * kc:(c + 1) * kc, :])
        if c == t:
            r = lax.broadcasted_iota(jnp.int32, (tq, kc), 0)
            col = lax.broadcasted_iota(jnp.int32, (tq, kc), 1)
            s = jnp.where(col <= r, s, NEG_INF)
        s_ref[:, c * kc:(c + 1) * kc] = s
        for j in range(kc // LANES):
            sj = s[:, j * LANES:(j + 1) * LANES]
            m_part = sj if m_part is None else jnp.maximum(m_part, sj)
    m_ref[...] = jnp.broadcast_to(jnp.max(m_part, axis=-1, keepdims=True), (tq, LANES))


def _attn_values(t, v_ref, o_ref, s_ref, m_ref):
    tq, kc = ATTN_TQ, ATTN_KV_CHUNK
    m = m_ref[...]
    l_part = jnp.zeros((tq, LANES), F32)
    acc = jnp.zeros((tq, HEAD_DIM), F32)
    for c in range(t + 1):
        p_tiles = []
        for j in range(kc // LANES):
            pj = jnp.exp(s_ref[:, c * kc + j * LANES:c * kc + (j + 1) * LANES] - m)
            l_part = l_part + pj
            p_tiles.append(pj.astype(BF16))
        acc = acc + _dot(jnp.concatenate(p_tiles, axis=1), v_ref[c * kc:(c + 1) * kc, :])
    o_ref[...] = acc / jnp.sum(l_part, axis=-1, keepdims=True)


def _attn_body(q_ref, k_ref, v_ref, o_ref, s_scr, m_scr):
    t = pl.program_id(2)
    n_tiles = k_ref.shape[0] // ATTN_TQ

    def step(ts):
        if ts == 0:
            o_ref[...] = jnp.zeros_like(o_ref)
        else:
            _attn_values(ts - 1, v_ref, o_ref, s_scr.at[(ts - 1) % 2], m_scr.at[(ts - 1) % 2])
        if ts < n_tiles:
            _attn_scores(ts, q_ref, k_ref, s_scr.at[ts % 2], m_scr.at[ts % 2])

    for ts in range(n_tiles + 1):
        pl.when(t == ts)(functools.partial(step, ts))


def _attention(qa, ka, v, batch, seq):
    tokens = batch * seq
    q_tiles = seq // ATTN_TQ
    return pl.pallas_call(
        _attn_body,
        grid=(batch, ATTN_HEADS, q_tiles + 1),
        in_specs=[pl.BlockSpec((ATTN_TQ, AUG_DIM),
                               lambda b, h, i: (b * q_tiles + jnp.minimum(i, q_tiles - 1), h)),
                  pl.BlockSpec((seq, AUG_DIM), lambda b, h, i: (b, h)),
                  pl.BlockSpec((seq, HEAD_DIM), lambda b, h, i: (b, h))],
        out_specs=pl.BlockSpec((ATTN_TQ, HEAD_DIM),
                               lambda b, h, i: (b * q_tiles + jnp.maximum(i - 1, 0), h)),
        out_shape=jax.ShapeDtypeStruct((tokens, D_ATTN), F32),
        scratch_shapes=[pltpu.VMEM((2, ATTN_TQ, seq), F32), pltpu.VMEM((2, ATTN_TQ, LANES), F32)],
        compiler_params=_params("parallel", "parallel", "arbitrary"),
        name="moba_attn",
    )(qa, ka, v)


def _outproj_body(a_ref, s_ref, x_ref, ga_ref, w_ref, o_ref):
    an = _rmsnorm(a_ref[...], ga_ref[...]).astype(BF16)
    o_ref[...] = (x_ref[...] + _dot(an, w_ref[0:D_ATTN, :])
                  + _dot(s_ref[...], w_ref[D_ATTN:D_ATTN + D_SGU, :]))


def _outproj(attn, sgu, x, attn_gain, w_out):
    tokens = x.shape[0]
    tm = PROJ_TM
    row = lambda width: pl.BlockSpec((tm, width), lambda i: (i, 0))
    return pl.pallas_call(
        _outproj_body,
        grid=(tokens // tm,),
        in_specs=[row(D_ATTN), row(D_SGU), row(D_MODEL), _resident((1, D_ATTN)),
                  _resident((D_ATTN + D_SGU, D_MODEL))],
        out_specs=row(D_MODEL),
        out_shape=jax.ShapeDtypeStruct((tokens, D_MODEL), F32),
        compiler_params=_params("parallel"),
        name="outproj",
    )(attn, sgu, x, attn_gain, w_out)


def kernel(x, ffn1_norm, ffn1_w_gate, ffn1_w_up, ffn1_w_down, mix_norm, w_in, q_norm, k_norm, sgu_ln_gain, sgu_ln_bias, sgu_w_spatial, sgu_b_spatial, attn_out_gain, sgu_out_gain, w_out, ffn2_norm, ffn2_w_gate, ffn2_w_up, ffn2_w_down):
    batch, seq, d_model = x.shape
    depth = ffn1_norm.shape[0]
    assert d_model == D_MODEL and seq % PROJ_TM == 0 and (batch * seq) % FFN_TM == 0
    xt = x.reshape(batch * seq, d_model)
    vec = lambda a: a.reshape(1, -1)
    for l in range(depth):
        xt, h = _ffn(xt, vec(ffn1_norm[l]), ffn1_w_gate[l].astype(BF16), ffn1_w_up[l].astype(BF16),
                     ffn1_w_down[l].astype(BF16), next_gain=vec(mix_norm[l]))
        w_in_bf = w_in[l].astype(BF16)
        qa, ka, v = _qkv(h, w_in_bf[:, :3 * D_ATTN], vec(q_norm[l]), vec(k_norm[l]), seq)
        sgu = _sgu(h, w_in_bf[:, 3 * D_ATTN:], vec(sgu_ln_gain[l]), vec(sgu_ln_bias[l]),
                   sgu_w_spatial[l], sgu_b_spatial[l].T, vec(sgu_out_gain[l]))
        attn = _attention(qa, ka, v, batch, seq)
        xt = _outproj(attn, sgu, xt, vec(attn_out_gain[l]), w_out[l].astype(BF16))
        xt = _ffn(xt, vec(ffn2_norm[l]), ffn2_w_gate[l].astype(BF16), ffn2_w_up[l].astype(BF16),
                  ffn2_w_down[l].astype(BF16))
    return xt.reshape(batch, seq, d_model)
```

```python
import functools

import jax
import jax.numpy as jnp
from jax import lax
from jax.experimental import pallas as pl
from jax.experimental.pallas import tpu as pltpu

D_MODEL = 2048
D_FF = 5632
ATTN_HEADS = 8
HEAD_DIM = 128
D_ATTN = ATTN_HEADS * HEAD_DIM
SGU_GROUPS = 8
SGU_GROUP_DIM = 128
D_SGU = SGU_GROUPS * SGU_GROUP_DIM
MOBA_BLOCK = 256
MOBA_TOPK = 3
SGU_CHUNK = 128
EPS = 1e-6
NEG_INF = -1e30

LANES = 128
AUG_DIM = 2 * HEAD_DIM
VMEM_LIMIT_BYTES = 56 * 1024 * 1024

FFN_TM = 1024
FFN_TF = 256
PROJ_TM = 512
ATTN_TQ = 512
ATTN_KV_CHUNK = 512

F32 = jnp.float32
BF16 = jnp.bfloat16


def _rmsnorm(x, gain):
    return x * lax.rsqrt(jnp.mean(x * x, axis=-1, keepdims=True) + EPS) * gain


def _gelu_tanh(x):
    cdf = 0.5 * (1.0 + jnp.tanh(0.7978845608028654 * (x + 0.044715 * (x * x * x))))
    return x * cdf


def _dot(a, b):
    return jnp.dot(a, b, preferred_element_type=F32)


def _dot_nt(a, b, precision=None):
    return lax.dot_general(a, b, (((1,), (1,)), ((), ())), precision=precision,
                           preferred_element_type=F32)


def _params(*semantics):
    return pltpu.CompilerParams(dimension_semantics=semantics, vmem_limit_bytes=VMEM_LIMIT_BYTES)


def _resident(shape):
    return pl.BlockSpec(shape, lambda *_: (0,) * len(shape), pipeline_mode=pl.Buffered(1))


def _ffn_body(x_ref, g_ref, wg_ref, wu_ref, wd_ref, o_ref, h_scr):
    j = pl.program_id(1)

    @pl.when(j == 0)
    def _init():
        x = x_ref[...]
        h_scr[...] = _rmsnorm(x, g_ref[...]).astype(BF16)
        o_ref[...] = x

    h = h_scr[...]
    gate = _dot(h, wg_ref[...].astype(BF16))
    up = _dot(h, wu_ref[...].astype(BF16))
    act = (gate / (1.0 + jnp.exp(-gate))) * up * 0.5
    o_ref[...] += _dot(act.astype(BF16), wd_ref[...].astype(BF16))


def _ffn(x, gain, w_gate, w_up, w_down):
    tokens = x.shape[0]
    row = pl.BlockSpec((FFN_TM, D_MODEL), lambda i, j: (i, 0))
    return pl.pallas_call(
        _ffn_body,
        grid=(tokens // FFN_TM, D_FF // FFN_TF),
        in_specs=[row, pl.BlockSpec((1, D_MODEL), lambda i, j: (0, 0)),
                  pl.BlockSpec((D_MODEL, FFN_TF), lambda i, j: (0, j)),
                  pl.BlockSpec((D_MODEL, FFN_TF), lambda i, j: (0, j)),
                  pl.BlockSpec((FFN_TF, D_MODEL), lambda i, j: (j, 0))],
        out_specs=row,
        out_shape=jax.ShapeDtypeStruct((tokens, D_MODEL), F32),
        scratch_shapes=[pltpu.VMEM((FFN_TM, D_MODEL), BF16)],
        compiler_params=_params("parallel", "arbitrary"),
        name="ffn",
    )(x, gain, w_gate, w_up, w_down)


def _qkv_body(x_ref, gx_ref, w_ref, gq_ref, gk_ref, qa_ref, ka_ref, v_ref, km_scr, *, tiles_per_batch):
    tm = PROJ_TM
    blocks_per_batch = tiles_per_batch * (tm // MOBA_BLOCK)
    i = pl.program_id(0)
    n0 = (i % tiles_per_batch) * (tm // MOBA_BLOCK)

    @pl.when(i == 0)
    def _init():
        km_scr[...] = jnp.zeros_like(km_scr)

    h = _rmsnorm(x_ref[...], gx_ref[...]).astype(BF16)
    q = _dot(h, w_ref[:, 0:D_ATTN])
    k = _dot(h, w_ref[:, D_ATTN:2 * D_ATTN])
    v_ref[...] = _dot(h, w_ref[:, 2 * D_ATTN:3 * D_ATTN]).astype(BF16)

    scale = HEAD_DIM ** -0.5
    blk_rows = lax.broadcasted_iota(jnp.int32, (blocks_per_batch, HEAD_DIM), 0)
    n_iota = lax.broadcasted_iota(jnp.int32, (blocks_per_batch, tm), 0)
    lane_tok = lax.broadcasted_iota(jnp.int32, (blocks_per_batch, tm), 1)
    q_blk = n0 + lane_tok // MOBA_BLOCK
    valid = n_iota < q_blk
    tok_blk = n0 + lax.broadcasted_iota(jnp.int32, (tm, LANES), 0) // MOBA_BLOCK
    onehot = jnp.where(lax.broadcasted_iota(jnp.int32, (tm, LANES), 1) == tok_blk, 1.0, 0.0).astype(BF16)
    pad_rows = jnp.zeros((LANES - blocks_per_batch, tm), F32)

    for hh in range(ATTN_HEADS):
        sl = slice(hh * HEAD_DIM, (hh + 1) * HEAD_DIM)
        qn = _rmsnorm(q[:, sl], gq_ref[...])
        kn = _rmsnorm(k[:, sl], gk_ref[...])

        kmean = km_scr[:, sl]
        for b in range(tm // MOBA_BLOCK):
            km = jnp.mean(kn[b * MOBA_BLOCK:(b + 1) * MOBA_BLOCK], axis=0, keepdims=True)
            kmean = jnp.where(blk_rows == n0 + b, km, kmean)
        km_scr[:, sl] = kmean

        gate = _dot_nt(kmean, qn, precision=lax.Precision.HIGHEST)
        cnt = jnp.zeros((blocks_per_batch, tm), jnp.int32)
        for m in range(blocks_per_batch - 1):
            gm = gate[m:m + 1, :]
            beats = (gm > gate) | ((gm == gate) & (m < n_iota))
            cnt = cnt + jnp.where(beats & (m < q_blk), 1, 0)
        sel = ((cnt < MOBA_TOPK) & valid) | (n_iota == q_blk)
        bias_t = jnp.concatenate([jnp.where(sel, 0.0, NEG_INF), pad_rows], axis=0)
        bias = bias_t.T

        a0 = hh * AUG_DIM
        qa_ref[:, a0:a0 + HEAD_DIM] = (qn * scale).astype(BF16)
        qa_ref[:, a0 + HEAD_DIM:a0 + AUG_DIM] = bias.astype(BF16)
        ka_ref[:, a0:a0 + HEAD_DIM] = kn.astype(BF16)
        ka_ref[:, a0 + HEAD_DIM:a0 + AUG_DIM] = onehot


def _qkv(x, x_gain, w_qkv, q_gain, k_gain, seq):
    tokens = x.shape[0]
    tm = PROJ_TM
    tiles_per_batch = seq // tm
    blocks_per_batch = seq // MOBA_BLOCK
    assert blocks_per_batch <= LANES and blocks_per_batch % 8 == 0
    row = lambda width: pl.BlockSpec((tm, width), lambda i: (i, 0))
    return pl.pallas_call(
        functools.partial(_qkv_body, tiles_per_batch=tiles_per_batch),
        grid=(tokens // tm,),
        in_specs=[row(D_MODEL), _resident((1, D_MODEL)), _resident((D_MODEL, 3 * D_ATTN)),
                  _resident((1, HEAD_DIM)), _resident((1, HEAD_DIM))],
        out_specs=(row(ATTN_HEADS * AUG_DIM), row(ATTN_HEADS * AUG_DIM), row(D_ATTN)),
        out_shape=(jax.ShapeDtypeStruct((tokens, ATTN_HEADS * AUG_DIM), BF16),
                   jax.ShapeDtypeStruct((tokens, ATTN_HEADS * AUG_DIM), BF16),
                   jax.ShapeDtypeStruct((tokens, D_ATTN), BF16)),
        scratch_shapes=[pltpu.VMEM((blocks_per_batch, D_ATTN), F32)],
        compiler_params=_params("arbitrary"),
        name="qkv_gate",
    )(x, x_gain, w_qkv, q_gain, k_gain)


def _sgu_body(x_ref, gx_ref, w_ref, lng_ref, lnb_ref, ws_ref, bs_ref, og_ref, o_ref, sg_scr):
    tm = PROJ_TM
    h = _rmsnorm(x_ref[...], gx_ref[...]).astype(BF16)
    u = _gelu_tanh(_dot(h, w_ref[:, 0:D_SGU]))
    g = _gelu_tanh(_dot(h, w_ref[:, D_SGU:2 * D_SGU]))
    r = lax.broadcasted_iota(jnp.int32, (SGU_CHUNK, SGU_CHUNK), 0)
    c = lax.broadcasted_iota(jnp.int32, (SGU_CHUNK, SGU_CHUNK), 1)
    for gi in range(SGU_GROUPS):
        sl = slice(gi * SGU_GROUP_DIM, (gi + 1) * SGU_GROUP_DIM)
        gg = g[:, sl]
        d = gg - jnp.mean(gg, axis=-1, keepdims=True)
        var = jnp.mean(d * d, axis=-1, keepdims=True)
        gn = (d * lax.rsqrt(var + EPS) * lng_ref[:, sl] + lnb_ref[:, sl]).astype(BF16)
        w_causal = jnp.where(r >= c, ws_ref[gi], 0.0).astype(BF16)
        b_col = bs_ref[:, gi:gi + 1]
        for ci in range(tm // SGU_CHUNK):
            rows = slice(ci * SGU_CHUNK, (ci + 1) * SGU_CHUNK)
            mixed = _dot(w_causal, gn[rows]) + b_col
            sg_scr[rows, sl] = u[rows, sl] * mixed
    o_ref[...] = _rmsnorm(sg_scr[...], og_ref[...]).astype(BF16)


def _sgu(x, x_gain, w_sgu, ln_gain, ln_bias, w_spatial, b_spatial_t, out_gain):
    tokens = x.shape[0]
    tm = PROJ_TM
    return pl.pallas_call(
        _sgu_body,
        grid=(tokens // tm,),
        in_specs=[pl.BlockSpec((tm, D_MODEL), lambda i: (i, 0)), _resident((1, D_MODEL)),
                  _resident((D_MODEL, 2 * D_SGU)),
                  _resident((1, D_SGU)), _resident((1, D_SGU)),
                  _resident((SGU_GROUPS, SGU_CHUNK, SGU_CHUNK)),
                  _resident((SGU_CHUNK, SGU_GROUPS)),
                  _resident((1, D_SGU))],
        out_specs=pl.BlockSpec((tm, D_SGU), lambda i: (i, 0)),
        out_shape=jax.ShapeDtypeStruct((tokens, D_SGU), BF16),
        scratch_shapes=[pltpu.VMEM((tm, D_SGU), F32)],
        compiler_params=_params("parallel"),
        name="sgu",
    )(x, x_gain, w_sgu, ln_gain, ln_bias, w_spatial, b_spatial_t, out_gain)


def _attn_scores(t, q_ref, k_ref, s_ref, m_ref):
    tq, kc = ATTN_TQ, ATTN_KV_CHUNK
    m_part = None
    for c in range(t + 1):
        s = _dot_nt(q_ref[...], k_ref[c * kc:(c + 1) * kc, :])
        if c == t:
            r = lax.broadcasted_iota(jnp.int32, (tq, kc), 0)
            col = lax.broadcasted_iota(jnp.int32, (tq, kc), 1)
            s = jnp.where(col <= r, s, NEG_INF)
        s_ref[:, c * kc:(c + 1) * kc] = s
        for j in range(kc // LANES):
            sj = s[:, j * LANES:(j + 1) * LANES]
            m_part = sj if m_part is None else jnp.maximum(m_part, sj)
    m_ref[...] = jnp.broadcast_to(jnp.max(m_part, axis=-1, keepdims=True), (tq, LANES))


def _attn_values(t, v_ref, o_ref, s_ref, m_ref):
    tq, kc = ATTN_TQ, ATTN_KV_CHUNK
    m = m_ref[...]
    l_part = jnp.zeros((tq, LANES), F32)
    acc = jnp.zeros((tq, HEAD_DIM), F32)
    for c in range(t + 1):
        p_tiles = []
        for j in range(kc // LANES):
            pj = jnp.exp(s_ref[:, c * kc + j * LANES:c * kc + (j + 1) * LANES] - m)
            l_part = l_part + pj
            p_tiles.append(pj.astype(BF16))
        acc = acc + _dot(jnp.concatenate(p_tiles, axis=1), v_ref[c * kc:(c + 1) * kc, :])
    o_ref[...] = acc / jnp.sum(l_part, axis=-1, keepdims=True)


def _attn_body(q_ref, k_ref, v_ref, o_ref, s_scr, m_scr):
    t = pl.program_id(2)
    n_tiles = k_ref.shape[0] // ATTN_TQ

    def step(ts):
        if ts == 0:
            o_ref[...] = jnp.zeros_like(o_ref)
        else:
            _attn_values(ts - 1, v_ref, o_ref, s_scr.at[(ts - 1) % 2], m_scr.at[(ts - 1) % 2])
        if ts < n_tiles:
            _attn_scores(ts, q_ref, k_ref, s_scr.at[ts % 2], m_scr.at[ts % 2])

    for ts in range(n_tiles + 1):
        pl.when(t == ts)(functools.partial(step, ts))


def _attention(qa, ka, v, batch, seq):
    tokens = batch * seq
    q_tiles = seq // ATTN_TQ
    return pl.pallas_call(
        _attn_body,
        grid=(batch, ATTN_HEADS, q_tiles + 1),
        in_specs=[pl.BlockSpec((ATTN_TQ, AUG_DIM),
                               lambda b, h, i: (b * q_tiles + jnp.minimum(i, q_tiles - 1), h)),
                  pl.BlockSpec((seq, AUG_DIM), lambda b, h, i: (b, h)),
                  pl.BlockSpec((seq, HEAD_DIM), lambda b, h, i: (b, h))],
        out_specs=pl.BlockSpec((ATTN_TQ, HEAD_DIM),
                               lambda b, h, i: (b * q_tiles + jnp.maximum(i - 1, 0), h)),
        out_shape=jax.ShapeDtypeStruct((tokens, D_ATTN), F32),
        scratch_shapes=[pltpu.VMEM((2, ATTN_TQ, seq), F32), pltpu.VMEM((2, ATTN_TQ, LANES), F32)],
        compiler_params=_params("parallel", "parallel", "arbitrary"),
        name="moba_attn",
    )(qa, ka, v)


def _outproj_body(a_ref, s_ref, x_ref, ga_ref, w_ref, o_ref):
    an = _rmsnorm(a_ref[...], ga_ref[...]).astype(BF16)
    o_ref[...] = (x_ref[...] + _dot(an, w_ref[0:D_ATTN, :])
                  + _dot(s_ref[...], w_ref[D_ATTN:D_ATTN + D_SGU, :]))


def _outproj(attn, sgu, x, attn_gain, w_out):
    tokens = x.shape[0]
    tm = PROJ_TM
    row = lambda width: pl.BlockSpec((tm, width), lambda i: (i, 0))
    return pl.pallas_call(
        _outproj_body,
        grid=(tokens // tm,),
        in_specs=[row(D_ATTN), row(D_SGU), row(D_MODEL), _resident((1, D_ATTN)),
                  _resident((D_ATTN + D_SGU, D_MODEL))],
        out_specs=row(D_MODEL),
        out_shape=jax.ShapeDtypeStruct((tokens, D_MODEL), F32),
        compiler_params=_params("parallel"),
        name="outproj",
    )(attn, sgu, x, attn_gain, w_out)


def kernel(x, ffn1_norm, ffn1_w_gate, ffn1_w_up, ffn1_w_down, mix_norm, w_in, q_norm, k_norm, sgu_ln_gain, sgu_ln_bias, sgu_w_spatial, sgu_b_spatial, attn_out_gain, sgu_out_gain, w_out, ffn2_norm, ffn2_w_gate, ffn2_w_up, ffn2_w_down):
    batch, seq, d_model = x.shape
    depth = ffn1_norm.shape[0]
    assert d_model == D_MODEL and seq % PROJ_TM == 0 and (batch * seq) % FFN_TM == 0
    xt = x.reshape(batch * seq, d_model)
    vec = lambda a: a.reshape(1, -1)
    for l in range(depth):
        xt = _ffn(xt, vec(ffn1_norm[l]), ffn1_w_gate[l], ffn1_w_up[l], ffn1_w_down[l])
        w_in_bf = w_in[l].astype(BF16)
        qa, ka, v = _qkv(xt, vec(mix_norm[l]), w_in_bf[:, :3 * D_ATTN], vec(q_norm[l]), vec(k_norm[l]), seq)
        sgu = _sgu(xt, vec(mix_norm[l]), w_in_bf[:, 3 * D_ATTN:], vec(sgu_ln_gain[l]), vec(sgu_ln_bias[l]),
                   sgu_w_spatial[l], sgu_b_spatial[l].T, vec(sgu_out_gain[l]))
        attn = _attention(qa, ka, v, batch, seq)
        xt = _outproj(attn, sgu, xt, vec(attn_out_gain[l]), w_out[l].astype(BF16))
        xt = _ffn(xt, vec(ffn2_norm[l]), ffn2_w_gate[l], ffn2_w_up[l], ffn2_w_down[l])
    return xt.reshape(batch, seq, d_model)
```

```python
import functools

import jax
import jax.numpy as jnp
from jax import lax
from jax.experimental import pallas as pl
from jax.experimental.pallas import tpu as pltpu

D_MODEL = 2048
D_FF = 5632
ATTN_HEADS = 8
HEAD_DIM = 128
D_ATTN = ATTN_HEADS * HEAD_DIM
SGU_GROUPS = 8
SGU_GROUP_DIM = 128
D_SGU = SGU_GROUPS * SGU_GROUP_DIM
MOBA_BLOCK = 256
MOBA_TOPK = 3
SGU_CHUNK = 128
EPS = 1e-6
NEG_INF = -1e30

LANES = 128
AUG_DIM = 2 * HEAD_DIM
VMEM_LIMIT_BYTES = 56 * 1024 * 1024

FFN_TM = 1024
FFN_TF = 512
PROJ_TM = 512
ATTN_TQ = 512
ATTN_KV_CHUNK = 512

F32 = jnp.float32
BF16 = jnp.bfloat16


def _rmsnorm(x, gain):
    return x * lax.rsqrt(jnp.mean(x * x, axis=-1, keepdims=True) + EPS) * gain


def _gelu_tanh(x):
    cdf = 0.5 * (1.0 + jnp.tanh(0.7978845608028654 * (x + 0.044715 * (x * x * x))))
    return x * cdf


def _dot(a, b):
    return jnp.dot(a, b, preferred_element_type=F32)


def _dot_nt(a, b, precision=None):
    return lax.dot_general(a, b, (((1,), (1,)), ((), ())), precision=precision,
                           preferred_element_type=F32)


def _params(*semantics):
    return pltpu.CompilerParams(dimension_semantics=semantics, vmem_limit_bytes=VMEM_LIMIT_BYTES)


def _resident(shape):
    return pl.BlockSpec(shape, lambda *_: (0,) * len(shape), pipeline_mode=pl.Buffered(1))


def _ffn_body(x_ref, g_ref, wg_ref, wu_ref, wd_ref, o_ref, h_scr):
    j = pl.program_id(1)

    @pl.when(j == 0)
    def _init():
        x = x_ref[...]
        h_scr[...] = _rmsnorm(x, g_ref[...]).astype(BF16)
        o_ref[...] = x

    h = h_scr[...]
    gate = _dot(h, wg_ref[...])
    up = _dot(h, wu_ref[...])
    act = (gate / (1.0 + jnp.exp(-gate))) * up * 0.5
    o_ref[...] += _dot(act.astype(BF16), wd_ref[...])


def _ffn(x, gain, w_gate, w_up, w_down):
    tokens = x.shape[0]
    row = pl.BlockSpec((FFN_TM, D_MODEL), lambda i, j: (i, 0))
    return pl.pallas_call(
        _ffn_body,
        grid=(tokens // FFN_TM, D_FF // FFN_TF),
        in_specs=[row, pl.BlockSpec((1, D_MODEL), lambda i, j: (0, 0)),
                  pl.BlockSpec((D_MODEL, FFN_TF), lambda i, j: (0, j)),
                  pl.BlockSpec((D_MODEL, FFN_TF), lambda i, j: (0, j)),
                  pl.BlockSpec((FFN_TF, D_MODEL), lambda i, j: (j, 0))],
        out_specs=row,
        out_shape=jax.ShapeDtypeStruct((tokens, D_MODEL), F32),
        scratch_shapes=[pltpu.VMEM((FFN_TM, D_MODEL), BF16)],
        compiler_params=_params("parallel", "arbitrary"),
        name="ffn",
    )(x, gain, w_gate, w_up, w_down)


def _qkv_body(x_ref, gx_ref, w_ref, gq_ref, gk_ref, qa_ref, ka_ref, v_ref, km_scr, *, tiles_per_batch):
    tm = PROJ_TM
    blocks_per_batch = tiles_per_batch * (tm // MOBA_BLOCK)
    i = pl.program_id(0)
    n0 = (i % tiles_per_batch) * (tm // MOBA_BLOCK)

    @pl.when(i == 0)
    def _init():
        km_scr[...] = jnp.zeros_like(km_scr)

    h = _rmsnorm(x_ref[...], gx_ref[...]).astype(BF16)
    q = _dot(h, w_ref[:, 0:D_ATTN])
    k = _dot(h, w_ref[:, D_ATTN:2 * D_ATTN])
    v_ref[...] = _dot(h, w_ref[:, 2 * D_ATTN:3 * D_ATTN]).astype(BF16)

    scale = HEAD_DIM ** -0.5
    blk_rows = lax.broadcasted_iota(jnp.int32, (blocks_per_batch, HEAD_DIM), 0)
    n_iota = lax.broadcasted_iota(jnp.int32, (blocks_per_batch, tm), 0)
    lane_tok = lax.broadcasted_iota(jnp.int32, (blocks_per_batch, tm), 1)
    q_blk = n0 + lane_tok // MOBA_BLOCK
    valid = n_iota < q_blk
    tok_blk = n0 + lax.broadcasted_iota(jnp.int32, (tm, LANES), 0) // MOBA_BLOCK
    onehot = jnp.where(lax.broadcasted_iota(jnp.int32, (tm, LANES), 1) == tok_blk, 1.0, 0.0).astype(BF16)
    pad_rows = jnp.zeros((LANES - blocks_per_batch, tm), F32)

    for hh in range(ATTN_HEADS):
        sl = slice(hh * HEAD_DIM, (hh + 1) * HEAD_DIM)
        qn = _rmsnorm(q[:, sl], gq_ref[...])
        kn = _rmsnorm(k[:, sl], gk_ref[...])

        kmean = km_scr[:, sl]
        for b in range(tm // MOBA_BLOCK):
            km = jnp.mean(kn[b * MOBA_BLOCK:(b + 1) * MOBA_BLOCK], axis=0, keepdims=True)
            kmean = jnp.where(blk_rows == n0 + b, km, kmean)
        km_scr[:, sl] = kmean

        gate = _dot_nt(kmean, qn, precision=lax.Precision.HIGHEST)
        cnt = jnp.zeros((blocks_per_batch, tm), jnp.int32)
        for m in range(blocks_per_batch - 1):
            gm = gate[m:m + 1, :]
            beats = (gm > gate) | ((gm == gate) & (m < n_iota))
            cnt = cnt + jnp.where(beats & (m < q_blk), 1, 0)
        sel = ((cnt < MOBA_TOPK) & valid) | (n_iota == q_blk)
        bias_t = jnp.concatenate([jnp.where(sel, 0.0, NEG_INF), pad_rows], axis=0)
        bias = bias_t.T

        a0 = hh * AUG_DIM
        qa_ref[:, a0:a0 + HEAD_DIM] = (qn * scale).astype(BF16)
        qa_ref[:, a0 + HEAD_DIM:a0 + AUG_DIM] = bias.astype(BF16)
        ka_ref[:, a0:a0 + HEAD_DIM] = kn.astype(BF16)
        ka_ref[:, a0 + HEAD_DIM:a0 + AUG_DIM] = onehot


def _qkv(x, x_gain, w_qkv, q_gain, k_gain, seq):
    tokens = x.shape[0]
    tm = PROJ_TM
    tiles_per_batch = seq // tm
    blocks_per_batch = seq // MOBA_BLOCK
    assert blocks_per_batch <= LANES and blocks_per_batch % 8 == 0
    row = lambda width: pl.BlockSpec((tm, width), lambda i: (i, 0))
    return pl.pallas_call(
        functools.partial(_qkv_body, tiles_per_batch=tiles_per_batch),
        grid=(tokens // tm,),
        in_specs=[row(D_MODEL), _resident((1, D_MODEL)), _resident((D_MODEL, 3 * D_ATTN)),
                  _resident((1, HEAD_DIM)), _resident((1, HEAD_DIM))],
        out_specs=(row(ATTN_HEADS * AUG_DIM), row(ATTN_HEADS * AUG_DIM), row(D_ATTN)),
        out_shape=(jax.ShapeDtypeStruct((tokens, ATTN_HEADS * AUG_DIM), BF16),
                   jax.ShapeDtypeStruct((tokens, ATTN_HEADS * AUG_DIM), BF16),
                   jax.ShapeDtypeStruct((tokens, D_ATTN), BF16)),
        scratch_shapes=[pltpu.VMEM((blocks_per_batch, D_ATTN), F32)],
        compiler_params=_params("arbitrary"),
        name="qkv_gate",
    )(x, x_gain, w_qkv, q_gain, k_gain)


def _sgu_body(x_ref, gx_ref, w_ref, lng_ref, lnb_ref, ws_ref, bs_ref, og_ref, o_ref, sg_scr):
    tm = PROJ_TM
    h = _rmsnorm(x_ref[...], gx_ref[...]).astype(BF16)
    u = _gelu_tanh(_dot(h, w_ref[:, 0:D_SGU]))
    g = _gelu_tanh(_dot(h, w_ref[:, D_SGU:2 * D_SGU]))
    r = lax.broadcasted_iota(jnp.int32, (SGU_CHUNK, SGU_CHUNK), 0)
    c = lax.broadcasted_iota(jnp.int32, (SGU_CHUNK, SGU_CHUNK), 1)
    for gi in range(SGU_GROUPS):
        sl = slice(gi * SGU_GROUP_DIM, (gi + 1) * SGU_GROUP_DIM)
        gg = g[:, sl]
        d = gg - jnp.mean(gg, axis=-1, keepdims=True)
        var = jnp.mean(d * d, axis=-1, keepdims=True)
        gn = (d * lax.rsqrt(var + EPS) * lng_ref[:, sl] + lnb_ref[:, sl]).astype(BF16)
        w_causal = jnp.where(r >= c, ws_ref[gi], 0.0).astype(BF16)
        b_col = bs_ref[:, gi:gi + 1]
        for ci in range(tm // SGU_CHUNK):
            rows = slice(ci * SGU_CHUNK, (ci + 1) * SGU_CHUNK)
            mixed = _dot(w_causal, gn[rows]) + b_col
            sg_scr[rows, sl] = u[rows, sl] * mixed
    o_ref[...] = _rmsnorm(sg_scr[...], og_ref[...]).astype(BF16)


def _sgu(x, x_gain, w_sgu, ln_gain, ln_bias, w_spatial, b_spatial_t, out_gain):
    tokens = x.shape[0]
    tm = PROJ_TM
    return pl.pallas_call(
        _sgu_body,
        grid=(tokens // tm,),
        in_specs=[pl.BlockSpec((tm, D_MODEL), lambda i: (i, 0)), _resident((1, D_MODEL)),
                  _resident((D_MODEL, 2 * D_SGU)),
                  _resident((1, D_SGU)), _resident((1, D_SGU)),
                  _resident((SGU_GROUPS, SGU_CHUNK, SGU_CHUNK)),
                  _resident((SGU_CHUNK, SGU_GROUPS)),
                  _resident((1, D_SGU))],
        out_specs=pl.BlockSpec((tm, D_SGU), lambda i: (i, 0)),
        out_shape=jax.ShapeDtypeStruct((tokens, D_SGU), BF16),
        scratch_shapes=[pltpu.VMEM((tm, D_SGU), F32)],
        compiler_params=_params("parallel"),
        name="sgu",
    )(x, x_gain, w_sgu, ln_gain, ln_bias, w_spatial, b_spatial_t, out_gain)


def _attn_scores(t, q_ref, k_ref, s_ref, m_ref):
    tq, kc = ATTN_TQ, ATTN_KV_CHUNK
    m_part = None
    for c in range(t + 1):
        s = _dot_nt(q_ref[...], k_ref[c * kc:(c + 1) * kc, :])
        if c == t:
            r = lax.broadcasted_iota(jnp.int32, (tq, kc), 0)
            col = lax.broadcasted_iota(jnp.int32, (tq, kc), 1)
            s = jnp.where(col <= r, s, NEG_INF)
        s_ref[:, c * kc:(c + 1) * kc] = s
        for j in range(kc // LANES):
            sj = s[:, j * LANES:(j + 1) * LANES]
            m_part = sj if m_part is None else jnp.maximum(m_part, sj)
    m_ref[...] = jnp.broadcast_to(jnp.max(m_part, axis=-1, keepdims=True), (tq, LANES))


def _attn_values(t, v_ref, o_ref, s_ref, m_ref):
    tq, kc = ATTN_TQ, ATTN_KV_CHUNK
    m = m_ref[...]
    l_part = jnp.zeros((tq, LANES), F32)
    acc = jnp.zeros((tq, HEAD_DIM), F32)
    for c in range(t + 1):
        p_tiles = []
        for j in range(kc // LANES):
            pj = jnp.exp(s_ref[:, c * kc + j * LANES:c * kc + (j + 1) * LANES] - m)
            l_part = l_part + pj
            p_tiles.append(pj.astype(BF16))
        acc = acc + _dot(jnp.concatenate(p_tiles, axis=1), v_ref[c * kc:(c + 1) * kc, :])
    o_ref[...] = acc / jnp.sum(l_part, axis=-1, keepdims=True)


def _attn_body(q_ref, k_ref, v_ref, o_ref, s_scr, m_scr):
    t = pl.program_id(2)
    n_tiles = k_ref.shape[0] // ATTN_TQ

    def step(ts):
        if ts == 0:
            o_ref[...] = jnp.zeros_like(o_ref)
        else:
            _attn_values(ts - 1, v_ref, o_ref, s_scr.at[(ts - 1) % 2], m_scr.at[(ts - 1) % 2])
        if ts < n_tiles:
            _attn_scores(ts, q_ref, k_ref, s_scr.at[ts % 2], m_scr.at[ts % 2])

    for ts in range(n_tiles + 1):
        pl.when(t == ts)(functools.partial(step, ts))


def _attention(qa, ka, v, batch, seq):
    tokens = batch * seq
    q_tiles = seq // ATTN_TQ
    return pl.pallas_call(
        _attn_body,
        grid=(batch, ATTN_HEADS, q_tiles + 1),
        in_specs=[pl.BlockSpec((ATTN_TQ, AUG_DIM),
                               lambda b, h, i: (b * q_tiles + jnp.minimum(i, q_tiles - 1), h)),
                  pl.BlockSpec((seq, AUG_DIM), lambda b, h, i: (b, h)),
                  pl.BlockSpec((seq, HEAD_DIM), lambda b, h, i: (b, h))],
        out_specs=pl.BlockSpec((ATTN_TQ, HEAD_DIM),
                               lambda b, h, i: (b * q_tiles + jnp.maximum(i - 1, 0), h)),
        out_shape=jax.ShapeDtypeStruct((tokens, D_ATTN), F32),
        scratch_shapes=[pltpu.VMEM((2, ATTN_TQ, seq), F32), pltpu.VMEM((2, ATTN_TQ, LANES), F32)],
        compiler_params=_params("parallel", "parallel", "arbitrary"),
        name="moba_attn",
    )(qa, ka, v)


def _outproj_body(a_ref, s_ref, x_ref, ga_ref, w_ref, o_ref):
    an = _rmsnorm(a_ref[...], ga_ref[...]).astype(BF16)
    o_ref[...] = (x_ref[...] + _dot(an, w_ref[0:D_ATTN, :])
                  + _dot(s_ref[...], w_ref[D_ATTN:D_ATTN + D_SGU, :]))


def _outproj(attn, sgu, x, attn_gain, w_out):
    tokens = x.shape[0]
    tm = PROJ_TM
    row = lambda width: pl.BlockSpec((tm, width), lambda i: (i, 0))
    return pl.pallas_call(
        _outproj_body,
        grid=(tokens // tm,),
        in_specs=[row(D_ATTN), row(D_SGU), row(D_MODEL), _resident((1, D_ATTN)),
                  _resident((D_ATTN + D_SGU, D_MODEL))],
        out_specs=row(D_MODEL),
        out_shape=jax.ShapeDtypeStruct((tokens, D_MODEL), F32),
        compiler_params=_params("parallel"),
        name="outproj",
    )(attn, sgu, x, attn_gain, w_out)


def kernel(x, ffn1_norm, ffn1_w_gate, ffn1_w_up, ffn1_w_down, mix_norm, w_in, q_norm, k_norm, sgu_ln_gain, sgu_ln_bias, sgu_w_spatial, sgu_b_spatial, attn_out_gain, sgu_out_gain, w_out, ffn2_norm, ffn2_w_gate, ffn2_w_up, ffn2_w_down):
    batch, seq, d_model = x.shape
    depth = ffn1_norm.shape[0]
    assert d_model == D_MODEL and seq % PROJ_TM == 0 and (batch * seq) % FFN_TM == 0
    xt = x.reshape(batch * seq, d_model)
    vec = lambda a: a.reshape(1, -1)
    for l in range(depth):
        xt = _ffn(xt, vec(ffn1_norm[l]), ffn1_w_gate[l].astype(BF16), ffn1_w_up[l].astype(BF16),
                  ffn1_w_down[l].astype(BF16))
        w_qkv = w_in[l][:, :3 * D_ATTN].astype(BF16)
        w_sgu = w_in[l][:, 3 * D_ATTN:].astype(BF16)
        qa, ka, v = _qkv(xt, vec(mix_norm[l]), w_qkv, vec(q_norm[l]), vec(k_norm[l]), seq)
        sgu = _sgu(xt, vec(mix_norm[l]), w_sgu, vec(sgu_ln_gain[l]), vec(sgu_ln_bias[l]),
                   sgu_w_spatial[l], sgu_b_spatial[l].T, vec(sgu_out_gain[l]))
        attn = _attention(qa, ka, v, batch, seq)
        xt = _outproj(attn, sgu, xt, vec(attn_out_gain[l]), w_out[l].astype(BF16))
        xt = _ffn(xt, vec(ffn2_norm[l]), ffn2_w_gate[l].astype(BF16), ffn2_w_up[l].astype(BF16),
                  ffn2_w_down[l].astype(BF16))
    return xt.reshape(batch, seq, d_model)
```

```python
import functools

import jax
import jax.numpy as jnp
from jax import lax
from jax.experimental import pallas as pl
from jax.experimental.pallas import tpu as pltpu

D_MODEL = 2048
D_FF = 5632
ATTN_HEADS = 8
HEAD_DIM = 128
D_ATTN = ATTN_HEADS * HEAD_DIM
SGU_GROUPS = 8
SGU_GROUP_DIM = 128
D_SGU = SGU_GROUPS * SGU_GROUP_DIM
MOBA_BLOCK = 256
MOBA_TOPK = 3
SGU_CHUNK = 128
EPS = 1e-6
NEG_INF = -1e30

LANES = 128
AUG_DIM = 2 * HEAD_DIM
VMEM_LIMIT_BYTES = 56 * 1024 * 1024

FFN_TM = 1024
FFN_HEAD_TF = 256
FFN_TF = 512
PROJ_TM = 512
ATTN_TQ = 512
ATTN_KV_CHUNK = 512

F32 = jnp.float32
BF16 = jnp.bfloat16


def _rmsnorm(x, gain):
    return x * lax.rsqrt(jnp.mean(x * x, axis=-1, keepdims=True) + EPS) * gain


def _gelu_tanh(x):
    cdf = 0.5 * (1.0 + jnp.tanh(0.7978845608028654 * (x + 0.044715 * (x * x * x))))
    return x * cdf


def _dot(a, b):
    return jnp.dot(a, b, preferred_element_type=F32)


def _dot_nt(a, b, precision=None):
    return lax.dot_general(a, b, (((1,), (1,)), ((), ())), precision=precision,
                           preferred_element_type=F32)


def _params(*semantics):
    return pltpu.CompilerParams(dimension_semantics=semantics, vmem_limit_bytes=VMEM_LIMIT_BYTES)


def _resident(shape):
    return pl.BlockSpec(shape, lambda *_: (0,) * len(shape), pipeline_mode=pl.Buffered(1))


def _ffn_start(x_ref, g_ref, o_ref, h_scr):
    x = x_ref[...]
    h_scr[...] = _rmsnorm(x, g_ref[...]).astype(BF16)
    o_ref[...] = x


def _ffn_accumulate(h_scr, wg, wu, wd, o_ref):
    h = h_scr[...]
    gate = _dot(h, wg)
    up = _dot(h, wu)
    act = (gate / (1.0 + jnp.exp(-gate))) * up * 0.5
    o_ref[...] += _dot(act.astype(BF16), wd)


def _ffn_head_body(x_ref, g_ref, wg_ref, wu_ref, wd_ref, o_ref, wg_bf_ref, wu_bf_ref, wd_bf_ref, h_scr):
    @pl.when(pl.program_id(0) == 0)
    def _():
        _ffn_start(x_ref, g_ref, o_ref, h_scr)

    wg_bf_ref[...] = wg_ref[...].astype(BF16)
    wu_bf_ref[...] = wu_ref[...].astype(BF16)
    wd_bf_ref[...] = wd_ref[...].astype(BF16)
    _ffn_accumulate(h_scr, wg_bf_ref[...], wu_bf_ref[...], wd_bf_ref[...], o_ref)


def _ffn_tail_body(x_ref, g_ref, wg_ref, wu_ref, wd_ref, head_hbm, o_ref, h_scr):
    i = pl.program_id(0)
    j = pl.program_id(1)

    @pl.when((i == 0) & (j == 0))
    def _():
        pltpu.sync_copy(head_hbm, o_ref)

    @pl.when((i > 0) & (j == 0))
    def _():
        _ffn_start(x_ref, g_ref, o_ref, h_scr)

    @pl.when(i > 0)
    def _():
        _ffn_accumulate(h_scr, wg_ref[...], wu_ref[...], wd_ref[...], o_ref)


def _ffn(x, gain, w_gate, w_up, w_down):
    tokens = x.shape[0]
    n_tiles = tokens // FFN_TM
    vec = pl.BlockSpec((1, D_MODEL), lambda *_: (0, 0))

    tf = FFN_HEAD_TF
    col = pl.BlockSpec((D_MODEL, tf), lambda j: (0, j))
    rowblk = pl.BlockSpec((tf, D_MODEL), lambda j: (j, 0))
    tile0 = pl.BlockSpec((FFN_TM, D_MODEL), lambda j: (0, 0))
    head, wg_bf, wu_bf, wd_bf = pl.pallas_call(
        _ffn_head_body,
        grid=(D_FF // tf,),
        in_specs=[tile0, vec, col, col, rowblk],
        out_specs=(tile0, col, col, rowblk),
        out_shape=(jax.ShapeDtypeStruct((FFN_TM, D_MODEL), F32),
                   jax.ShapeDtypeStruct((D_MODEL, D_FF), BF16),
                   jax.ShapeDtypeStruct((D_MODEL, D_FF), BF16),
                   jax.ShapeDtypeStruct((D_FF, D_MODEL), BF16)),
        scratch_shapes=[pltpu.VMEM((FFN_TM, D_MODEL), BF16)],
        compiler_params=_params("arbitrary"),
        name="ffn_head",
    )(x, gain, w_gate, w_up, w_down)

    tf = FFN_TF
    wcol = lambda i, j: (0, jnp.where(i == 0, 0, j))
    wrow = lambda i, j: (jnp.where(i == 0, 0, j), 0)
    row = pl.BlockSpec((FFN_TM, D_MODEL), lambda i, j: (i, 0))
    return pl.pallas_call(
        _ffn_tail_body,
        grid=(n_tiles, D_FF // tf),
        in_specs=[row, vec, pl.BlockSpec((D_MODEL, tf), wcol), pl.BlockSpec((D_MODEL, tf), wcol),
                  pl.BlockSpec((tf, D_MODEL), wrow), pl.BlockSpec(memory_space=pl.ANY)],
        out_specs=row,
        out_shape=jax.ShapeDtypeStruct((tokens, D_MODEL), F32),
        scratch_shapes=[pltpu.VMEM((FFN_TM, D_MODEL), BF16)],
        compiler_params=_params("parallel", "arbitrary"),
        name="ffn_tail",
    )(x, gain, wg_bf, wu_bf, wd_bf, head)


def _qkv_body(x_ref, gx_ref, w_ref, gq_ref, gk_ref, qa_ref, ka_ref, v_ref, km_scr, *, tiles_per_batch):
    tm = PROJ_TM
    blocks_per_batch = tiles_per_batch * (tm // MOBA_BLOCK)
    i = pl.program_id(0)
    n0 = (i % tiles_per_batch) * (tm // MOBA_BLOCK)

    @pl.when(i == 0)
    def _init():
        km_scr[...] = jnp.zeros_like(km_scr)

    h = _rmsnorm(x_ref[...], gx_ref[...]).astype(BF16)
    q = _dot(h, w_ref[:, 0:D_ATTN])
    k = _dot(h, w_ref[:, D_ATTN:2 * D_ATTN])
    v_ref[...] = _dot(h, w_ref[:, 2 * D_ATTN:3 * D_ATTN]).astype(BF16)

    scale = HEAD_DIM ** -0.5
    blk_rows = lax.broadcasted_iota(jnp.int32, (blocks_per_batch, HEAD_DIM), 0)
    n_iota = lax.broadcasted_iota(jnp.int32, (blocks_per_batch, tm), 0)
    lane_tok = lax.broadcasted_iota(jnp.int32, (blocks_per_batch, tm), 1)
    q_blk = n0 + lane_tok // MOBA_BLOCK
    valid = n_iota < q_blk
    tok_blk = n0 + lax.broadcasted_iota(jnp.int32, (tm, LANES), 0) // MOBA_BLOCK
    onehot = jnp.where(lax.broadcasted_iota(jnp.int32, (tm, LANES), 1) == tok_blk, 1.0, 0.0).astype(BF16)
    pad_rows = jnp.zeros((LANES - blocks_per_batch, tm), F32)

    for hh in range(ATTN_HEADS):
        sl = slice(hh * HEAD_DIM, (hh + 1) * HEAD_DIM)
        qn = _rmsnorm(q[:, sl], gq_ref[...])
        kn = _rmsnorm(k[:, sl], gk_ref[...])

        kmean = km_scr[:, sl]
        for b in range(tm // MOBA_BLOCK):
            km = jnp.mean(kn[b * MOBA_BLOCK:(b + 1) * MOBA_BLOCK], axis=0, keepdims=True)
            kmean = jnp.where(blk_rows == n0 + b, km, kmean)
        km_scr[:, sl] = kmean

        gate = _dot_nt(kmean, qn, precision=lax.Precision.HIGHEST)
        cnt = jnp.zeros((blocks_per_batch, tm), jnp.int32)
        for m in range(blocks_per_batch - 1):
            gm = gate[m:m + 1, :]
            beats = (gm > gate) | ((gm == gate) & (m < n_iota))
            cnt = cnt + jnp.where(beats & (m < q_blk), 1, 0)
        sel = ((cnt < MOBA_TOPK) & valid) | (n_iota == q_blk)
        bias_t = jnp.concatenate([jnp.where(sel, 0.0, NEG_INF), pad_rows], axis=0)
        bias = bias_t.T

        a0 = hh * AUG_DIM
        qa_ref[:, a0:a0 + HEAD_DIM] = (qn * scale).astype(BF16)
        qa_ref[:, a0 + HEAD_DIM:a0 + AUG_DIM] = bias.astype(BF16)
        ka_ref[:, a0:a0 + HEAD_DIM] = kn.astype(BF16)
        ka_ref[:, a0 + HEAD_DIM:a0 + AUG_DIM] = onehot


def _qkv(x, x_gain, w_qkv, q_gain, k_gain, seq):
    tokens = x.shape[0]
    tm = PROJ_TM
    tiles_per_batch = seq // tm
    blocks_per_batch = seq // MOBA_BLOCK
    assert blocks_per_batch <= LANES and blocks_per_batch % 8 == 0
    row = lambda width: pl.BlockSpec((tm, width), lambda i: (i, 0))
    return pl.pallas_call(
        functools.partial(_qkv_body, tiles_per_batch=tiles_per_batch),
        grid=(tokens // tm,),
        in_specs=[row(D_MODEL), _resident((1, D_MODEL)), _resident((D_MODEL, 3 * D_ATTN)),
                  _resident((1, HEAD_DIM)), _resident((1, HEAD_DIM))],
        out_specs=(row(ATTN_HEADS * AUG_DIM), row(ATTN_HEADS * AUG_DIM), row(D_ATTN)),
        out_shape=(jax.ShapeDtypeStruct((tokens, ATTN_HEADS * AUG_DIM), BF16),
                   jax.ShapeDtypeStruct((tokens, ATTN_HEADS * AUG_DIM), BF16),
                   jax.ShapeDtypeStruct((tokens, D_ATTN), BF16)),
        scratch_shapes=[pltpu.VMEM((blocks_per_batch, D_ATTN), F32)],
        compiler_params=_params("arbitrary"),
        name="qkv_gate",
    )(x, x_gain, w_qkv, q_gain, k_gain)


def _sgu_body(x_ref, gx_ref, w_ref, lng_ref, lnb_ref, ws_ref, bs_ref, og_ref, o_ref, sg_scr):
    tm = PROJ_TM
    h = _rmsnorm(x_ref[...], gx_ref[...]).astype(BF16)
    u = _gelu_tanh(_dot(h, w_ref[:, 0:D_SGU]))
    g = _gelu_tanh(_dot(h, w_ref[:, D_SGU:2 * D_SGU]))
    r = lax.broadcasted_iota(jnp.int32, (SGU_CHUNK, SGU_CHUNK), 0)
    c = lax.broadcasted_iota(jnp.int32, (SGU_CHUNK, SGU_CHUNK), 1)
    for gi in range(SGU_GROUPS):
        sl = slice(gi * SGU_GROUP_DIM, (gi + 1) * SGU_GROUP_DIM)
        gg = g[:, sl]
        d = gg - jnp.mean(gg, axis=-1, keepdims=True)
        var = jnp.mean(d * d, axis=-1, keepdims=True)
        gn = (d * lax.rsqrt(var + EPS) * lng_ref[:, sl] + lnb_ref[:, sl]).astype(BF16)
        w_causal = jnp.where(r >= c, ws_ref[gi], 0.0).astype(BF16)
        b_col = bs_ref[:, gi:gi + 1]
        for ci in range(tm // SGU_CHUNK):
            rows = slice(ci * SGU_CHUNK, (ci + 1) * SGU_CHUNK)
            mixed = _dot(w_causal, gn[rows]) + b_col
            sg_scr[rows, sl] = u[rows, sl] * mixed
    o_ref[...] = _rmsnorm(sg_scr[...], og_ref[...]).astype(BF16)


def _sgu(x, x_gain, w_sgu, ln_gain, ln_bias, w_spatial, b_spatial_t, out_gain):
    tokens = x.shape[0]
    tm = PROJ_TM
    return pl.pallas_call(
        _sgu_body,
        grid=(tokens // tm,),
        in_specs=[pl.BlockSpec((tm, D_MODEL), lambda i: (i, 0)), _resident((1, D_MODEL)),
                  _resident((D_MODEL, 2 * D_SGU)),
                  _resident((1, D_SGU)), _resident((1, D_SGU)),
                  _resident((SGU_GROUPS, SGU_CHUNK, SGU_CHUNK)),
                  _resident((SGU_CHUNK, SGU_GROUPS)),
                  _resident((1, D_SGU))],
        out_specs=pl.BlockSpec((tm, D_SGU), lambda i: (i, 0)),
        out_shape=jax.ShapeDtypeStruct((tokens, D_SGU), BF16),
        scratch_shapes=[pltpu.VMEM((tm, D_SGU), F32)],
        compiler_params=_params("parallel"),
        name="sgu",
    )(x, x_gain, w_sgu, ln_gain, ln_bias, w_spatial, b_spatial_t, out_gain)


def _attn_scores(t, q_ref, k_ref, s_ref, m_ref):
    tq, kc = ATTN_TQ, ATTN_KV_CHUNK
    m_part = None
    for c in range(t + 1):
        s = _dot_nt(q_ref[...], k_ref[c * kc:(c + 1) * kc, :])
        if c == t:
            r = lax.broadcasted_iota(jnp.int32, (tq, kc), 0)
            col = lax.broadcasted_iota(jnp.int32, (tq, kc), 1)
            s = jnp.where(col <= r, s, NEG_INF)
        s_ref[:, c * kc:(c + 1) * kc] = s
        for j in range(kc // LANES):
            sj = s[:, j * LANES:(j + 1) * LANES]
            m_part = sj if m_part is None else jnp.maximum(m_part, sj)
    m_ref[...] = jnp.broadcast_to(jnp.max(m_part, axis=-1, keepdims=True), (tq, LANES))


def _attn_values(t, v_ref, o_ref, s_ref, m_ref):
    tq, kc = ATTN_TQ, ATTN_KV_CHUNK
    m = m_ref[...]
    l_part = jnp.zeros((tq, LANES), F32)
    acc = jnp.zeros((tq, HEAD_DIM), F32)
    for c in range(t + 1):
        p_tiles = []
        for j in range(kc // LANES):
            pj = jnp.exp(s_ref[:, c * kc + j * LANES:c * kc + (j + 1) * LANES] - m)
            l_part = l_part + pj
            p_tiles.append(pj.astype(BF16))
        acc = acc + _dot(jnp.concatenate(p_tiles, axis=1), v_ref[c * kc:(c + 1) * kc, :])
    o_ref[...] = acc / jnp.sum(l_part, axis=-1, keepdims=True)


def _attn_body(q_ref, k_ref, v_ref, o_ref, s_scr, m_scr):
    t = pl.program_id(2)
    n_tiles = k_ref.shape[0] // ATTN_TQ

    def step(ts):
        if ts == 0:
            o_ref[...] = jnp.zeros_like(o_ref)
        else:
            _attn_values(ts - 1, v_ref, o_ref, s_scr.at[(ts - 1) % 2], m_scr.at[(ts - 1) % 2])
        if ts < n_tiles:
            _attn_scores(ts, q_ref, k_ref, s_scr.at[ts % 2], m_scr.at[ts % 2])

    for ts in range(n_tiles + 1):
        pl.when(t == ts)(functools.partial(step, ts))


def _attention(qa, ka, v, batch, seq):
    tokens = batch * seq
    q_tiles = seq // ATTN_TQ
    return pl.pallas_call(
        _attn_body,
        grid=(batch, ATTN_HEADS, q_tiles + 1),
        in_specs=[pl.BlockSpec((ATTN_TQ, AUG_DIM),
                               lambda b, h, i: (b * q_tiles + jnp.minimum(i, q_tiles - 1), h)),
                  pl.BlockSpec((seq, AUG_DIM), lambda b, h, i: (b, h)),
                  pl.BlockSpec((seq, HEAD_DIM), lambda b, h, i: (b, h))],
        out_specs=pl.BlockSpec((ATTN_TQ, HEAD_DIM),
                               lambda b, h, i: (b * q_tiles + jnp.maximum(i - 1, 0), h)),
        out_shape=jax.ShapeDtypeStruct((tokens, D_ATTN), F32),
        scratch_shapes=[pltpu.VMEM((2, ATTN_TQ, seq), F32), pltpu.VMEM((2, ATTN_TQ, LANES), F32)],
        compiler_params=_params("parallel", "parallel", "arbitrary"),
        name="moba_attn",
    )(qa, ka, v)


def _outproj_body(a_ref, s_ref, x_ref, ga_ref, w_ref, o_ref):
    an = _rmsnorm(a_ref[...], ga_ref[...]).astype(BF16)
    o_ref[...] = (x_ref[...] + _dot(an, w_ref[0:D_ATTN, :])
                  + _dot(s_ref[...], w_ref[D_ATTN:D_ATTN + D_SGU, :]))


def _outproj(attn, sgu, x, attn_gain, w_out):
    tokens = x.shape[0]
    tm = PROJ_TM
    row = lambda width: pl.BlockSpec((tm, width), lambda i: (i, 0))
    return pl.pallas_call(
        _outproj_body,
        grid=(tokens // tm,),
        in_specs=[row(D_ATTN), row(D_SGU), row(D_MODEL), _resident((1, D_ATTN)),
                  _resident((D_ATTN + D_SGU, D_MODEL))],
        out_specs=row(D_MODEL),
        out_shape=jax.ShapeDtypeStruct((tokens, D_MODEL), F32),
        compiler_params=_params("parallel"),
        name="outproj",
    )(attn, sgu, x, attn_gain, w_out)


def kernel(x, ffn1_norm, ffn1_w_gate, ffn1_w_up, ffn1_w_down, mix_norm, w_in, q_norm, k_norm, sgu_ln_gain, sgu_ln_bias, sgu_w_spatial, sgu_b_spatial, attn_out_gain, sgu_out_gain, w_out, ffn2_norm, ffn2_w_gate, ffn2_w_up, ffn2_w_down):
    batch, seq, d_model = x.shape
    depth = ffn1_norm.shape[0]
    assert d_model == D_MODEL and seq % PROJ_TM == 0 and (batch * seq) % FFN_TM == 0
    xt = x.reshape(batch * seq, d_model)
    vec = lambda a: a.reshape(1, -1)
    for l in range(depth):
        xt = _ffn(xt, vec(ffn1_norm[l]), ffn1_w_gate[l], ffn1_w_up[l], ffn1_w_down[l])
        w_qkv = w_in[l][:, :3 * D_ATTN].astype(BF16)
        w_sgu = w_in[l][:, 3 * D_ATTN:].astype(BF16)
        qa, ka, v = _qkv(xt, vec(mix_norm[l]), w_qkv, vec(q_norm[l]), vec(k_norm[l]), seq)
        sgu = _sgu(xt, vec(mix_norm[l]), w_sgu, vec(sgu_ln_gain[l]), vec(sgu_ln_bias[l]),
                   sgu_w_spatial[l], sgu_b_spatial[l].T, vec(sgu_out_gain[l]))
        attn = _attention(qa, ka, v, batch, seq)
        xt = _outproj(attn, sgu, xt, vec(attn_out_gain[l]), w_out[l].astype(BF16))
        xt = _ffn(xt, vec(ffn2_norm[l]), ffn2_w_gate[l], ffn2_w_up[l], ffn2_w_down[l])
    return xt.reshape(batch, seq, d_model)
```

```python
import functools

import jax
import jax.numpy as jnp
from jax import lax
from jax.experimental import pallas as pl
from jax.experimental.pallas import tpu as pltpu

D_MODEL = 2048
D_FF = 5632
ATTN_HEADS = 8
HEAD_DIM = 128
D_ATTN = ATTN_HEADS * HEAD_DIM
SGU_GROUPS = 8
SGU_GROUP_DIM = 128
D_SGU = SGU_GROUPS * SGU_GROUP_DIM
MOBA_BLOCK = 256
MOBA_TOPK = 3
SGU_CHUNK = 128
EPS = 1e-6
NEG_INF = -1e30

LANES = 128
AUG_DIM = 2 * HEAD_DIM
VMEM_LIMIT_BYTES = 60 * 1024 * 1024

FFN_TM = 1024
FFN_HEAD_TF = 256
FFN_TF = 512
PROJ_TM = 512
ATTN_TQ = 512
ATTN_KV_CHUNK = 512

F32 = jnp.float32
BF16 = jnp.bfloat16


def _rmsnorm(x, gain):
    return x * lax.rsqrt(jnp.mean(x * x, axis=-1, keepdims=True) + EPS) * gain


def _gelu_tanh(x):
    cdf = 0.5 * (1.0 + jnp.tanh(0.7978845608028654 * (x + 0.044715 * (x * x * x))))
    return x * cdf


def _dot(a, b):
    return jnp.dot(a, b, preferred_element_type=F32)


def _dot_nt(a, b, precision=None):
    return lax.dot_general(a, b, (((1,), (1,)), ((), ())), precision=precision,
                           preferred_element_type=F32)


def _params(*semantics):
    return pltpu.CompilerParams(dimension_semantics=semantics, vmem_limit_bytes=VMEM_LIMIT_BYTES)


def _resident(shape):
    return pl.BlockSpec(shape, lambda *_: (0,) * len(shape), pipeline_mode=pl.Buffered(1))


def _ffn_start(x_ref, g_ref, o_ref, h_scr):
    x = x_ref[...]
    h_scr[...] = _rmsnorm(x, g_ref[...]).astype(BF16)
    o_ref[...] = x


def _ffn_accumulate(h_scr, wg, wu, wd, o_ref):
    h = h_scr[...]
    gate = _dot(h, wg)
    up = _dot(h, wu)
    act = (gate / (1.0 + jnp.exp(-gate))) * up * 0.5
    o_ref[...] += _dot(act.astype(BF16), wd)


def _ffn_head_body(x_ref, g_ref, wg_ref, wu_ref, wd_ref, o_ref, wg_bf_ref, wu_bf_ref, wd_bf_ref, h_scr):
    @pl.when(pl.program_id(0) == 0)
    def _():
        _ffn_start(x_ref, g_ref, o_ref, h_scr)

    wg_bf_ref[...] = wg_ref[...].astype(BF16)
    wu_bf_ref[...] = wu_ref[...].astype(BF16)
    wd_bf_ref[...] = wd_ref[...].astype(BF16)
    _ffn_accumulate(h_scr, wg_bf_ref[...], wu_bf_ref[...], wd_bf_ref[...], o_ref)


def _ffn_tail_body(x_ref, g_ref, wg_ref, wu_ref, wd_ref, head_hbm, *rest, n_side):
    side_in, (o_ref, *side_out), h_scr = rest[:n_side], rest[n_side:-1], rest[-1]
    i = pl.program_id(0)
    j = pl.program_id(1)

    for src, dst in zip(side_in, side_out):
        dst[...] = src[...].astype(BF16)

    @pl.when((i == 0) & (j == 0))
    def _():
        pltpu.sync_copy(head_hbm, o_ref)

    @pl.when((i > 0) & (j == 0))
    def _():
        _ffn_start(x_ref, g_ref, o_ref, h_scr)

    @pl.when(i > 0)
    def _():
        _ffn_accumulate(h_scr, wg_ref[...], wu_ref[...], wd_ref[...], o_ref)


def _ffn(x, gain, w_gate, w_up, w_down, side=()):
    tokens = x.shape[0]
    n_tiles = tokens // FFN_TM
    vec = pl.BlockSpec((1, D_MODEL), lambda *_: (0, 0))

    tf = FFN_HEAD_TF
    col = pl.BlockSpec((D_MODEL, tf), lambda j: (0, j))
    rowblk = pl.BlockSpec((tf, D_MODEL), lambda j: (j, 0))
    tile0 = pl.BlockSpec((FFN_TM, D_MODEL), lambda j: (0, 0))
    head, wg_bf, wu_bf, wd_bf = pl.pallas_call(
        _ffn_head_body,
        grid=(D_FF // tf,),
        in_specs=[tile0, vec, col, col, rowblk],
        out_specs=(tile0, col, col, rowblk),
        out_shape=(jax.ShapeDtypeStruct((FFN_TM, D_MODEL), F32),
                   jax.ShapeDtypeStruct((D_MODEL, D_FF), BF16),
                   jax.ShapeDtypeStruct((D_MODEL, D_FF), BF16),
                   jax.ShapeDtypeStruct((D_FF, D_MODEL), BF16)),
        scratch_shapes=[pltpu.VMEM((FFN_TM, D_MODEL), BF16)],
        compiler_params=_params("arbitrary"),
        name="ffn_head",
    )(x, gain, w_gate, w_up, w_down)

    tf = FFN_TF
    wcol = lambda i, j: (0, jnp.where(i == 0, 0, j))
    wrow = lambda i, j: (jnp.where(i == 0, 0, j), 0)
    row = pl.BlockSpec((FFN_TM, D_MODEL), lambda i, j: (i, 0))
    n_steps = D_FF // tf
    side_specs, start = [], 0
    for w in side:
        n_side_tiles = w.shape[1] // LANES
        side_specs.append(pl.BlockSpec(
            (w.shape[0], LANES),
            lambda i, j, start=start, last=n_side_tiles - 1: (0, jnp.clip(i * n_steps + j - start, 0, last))))
        start += n_side_tiles
    assert start <= n_tiles * n_steps
    outs = pl.pallas_call(
        functools.partial(_ffn_tail_body, n_side=len(side)),
        grid=(n_tiles, n_steps),
        in_specs=[row, vec, pl.BlockSpec((D_MODEL, tf), wcol), pl.BlockSpec((D_MODEL, tf), wcol),
                  pl.BlockSpec((tf, D_MODEL), wrow), pl.BlockSpec(memory_space=pl.ANY), *side_specs],
        out_specs=(row, *side_specs),
        out_shape=(jax.ShapeDtypeStruct((tokens, D_MODEL), F32),
                   *(jax.ShapeDtypeStruct(w.shape, BF16) for w in side)),
        scratch_shapes=[pltpu.VMEM((FFN_TM, D_MODEL), BF16)],
        compiler_params=_params("arbitrary", "arbitrary"),
        name="ffn_tail",
    )(x, gain, wg_bf, wu_bf, wd_bf, head, *side)
    return outs if side else outs[0]


def _qkv_body(x_ref, gx_ref, w_ref, gq_ref, gk_ref, qa_ref, ka_ref, v_ref, km_scr, *, tiles_per_batch):
    tm = PROJ_TM
    blocks_per_batch = tiles_per_batch * (tm // MOBA_BLOCK)
    i = pl.program_id(0)
    n0 = (i % tiles_per_batch) * (tm // MOBA_BLOCK)

    @pl.when(i == 0)
    def _init():
        km_scr[...] = jnp.zeros_like(km_scr)

    h = _rmsnorm(x_ref[...], gx_ref[...]).astype(BF16)
    q = _dot(h, w_ref[:, 0:D_ATTN])
    k = _dot(h, w_ref[:, D_ATTN:2 * D_ATTN])
    v_ref[...] = _dot(h, w_ref[:, 2 * D_ATTN:3 * D_ATTN]).astype(BF16)

    scale = HEAD_DIM ** -0.5
    blk_rows = lax.broadcasted_iota(jnp.int32, (blocks_per_batch, HEAD_DIM), 0)
    n_iota = lax.broadcasted_iota(jnp.int32, (blocks_per_batch, tm), 0)
    lane_tok = lax.broadcasted_iota(jnp.int32, (blocks_per_batch, tm), 1)
    q_blk = n0 + lane_tok // MOBA_BLOCK
    valid = n_iota < q_blk
    tok_blk = n0 + lax.broadcasted_iota(jnp.int32, (tm, LANES), 0) // MOBA_BLOCK
    onehot = jnp.where(lax.broadcasted_iota(jnp.int32, (tm, LANES), 1) == tok_blk, 1.0, 0.0).astype(BF16)
    pad_rows = jnp.zeros((LANES - blocks_per_batch, tm), F32)

    for hh in range(ATTN_HEADS):
        sl = slice(hh * HEAD_DIM, (hh + 1) * HEAD_DIM)
        qn = _rmsnorm(q[:, sl], gq_ref[...])
        kn = _rmsnorm(k[:, sl], gk_ref[...])

        kmean = km_scr[:, sl]
        for b in range(tm // MOBA_BLOCK):
            km = jnp.mean(kn[b * MOBA_BLOCK:(b + 1) * MOBA_BLOCK], axis=0, keepdims=True)
            kmean = jnp.where(blk_rows == n0 + b, km, kmean)
        km_scr[:, sl] = kmean

        gate = _dot_nt(kmean, qn, precision=lax.Precision.HIGHEST)
        cnt = jnp.zeros((blocks_per_batch, tm), jnp.int32)
        for m in range(blocks_per_batch - 1):
            gm = gate[m:m + 1, :]
            beats = (gm > gate) | ((gm == gate) & (m < n_iota))
            cnt = cnt + jnp.where(beats & (m < q_blk), 1, 0)
        sel = ((cnt < MOBA_TOPK) & valid) | (n_iota == q_blk)
        bias_t = jnp.concatenate([jnp.where(sel, 0.0, NEG_INF), pad_rows], axis=0)
        bias = bias_t.T

        a0 = hh * AUG_DIM
        qa_ref[:, a0:a0 + HEAD_DIM] = (qn * scale).astype(BF16)
        qa_ref[:, a0 + HEAD_DIM:a0 + AUG_DIM] = bias.astype(BF16)
        ka_ref[:, a0:a0 + HEAD_DIM] = kn.astype(BF16)
        ka_ref[:, a0 + HEAD_DIM:a0 + AUG_DIM] = onehot


def _qkv(x, x_gain, w_in, q_gain, k_gain, seq):
    tokens = x.shape[0]
    tm = PROJ_TM
    tiles_per_batch = seq // tm
    blocks_per_batch = seq // MOBA_BLOCK
    assert blocks_per_batch <= LANES and blocks_per_batch % 8 == 0
    row = lambda width: pl.BlockSpec((tm, width), lambda i: (i, 0))
    return pl.pallas_call(
        functools.partial(_qkv_body, tiles_per_batch=tiles_per_batch),
        grid=(tokens // tm,),
        in_specs=[row(D_MODEL), _resident((1, D_MODEL)), _resident((D_MODEL, 3 * D_ATTN)),
                  _resident((1, HEAD_DIM)), _resident((1, HEAD_DIM))],
        out_specs=(row(ATTN_HEADS * AUG_DIM), row(ATTN_HEADS * AUG_DIM), row(D_ATTN)),
        out_shape=(jax.ShapeDtypeStruct((tokens, ATTN_HEADS * AUG_DIM), BF16),
                   jax.ShapeDtypeStruct((tokens, ATTN_HEADS * AUG_DIM), BF16),
                   jax.ShapeDtypeStruct((tokens, D_ATTN), BF16)),
        scratch_shapes=[pltpu.VMEM((blocks_per_batch, D_ATTN), F32)],
        compiler_params=_params("arbitrary"),
        name="qkv_gate",
    )(x, x_gain, w_in, q_gain, k_gain)


def _sgu_body(x_ref, gx_ref, wu_ref, wg_ref, lng_ref, lnb_ref, ws_ref, bs_ref, og_ref, o_ref, sg_scr):
    tm = PROJ_TM
    h = _rmsnorm(x_ref[...], gx_ref[...]).astype(BF16)
    u = _gelu_tanh(_dot(h, wu_ref[...]))
    g = _gelu_tanh(_dot(h, wg_ref[...]))
    r = lax.broadcasted_iota(jnp.int32, (SGU_CHUNK, SGU_CHUNK), 0)
    c = lax.broadcasted_iota(jnp.int32, (SGU_CHUNK, SGU_CHUNK), 1)
    for gi in range(SGU_GROUPS):
        sl = slice(gi * SGU_GROUP_DIM, (gi + 1) * SGU_GROUP_DIM)
        gg = g[:, sl]
        d = gg - jnp.mean(gg, axis=-1, keepdims=True)
        var = jnp.mean(d * d, axis=-1, keepdims=True)
        gn = (d * lax.rsqrt(var + EPS) * lng_ref[:, sl] + lnb_ref[:, sl]).astype(BF16)
        w_causal = jnp.where(r >= c, ws_ref[gi], 0.0).astype(BF16)
        b_col = bs_ref[:, gi:gi + 1]
        for ci in range(tm // SGU_CHUNK):
            rows = slice(ci * SGU_CHUNK, (ci + 1) * SGU_CHUNK)
            mixed = _dot(w_causal, gn[rows]) + b_col
            sg_scr[rows, sl] = u[rows, sl] * mixed
    o_ref[...] = _rmsnorm(sg_scr[...], og_ref[...]).astype(BF16)


def _sgu(x, x_gain, w_in, ln_gain, ln_bias, w_spatial, b_spatial_t, out_gain):
    tokens = x.shape[0]
    tm = PROJ_TM
    w_block = lambda c: pl.BlockSpec((D_MODEL, D_SGU), lambda i: (0, c), pipeline_mode=pl.Buffered(1))
    first = 3 * D_ATTN // D_SGU
    return pl.pallas_call(
        _sgu_body,
        grid=(tokens // tm,),
        in_specs=[pl.BlockSpec((tm, D_MODEL), lambda i: (i, 0)), _resident((1, D_MODEL)),
                  w_block(first), w_block(first + 1),
                  _resident((1, D_SGU)), _resident((1, D_SGU)),
                  _resident((SGU_GROUPS, SGU_CHUNK, SGU_CHUNK)),
                  _resident((SGU_CHUNK, SGU_GROUPS)),
                  _resident((1, D_SGU))],
        out_specs=pl.BlockSpec((tm, D_SGU), lambda i: (i, 0)),
        out_shape=jax.ShapeDtypeStruct((tokens, D_SGU), BF16),
        scratch_shapes=[pltpu.VMEM((tm, D_SGU), F32)],
        compiler_params=_params("parallel"),
        name="sgu",
    )(x, x_gain, w_in, w_in, ln_gain, ln_bias, w_spatial, b_spatial_t, out_gain)


def _attn_scores(t, q_ref, k_ref, s_ref, m_ref):
    tq, kc = ATTN_TQ, ATTN_KV_CHUNK
    m_part = None
    for c in range(t + 1):
        s = _dot_nt(q_ref[...], k_ref[c * kc:(c + 1) * kc, :])
        if c == t:
            r = lax.broadcasted_iota(jnp.int32, (tq, kc), 0)
            col = lax.broadcasted_iota(jnp.int32, (tq, kc), 1)
            s = jnp.where(col <= r, s, NEG_INF)
        s_ref[:, c * kc:(c + 1) * kc] = s
        for j in range(kc // LANES):
            sj = s[:, j * LANES:(j + 1) * LANES]
            m_part = sj if m_part is None else jnp.maximum(m_part, sj)
    m_ref[...] = jnp.broadcast_to(jnp.max(m_part, axis=-1, keepdims=True), (tq, LANES))


def _attn_values(t, v_ref, o_ref, s_ref, m_ref):
    tq, kc = ATTN_TQ, ATTN_KV_CHUNK
    m = m_ref[...]
    l_part = jnp.zeros((tq, LANES), F32)
    acc = jnp.zeros((tq, HEAD_DIM), F32)
    for c in range(t + 1):
        p_tiles = []
        for j in range(kc // LANES):
            pj = jnp.exp(s_ref[:, c * kc + j * LANES:c * kc + (j + 1) * LANES] - m)
            l_part = l_part + pj
            p_tiles.append(pj.astype(BF16))
        acc = acc + _dot(jnp.concatenate(p_tiles, axis=1), v_ref[c * kc:(c + 1) * kc, :])
    o_ref[...] = acc / jnp.sum(l_part, axis=-1, keepdims=True)


def _attn_body(q_ref, k_ref, v_ref, o_ref, s_scr, m_scr):
    t = pl.program_id(2)
    n_tiles = k_ref.shape[0] // ATTN_TQ

    def step(ts):
        if ts == 0:
            o_ref[...] = jnp.zeros_like(o_ref)
        else:
            _attn_values(ts - 1, v_ref, o_ref, s_scr.at[(ts - 1) % 2], m_scr.at[(ts - 1) % 2])
        if ts < n_tiles:
            _attn_scores(ts, q_ref, k_ref, s_scr.at[ts % 2], m_scr.at[ts % 2])

    for ts in range(n_tiles + 1):
        pl.when(t == ts)(functools.partial(step, ts))


def _attention(qa, ka, v, batch, seq):
    tokens = batch * seq
    q_tiles = seq // ATTN_TQ
    return pl.pallas_call(
        _attn_body,
        grid=(batch, ATTN_HEADS, q_tiles + 1),
        in_specs=[pl.BlockSpec((ATTN_TQ, AUG_DIM),
                               lambda b, h, i: (b * q_tiles + jnp.minimum(i, q_tiles - 1), h)),
                  pl.BlockSpec((seq, AUG_DIM), lambda b, h, i: (b, h)),
                  pl.BlockSpec((seq, HEAD_DIM), lambda b, h, i: (b, h))],
        out_specs=pl.BlockSpec((ATTN_TQ, HEAD_DIM),
                               lambda b, h, i: (b * q_tiles + jnp.maximum(i - 1, 0), h)),
        out_shape=jax.ShapeDtypeStruct((tokens, D_ATTN), F32),
        scratch_shapes=[pltpu.VMEM((2, ATTN_TQ, seq), F32), pltpu.VMEM((2, ATTN_TQ, LANES), F32)],
        compiler_params=_params("parallel", "parallel", "arbitrary"),
        name="moba_attn",
    )(qa, ka, v)


def _outproj_body(a_ref, s_ref, x_ref, ga_ref, w_ref, o_ref):
    an = _rmsnorm(a_ref[...], ga_ref[...]).astype(BF16)
    o_ref[...] = (x_ref[...] + _dot(an, w_ref[0:D_ATTN, :])
                  + _dot(s_ref[...], w_ref[D_ATTN:D_ATTN + D_SGU, :]))


def _outproj(attn, sgu, x, attn_gain, w_out):
    tokens = x.shape[0]
    tm = PROJ_TM
    row = lambda width: pl.BlockSpec((tm, width), lambda i: (i, 0))
    return pl.pallas_call(
        _outproj_body,
        grid=(tokens // tm,),
        in_specs=[row(D_ATTN), row(D_SGU), row(D_MODEL), _resident((1, D_ATTN)),
                  _resident((D_ATTN + D_SGU, D_MODEL))],
        out_specs=row(D_MODEL),
        out_shape=jax.ShapeDtypeStruct((tokens, D_MODEL), F32),
        compiler_params=_params("parallel"),
        name="outproj",
    )(attn, sgu, x, attn_gain, w_out)


def kernel(x, ffn1_norm, ffn1_w_gate, ffn1_w_up, ffn1_w_down, mix_norm, w_in, q_norm, k_norm, sgu_ln_gain, sgu_ln_bias, sgu_w_spatial, sgu_b_spatial, attn_out_gain, sgu_out_gain, w_out, ffn2_norm, ffn2_w_gate, ffn2_w_up, ffn2_w_down):
    batch, seq, d_model = x.shape
    depth = ffn1_norm.shape[0]
    assert d_model == D_MODEL and seq % PROJ_TM == 0 and (batch * seq) % FFN_TM == 0
    xt = x.reshape(batch * seq, d_model)
    vec = lambda a: a.reshape(1, -1)
    for l in range(depth):
        xt, w_in_bf, w_out_bf = _ffn(xt, vec(ffn1_norm[l]), ffn1_w_gate[l], ffn1_w_up[l], ffn1_w_down[l],
                                     side=(w_in[l], w_out[l]))
        qa, ka, v = _qkv(xt, vec(mix_norm[l]), w_in_bf, vec(q_norm[l]), vec(k_norm[l]), seq)
        sgu = _sgu(xt, vec(mix_norm[l]), w_in_bf, vec(sgu_ln_gain[l]), vec(sgu_ln_bias[l]),
                   sgu_w_spatial[l], sgu_b_spatial[l].T, vec(sgu_out_gain[l]))
        attn = _attention(qa, ka, v, batch, seq)
        xt = _outproj(attn, sgu, xt, vec(attn_out_gain[l]), w_out_bf)
        xt = _ffn(xt, vec(ffn2_norm[l]), ffn2_w_gate[l], ffn2_w_up[l], ffn2_w_down[l])
    return xt.reshape(batch, seq, d_model)
```

```python
import functools

import jax
import jax.numpy as jnp
from jax import lax
from jax.experimental import pallas as pl
from jax.experimental.pallas import tpu as pltpu

D_MODEL = 2048
D_FF = 5632
ATTN_HEADS = 8
HEAD_DIM = 128
D_ATTN = ATTN_HEADS * HEAD_DIM
SGU_GROUPS = 8
SGU_GROUP_DIM = 128
D_SGU = SGU_GROUPS * SGU_GROUP_DIM
MOBA_BLOCK = 256
MOBA_TOPK = 3
SGU_CHUNK = 128
EPS = 1e-6
NEG_INF = -1e30

LANES = 128
AUG_DIM = 2 * HEAD_DIM
VMEM_LIMIT_BYTES = 60 * 1024 * 1024

FFN_TM = 1024
FFN_HEAD_TF = 256
FFN_TF = 512
PROJ_TM = 512
ATTN_TQ = 512
ATTN_KV_CHUNK = 512

F32 = jnp.float32
BF16 = jnp.bfloat16


def _rmsnorm(x, gain):
    return x * lax.rsqrt(jnp.mean(x * x, axis=-1, keepdims=True) + EPS) * gain


def _gelu_tanh(x):
    cdf = 0.5 * (1.0 + jnp.tanh(0.7978845608028654 * (x + 0.044715 * (x * x * x))))
    return x * cdf


def _dot(a, b):
    return jnp.dot(a, b, preferred_element_type=F32)


def _dot_nt(a, b, precision=None):
    return lax.dot_general(a, b, (((1,), (1,)), ((), ())), precision=precision,
                           preferred_element_type=F32)


def _params(*semantics):
    return pltpu.CompilerParams(dimension_semantics=semantics, vmem_limit_bytes=VMEM_LIMIT_BYTES)


def _resident(shape):
    return pl.BlockSpec(shape, lambda *_: (0,) * len(shape), pipeline_mode=pl.Buffered(1))


def _ffn_start(x_ref, g_ref, o_ref, h_scr):
    x = x_ref[...]
    h_scr[...] = _rmsnorm(x, g_ref[...]).astype(BF16)
    o_ref[...] = x


def _ffn_accumulate(h_scr, wg, wu, wd, o_ref):
    h = h_scr[...]
    gate = _dot(h, wg)
    up = _dot(h, wu)
    act = (gate / (1.0 + jnp.exp(-gate))) * up * 0.5
    o_ref[...] += _dot(act.astype(BF16), wd)


def _ffn_head_body(x_ref, g_ref, wg_ref, wu_ref, wd_ref, o_ref, wg_bf_ref, wu_bf_ref, wd_bf_ref, h_scr):
    @pl.when(pl.program_id(0) == 0)
    def _():
        _ffn_start(x_ref, g_ref, o_ref, h_scr)

    wg_bf_ref[...] = wg_ref[...].astype(BF16)
    wu_bf_ref[...] = wu_ref[...].astype(BF16)
    wd_bf_ref[...] = wd_ref[...].astype(BF16)
    _ffn_accumulate(h_scr, wg_bf_ref[...], wu_bf_ref[...], wd_bf_ref[...], o_ref)


def _ffn_tail_body(x_ref, g_ref, wg_ref, wu_ref, wd_ref, head_hbm, *rest, n_side):
    side_in, (o_ref, *side_out), h_scr = rest[:n_side], rest[n_side:-1], rest[-1]
    i = pl.program_id(0)
    j = pl.program_id(1)

    for src, dst in zip(side_in, side_out):
        dst[...] = src[...].astype(BF16)

    @pl.when((i == 0) & (j == 0))
    def _():
        pltpu.sync_copy(head_hbm, o_ref)

    @pl.when((i > 0) & (j == 0))
    def _():
        _ffn_start(x_ref, g_ref, o_ref, h_scr)

    @pl.when(i > 0)
    def _():
        _ffn_accumulate(h_scr, wg_ref[...], wu_ref[...], wd_ref[...], o_ref)


def _ffn(x, gain, w_gate, w_up, w_down, side=()):
    tokens = x.shape[0]
    n_tiles = tokens // FFN_TM
    vec = pl.BlockSpec((1, D_MODEL), lambda *_: (0, 0))

    tf = FFN_HEAD_TF
    col = pl.BlockSpec((D_MODEL, tf), lambda j: (0, j))
    rowblk = pl.BlockSpec((tf, D_MODEL), lambda j: (j, 0))
    tile0 = pl.BlockSpec((FFN_TM, D_MODEL), lambda j: (0, 0))
    head, wg_bf, wu_bf, wd_bf = pl.pallas_call(
        _ffn_head_body,
        grid=(D_FF // tf,),
        in_specs=[tile0, vec, col, col, rowblk],
        out_specs=(tile0, col, col, rowblk),
        out_shape=(jax.ShapeDtypeStruct((FFN_TM, D_MODEL), F32),
                   jax.ShapeDtypeStruct((D_MODEL, D_FF), BF16),
                   jax.ShapeDtypeStruct((D_MODEL, D_FF), BF16),
                   jax.ShapeDtypeStruct((D_FF, D_MODEL), BF16)),
        scratch_shapes=[pltpu.VMEM((FFN_TM, D_MODEL), BF16)],
        compiler_params=_params("arbitrary"),
        name="ffn_head",
    )(x, gain, w_gate, w_up, w_down)

    tf = FFN_TF
    wcol = lambda i, j: (0, jnp.where(i == 0, 0, j))
    wrow = lambda i, j: (jnp.where(i == 0, 0, j), 0)
    row = pl.BlockSpec((FFN_TM, D_MODEL), lambda i, j: (i, 0))
    n_steps = D_FF // tf
    side_specs, start = [], 0
    for w in side:
        n_side_tiles = w.shape[1] // LANES
        side_specs.append(pl.BlockSpec(
            (w.shape[0], LANES),
            lambda i, j, start=start, last=n_side_tiles - 1: (0, jnp.clip(i * n_steps + j - start, 0, last))))
        start += n_side_tiles
    assert start <= n_tiles * n_steps
    outs = pl.pallas_call(
        functools.partial(_ffn_tail_body, n_side=len(side)),
        grid=(n_tiles, n_steps),
        in_specs=[row, vec, pl.BlockSpec((D_MODEL, tf), wcol), pl.BlockSpec((D_MODEL, tf), wcol),
                  pl.BlockSpec((tf, D_MODEL), wrow), pl.BlockSpec(memory_space=pl.ANY), *side_specs],
        out_specs=(row, *side_specs),
        out_shape=(jax.ShapeDtypeStruct((tokens, D_MODEL), F32),
                   *(jax.ShapeDtypeStruct(w.shape, BF16) for w in side)),
        scratch_shapes=[pltpu.VMEM((FFN_TM, D_MODEL), BF16)],
        compiler_params=_params("arbitrary", "arbitrary"),
        name="ffn_tail",
    )(x, gain, wg_bf, wu_bf, wd_bf, head, *side)
    return outs if side else outs[0]


def _qkv_body(x_ref, gx_ref, w_ref, gq_ref, gk_ref, qa_ref, ka_ref, v_ref, km_scr, *, tiles_per_batch):
    tm = PROJ_TM
    blocks_per_batch = tiles_per_batch * (tm // MOBA_BLOCK)
    i = pl.program_id(0)
    n0 = (i % tiles_per_batch) * (tm // MOBA_BLOCK)

    @pl.when(i == 0)
    def _init():
        km_scr[...] = jnp.zeros_like(km_scr)

    h = _rmsnorm(x_ref[...], gx_ref[...]).astype(BF16)
    q = _dot(h, w_ref[:, 0:D_ATTN])
    k = _dot(h, w_ref[:, D_ATTN:2 * D_ATTN])
    v_ref[...] = _dot(h, w_ref[:, 2 * D_ATTN:3 * D_ATTN]).astype(BF16)

    scale = HEAD_DIM ** -0.5
    blk_rows = lax.broadcasted_iota(jnp.int32, (blocks_per_batch, HEAD_DIM), 0)
    n_iota = lax.broadcasted_iota(jnp.int32, (blocks_per_batch, tm), 0)
    lane_tok = lax.broadcasted_iota(jnp.int32, (blocks_per_batch, tm), 1)
    q_blk = n0 + lane_tok // MOBA_BLOCK
    valid = n_iota < q_blk
    tok_blk = n0 + lax.broadcasted_iota(jnp.int32, (tm, LANES), 0) // MOBA_BLOCK
    onehot = jnp.where(lax.broadcasted_iota(jnp.int32, (tm, LANES), 1) == tok_blk, 1.0, 0.0).astype(BF16)
    pad_rows = jnp.zeros((LANES - blocks_per_batch, tm), F32)

    for hh in range(ATTN_HEADS):
        sl = slice(hh * HEAD_DIM, (hh + 1) * HEAD_DIM)
        qn = _rmsnorm(q[:, sl], gq_ref[...])
        kn = _rmsnorm(k[:, sl], gk_ref[...])

        kmean = km_scr[:, sl]
        for b in range(tm // MOBA_BLOCK):
            km = jnp.mean(kn[b * MOBA_BLOCK:(b + 1) * MOBA_BLOCK], axis=0, keepdims=True)
            kmean = jnp.where(blk_rows == n0 + b, km, kmean)
        km_scr[:, sl] = kmean

        gate = _dot_nt(kmean, qn, precision=lax.Precision.HIGHEST)
        cnt = jnp.zeros((blocks_per_batch, tm), jnp.int32)
        for m in range(blocks_per_batch - 1):
            gm = gate[m:m + 1, :]
            beats = (gm > gate) | ((gm == gate) & (m < n_iota))
            cnt = cnt + jnp.where(beats & (m < q_blk), 1, 0)
        sel = ((cnt < MOBA_TOPK) & valid) | (n_iota == q_blk)
        bias_t = jnp.concatenate([jnp.where(sel, 0.0, NEG_INF), pad_rows], axis=0)
        bias = bias_t.T

        a0 = hh * AUG_DIM
        qa_ref[:, a0:a0 + HEAD_DIM] = (qn * scale).astype(BF16)
        qa_ref[:, a0 + HEAD_DIM:a0 + AUG_DIM] = bias.astype(BF16)
        ka_ref[:, a0:a0 + HEAD_DIM] = kn.astype(BF16)
        ka_ref[:, a0 + HEAD_DIM:a0 + AUG_DIM] = onehot


def _qkv(x, x_gain, w_in, q_gain, k_gain, seq):
    tokens = x.shape[0]
    tm = PROJ_TM
    tiles_per_batch = seq // tm
    blocks_per_batch = seq // MOBA_BLOCK
    assert blocks_per_batch <= LANES and blocks_per_batch % 8 == 0
    row = lambda width: pl.BlockSpec((tm, width), lambda i: (i, 0))
    return pl.pallas_call(
        functools.partial(_qkv_body, tiles_per_batch=tiles_per_batch),
        grid=(tokens // tm,),
        in_specs=[row(D_MODEL), _resident((1, D_MODEL)), _resident((D_MODEL, 3 * D_ATTN)),
                  _resident((1, HEAD_DIM)), _resident((1, HEAD_DIM))],
        out_specs=(row(ATTN_HEADS * AUG_DIM), row(ATTN_HEADS * AUG_DIM), row(D_ATTN)),
        out_shape=(jax.ShapeDtypeStruct((tokens, ATTN_HEADS * AUG_DIM), BF16),
                   jax.ShapeDtypeStruct((tokens, ATTN_HEADS * AUG_DIM), BF16),
                   jax.ShapeDtypeStruct((tokens, D_ATTN), BF16)),
        scratch_shapes=[pltpu.VMEM((blocks_per_batch, D_ATTN), F32)],
        compiler_params=_params("arbitrary"),
        name="qkv_gate",
    )(x, x_gain, w_in, q_gain, k_gain)


def _sgu_body(x_ref, gx_ref, wu_ref, wg_ref, lng_ref, lnb_ref, ws_ref, bs_ref, og_ref, o_ref, sg_scr):
    tm = PROJ_TM
    h = _rmsnorm(x_ref[...], gx_ref[...]).astype(BF16)
    u = _gelu_tanh(_dot(h, wu_ref[...]))
    g = _gelu_tanh(_dot(h, wg_ref[...]))
    r = lax.broadcasted_iota(jnp.int32, (SGU_CHUNK, SGU_CHUNK), 0)
    c = lax.broadcasted_iota(jnp.int32, (SGU_CHUNK, SGU_CHUNK), 1)
    for gi in range(SGU_GROUPS):
        sl = slice(gi * SGU_GROUP_DIM, (gi + 1) * SGU_GROUP_DIM)
        gg = g[:, sl]
        d = gg - jnp.mean(gg, axis=-1, keepdims=True)
        var = jnp.mean(d * d, axis=-1, keepdims=True)
        gn = (d * lax.rsqrt(var + EPS) * lng_ref[:, sl] + lnb_ref[:, sl]).astype(BF16)
        w_causal = jnp.where(r >= c, ws_ref[gi], 0.0).astype(BF16)
        b_col = bs_ref[:, gi:gi + 1]
        for ci in range(tm // SGU_CHUNK):
            rows = slice(ci * SGU_CHUNK, (ci + 1) * SGU_CHUNK)
            mixed = _dot(w_causal, gn[rows]) + b_col
            sg_scr[rows, sl] = u[rows, sl] * mixed
    o_ref[...] = _rmsnorm(sg_scr[...], og_ref[...]).astype(BF16)


def _sgu(x, x_gain, w_in, ln_gain, ln_bias, w_spatial, b_spatial_t, out_gain):
    tokens = x.shape[0]
    tm = PROJ_TM
    w_block = lambda c: pl.BlockSpec((D_MODEL, D_SGU), lambda i: (0, c), pipeline_mode=pl.Buffered(1))
    first = 3 * D_ATTN // D_SGU
    return pl.pallas_call(
        _sgu_body,
        grid=(tokens // tm,),
        in_specs=[pl.BlockSpec((tm, D_MODEL), lambda i: (i, 0)), _resident((1, D_MODEL)),
                  w_block(first), w_block(first + 1),
                  _resident((1, D_SGU)), _resident((1, D_SGU)),
                  _resident((SGU_GROUPS, SGU_CHUNK, SGU_CHUNK)),
                  _resident((SGU_CHUNK, SGU_GROUPS)),
                  _resident((1, D_SGU))],
        out_specs=pl.BlockSpec((tm, D_SGU), lambda i: (i, 0)),
        out_shape=jax.ShapeDtypeStruct((tokens, D_SGU), BF16),
        scratch_shapes=[pltpu.VMEM((tm, D_SGU), F32)],
        compiler_params=_params("parallel"),
        name="sgu",
    )(x, x_gain, w_in, w_in, ln_gain, ln_bias, w_spatial, b_spatial_t, out_gain)


def _attn_scores(t, q_ref, k_ref, s_ref, m_ref):
    tq, kc = ATTN_TQ, ATTN_KV_CHUNK
    m_part = None
    for c in range(t + 1):
        s = _dot_nt(q_ref[...], k_ref[c * kc:(c + 1) * kc, :])
        if c == t:
            r = lax.broadcasted_iota(jnp.int32, (tq, kc), 0)
            col = lax.broadcasted_iota(jnp.int32, (tq, kc), 1)
            s = jnp.where(col <= r, s, NEG_INF)
        s_ref[:, c * kc:(c + 1) * kc] = s
        for j in range(kc // LANES):
            sj = s[:, j * LANES:(j + 1) * LANES]
            m_part = sj if m_part is None else jnp.maximum(m_part, sj)
    m_ref[...] = jnp.broadcast_to(jnp.max(m_part, axis=-1, keepdims=True), (tq, LANES))


def _attn_values(t, va_ref, o_ref, s_ref, m_ref):
    tq, kc = ATTN_TQ, ATTN_KV_CHUNK
    m = m_ref[...]
    acc = jnp.zeros((tq, AUG_DIM), F32)
    for c in range(t + 1):
        p = jnp.concatenate(
            [jnp.exp(s_ref[:, c * kc + j * LANES:c * kc + (j + 1) * LANES] - m).astype(BF16)
             for j in range(kc // LANES)], axis=1)
        acc = acc + _dot(p, va_ref[c * kc:(c + 1) * kc, :])
    o_ref[...] = acc[:, 0:HEAD_DIM] / acc[:, HEAD_DIM:AUG_DIM]


def _attn_body(q_ref, k_ref, v_ref, o_ref, s_scr, m_scr, va_scr):
    t = pl.program_id(2)
    n_tiles = k_ref.shape[0] // ATTN_TQ

    def step(ts):
        if ts == 0:
            o_ref[...] = jnp.zeros_like(o_ref)
            va_scr[:, 0:HEAD_DIM] = v_ref[...]
            va_scr[:, HEAD_DIM:AUG_DIM] = jnp.ones((va_scr.shape[0], HEAD_DIM), BF16)
        else:
            _attn_values(ts - 1, va_scr, o_ref, s_scr.at[(ts - 1) % 2], m_scr.at[(ts - 1) % 2])
        if ts < n_tiles:
            _attn_scores(ts, q_ref, k_ref, s_scr.at[ts % 2], m_scr.at[ts % 2])

    for ts in range(n_tiles + 1):
        pl.when(t == ts)(functools.partial(step, ts))


def _attention(qa, ka, v, batch, seq):
    tokens = batch * seq
    q_tiles = seq // ATTN_TQ
    return pl.pallas_call(
        _attn_body,
        grid=(batch, ATTN_HEADS, q_tiles + 1),
        in_specs=[pl.BlockSpec((ATTN_TQ, AUG_DIM),
                               lambda b, h, i: (b * q_tiles + jnp.minimum(i, q_tiles - 1), h)),
                  pl.BlockSpec((seq, AUG_DIM), lambda b, h, i: (b, h)),
                  pl.BlockSpec((seq, HEAD_DIM), lambda b, h, i: (b, h))],
        out_specs=pl.BlockSpec((ATTN_TQ, HEAD_DIM),
                               lambda b, h, i: (b * q_tiles + jnp.maximum(i - 1, 0), h)),
        out_shape=jax.ShapeDtypeStruct((tokens, D_ATTN), F32),
        scratch_shapes=[pltpu.VMEM((2, ATTN_TQ, seq), F32), pltpu.VMEM((2, ATTN_TQ, LANES), F32),
                        pltpu.VMEM((seq, AUG_DIM), BF16)],
        compiler_params=_params("parallel", "parallel", "arbitrary"),
        name="moba_attn",
    )(qa, ka, v)


def _outproj_body(a_ref, s_ref, x_ref, ga_ref, w_ref, o_ref):
    an = _rmsnorm(a_ref[...], ga_ref[...]).astype(BF16)
    o_ref[...] = (x_ref[...] + _dot(an, w_ref[0:D_ATTN, :])
                  + _dot(s_ref[...], w_ref[D_ATTN:D_ATTN + D_SGU, :]))


def _outproj(attn, sgu, x, attn_gain, w_out):
    tokens = x.shape[0]
    tm = PROJ_TM
    row = lambda width: pl.BlockSpec((tm, width), lambda i: (i, 0))
    return pl.pallas_call(
        _outproj_body,
        grid=(tokens // tm,),
        in_specs=[row(D_ATTN), row(D_SGU), row(D_MODEL), _resident((1, D_ATTN)),
                  _resident((D_ATTN + D_SGU, D_MODEL))],
        out_specs=row(D_MODEL),
        out_shape=jax.ShapeDtypeStruct((tokens, D_MODEL), F32),
        compiler_params=_params("parallel"),
        name="outproj",
    )(attn, sgu, x, attn_gain, w_out)


def kernel(x, ffn1_norm, ffn1_w_gate, ffn1_w_up, ffn1_w_down, mix_norm, w_in, q_norm, k_norm, sgu_ln_gain, sgu_ln_bias, sgu_w_spatial, sgu_b_spatial, attn_out_gain, sgu_out_gain, w_out, ffn2_norm, ffn2_w_gate, ffn2_w_up, ffn2_w_down):
    batch, seq, d_model = x.shape
    depth = ffn1_norm.shape[0]
    assert d_model == D_MODEL and seq % PROJ_TM == 0 and (batch * seq) % FFN_TM == 0
    xt = x.reshape(batch * seq, d_model)
    vec = lambda a: a.reshape(1, -1)
    for l in range(depth):
        xt, w_in_bf, w_out_bf = _ffn(xt, vec(ffn1_norm[l]), ffn1_w_gate[l], ffn1_w_up[l], ffn1_w_down[l],
                                     side=(w_in[l], w_out[l]))
        qa, ka, v = _qkv(xt, vec(mix_norm[l]), w_in_bf, vec(q_norm[l]), vec(k_norm[l]), seq)
        sgu = _sgu(xt, vec(mix_norm[l]), w_in_bf, vec(sgu_ln_gain[l]), vec(sgu_ln_bias[l]),
                   sgu_w_spatial[l], sgu_b_spatial[l].T, vec(sgu_out_gain[l]))
        attn = _attention(qa, ka, v, batch, seq)
        xt = _outproj(attn, sgu, xt, vec(attn_out_gain[l]), w_out_bf)
        xt = _ffn(xt, vec(ffn2_norm[l]), ffn2_w_gate[l], ffn2_w_up[l], ffn2_w_down[l])
    return xt.reshape(batch, seq, d_model)
```

```python
import functools

import jax
import jax.numpy as jnp
from jax import lax
from jax.experimental import pallas as pl
from jax.experimental.pallas import tpu as pltpu

D_MODEL = 2048
D_FF = 5632
ATTN_HEADS = 8
HEAD_DIM = 128
D_ATTN = ATTN_HEADS * HEAD_DIM
SGU_GROUPS = 8
SGU_GROUP_DIM = 128
D_SGU = SGU_GROUPS * SGU_GROUP_DIM
MOBA_BLOCK = 256
MOBA_TOPK = 3
SGU_CHUNK = 128
EPS = 1e-6
NEG_INF = -1e30

LANES = 128
AUG_DIM = 2 * HEAD_DIM
VMEM_LIMIT_BYTES = 60 * 1024 * 1024

FFN_TM = 1024
FFN_HEAD_TF = 256
FFN_TF = 512
PROJ_TM = 512
ATTN_TQ = 512
ATTN_KV_CHUNK = 512

F32 = jnp.float32
BF16 = jnp.bfloat16


def _rmsnorm(x, gain):
    return x * lax.rsqrt(jnp.mean(x * x, axis=-1, keepdims=True) + EPS) * gain


def _gelu_tanh(x):
    cdf = 0.5 * (1.0 + jnp.tanh(0.7978845608028654 * (x + 0.044715 * (x * x * x))))
    return x * cdf


def _dot(a, b):
    return jnp.dot(a, b, preferred_element_type=F32)


def _dot_nt(a, b, precision=None):
    return lax.dot_general(a, b, (((1,), (1,)), ((), ())), precision=precision,
                           preferred_element_type=F32)


def _params(*semantics):
    return pltpu.CompilerParams(dimension_semantics=semantics, vmem_limit_bytes=VMEM_LIMIT_BYTES)


def _resident(shape):
    return pl.BlockSpec(shape, lambda *_: (0,) * len(shape), pipeline_mode=pl.Buffered(1))


def _ffn_start(x_ref, g_ref, o_ref, h_scr):
    x = x_ref[...]
    h_scr[...] = _rmsnorm(x, g_ref[...]).astype(BF16)
    o_ref[...] = x


def _ffn_accumulate(h_scr, wg, wu, wd, o_ref):
    h = h_scr[...]
    gate = _dot(h, wg)
    up = _dot(h, wu)
    act = (gate / (1.0 + jnp.exp(-gate))) * up * 0.5
    o_ref[...] += _dot(act.astype(BF16), wd)


def _ffn_head_body(x_ref, g_ref, wg_ref, wu_ref, wd_ref, o_ref, wg_bf_ref, wu_bf_ref, wd_bf_ref, h_scr):
    @pl.when(pl.program_id(0) == 0)
    def _():
        _ffn_start(x_ref, g_ref, o_ref, h_scr)

    wg_bf_ref[...] = wg_ref[...].astype(BF16)
    wu_bf_ref[...] = wu_ref[...].astype(BF16)
    wd_bf_ref[...] = wd_ref[...].astype(BF16)
    _ffn_accumulate(h_scr, wg_bf_ref[...], wu_bf_ref[...], wd_bf_ref[...], o_ref)


def _ffn_tail_body(x_ref, g_ref, wg_ref, wu_ref, wd_ref, head_hbm, *rest, n_side):
    side_in, (o_ref, *side_out), h_scr = rest[:n_side], rest[n_side:-1], rest[-1]
    i = pl.program_id(0)
    j = pl.program_id(1)

    for src, dst in zip(side_in, side_out):
        dst[...] = src[...].astype(BF16)

    @pl.when((i == 0) & (j == 0))
    def _():
        pltpu.sync_copy(head_hbm, o_ref)

    @pl.when((i > 0) & (j == 0))
    def _():
        _ffn_start(x_ref, g_ref, o_ref, h_scr)

    @pl.when(i > 0)
    def _():
        _ffn_accumulate(h_scr, wg_ref[...], wu_ref[...], wd_ref[...], o_ref)


def _ffn(x, gain, w_gate, w_up, w_down, side=()):
    tokens = x.shape[0]
    n_tiles = tokens // FFN_TM
    vec = pl.BlockSpec((1, D_MODEL), lambda *_: (0, 0))

    tf = FFN_HEAD_TF
    col = pl.BlockSpec((D_MODEL, tf), lambda j: (0, j))
    rowblk = pl.BlockSpec((tf, D_MODEL), lambda j: (j, 0))
    tile0 = pl.BlockSpec((FFN_TM, D_MODEL), lambda j: (0, 0))
    head, wg_bf, wu_bf, wd_bf = pl.pallas_call(
        _ffn_head_body,
        grid=(D_FF // tf,),
        in_specs=[tile0, vec, col, col, rowblk],
        out_specs=(tile0, col, col, rowblk),
        out_shape=(jax.ShapeDtypeStruct((FFN_TM, D_MODEL), F32),
                   jax.ShapeDtypeStruct((D_MODEL, D_FF), BF16),
                   jax.ShapeDtypeStruct((D_MODEL, D_FF), BF16),
                   jax.ShapeDtypeStruct((D_FF, D_MODEL), BF16)),
        scratch_shapes=[pltpu.VMEM((FFN_TM, D_MODEL), BF16)],
        compiler_params=_params("arbitrary"),
        name="ffn_head",
    )(x, gain, w_gate, w_up, w_down)

    tf = FFN_TF
    wcol = lambda i, j: (0, jnp.where(i == 0, 0, j))
    wrow = lambda i, j: (jnp.where(i == 0, 0, j), 0)
    row = pl.BlockSpec((FFN_TM, D_MODEL), lambda i, j: (i, 0))
    n_steps = D_FF // tf
    side_specs, start = [], 0
    for w in side:
        n_side_tiles = w.shape[1] // LANES
        side_specs.append(pl.BlockSpec(
            (w.shape[0], LANES),
            lambda i, j, start=start, last=n_side_tiles - 1: (0, jnp.clip(i * n_steps + j - start, 0, last))))
        start += n_side_tiles
    assert start <= n_tiles * n_steps
    outs = pl.pallas_call(
        functools.partial(_ffn_tail_body, n_side=len(side)),
        grid=(n_tiles, n_steps),
        in_specs=[row, vec, pl.BlockSpec((D_MODEL, tf), wcol), pl.BlockSpec((D_MODEL, tf), wcol),
                  pl.BlockSpec((tf, D_MODEL), wrow), pl.BlockSpec(memory_space=pl.ANY), *side_specs],
        out_specs=(row, *side_specs),
        out_shape=(jax.ShapeDtypeStruct((tokens, D_MODEL), F32),
                   *(jax.ShapeDtypeStruct(w.shape, BF16) for w in side)),
        scratch_shapes=[pltpu.VMEM((FFN_TM, D_MODEL), BF16)],
        compiler_params=_params("arbitrary", "arbitrary"),
        name="ffn_tail",
    )(x, gain, wg_bf, wu_bf, wd_bf, head, *side)
    return outs if side else outs[0]


def _qkv_body(x_ref, gx_ref, w_ref, gq_ref, gk_ref, qa_ref, ka_ref, v_ref, km_scr, *, tiles_per_batch):
    tm = PROJ_TM
    blocks_per_batch = tiles_per_batch * (tm // MOBA_BLOCK)
    i = pl.program_id(0)
    n0 = (i % tiles_per_batch) * (tm // MOBA_BLOCK)

    @pl.when(i == 0)
    def _init():
        km_scr[...] = jnp.zeros_like(km_scr)

    h = _rmsnorm(x_ref[...], gx_ref[...]).astype(BF16)
    q = _dot(h, w_ref[:, 0:D_ATTN])
    k = _dot(h, w_ref[:, D_ATTN:2 * D_ATTN])
    v_ref[...] = _dot(h, w_ref[:, 2 * D_ATTN:3 * D_ATTN]).astype(BF16)

    scale = HEAD_DIM ** -0.5
    blk_rows = lax.broadcasted_iota(jnp.int32, (blocks_per_batch, HEAD_DIM), 0)
    n_iota = lax.broadcasted_iota(jnp.int32, (blocks_per_batch, tm), 0)
    lane_tok = lax.broadcasted_iota(jnp.int32, (blocks_per_batch, tm), 1)
    q_blk = n0 + lane_tok // MOBA_BLOCK
    valid = n_iota < q_blk
    tok_blk = n0 + lax.broadcasted_iota(jnp.int32, (tm, LANES), 0) // MOBA_BLOCK
    onehot = jnp.where(lax.broadcasted_iota(jnp.int32, (tm, LANES), 1) == tok_blk, 1.0, 0.0).astype(BF16)
    pad_rows = jnp.zeros((LANES - blocks_per_batch, tm), F32)

    for hh in range(ATTN_HEADS):
        sl = slice(hh * HEAD_DIM, (hh + 1) * HEAD_DIM)
        qn = _rmsnorm(q[:, sl], gq_ref[...])
        kn = _rmsnorm(k[:, sl], gk_ref[...])

        kmean = km_scr[:, sl]
        for b in range(tm // MOBA_BLOCK):
            km = jnp.mean(kn[b * MOBA_BLOCK:(b + 1) * MOBA_BLOCK], axis=0, keepdims=True)
            kmean = jnp.where(blk_rows == n0 + b, km, kmean)
        km_scr[:, sl] = kmean

        gate = _dot_nt(kmean, qn, precision=lax.Precision.HIGHEST)
        cnt = jnp.zeros((blocks_per_batch, tm), jnp.int32)
        for m in range(blocks_per_batch - 1):
            gm = gate[m:m + 1, :]
            beats = (gm > gate) | ((gm == gate) & (m < n_iota))
            cnt = cnt + jnp.where(beats & (m < q_blk), 1, 0)
        sel = ((cnt < MOBA_TOPK) & valid) | (n_iota == q_blk)
        bias_t = jnp.concatenate([jnp.where(sel, 0.0, NEG_INF), pad_rows], axis=0)
        bias = bias_t.T

        a0 = hh * AUG_DIM
        qa_ref[:, a0:a0 + HEAD_DIM] = (qn * scale).astype(BF16)
        qa_ref[:, a0 + HEAD_DIM:a0 + AUG_DIM] = bias.astype(BF16)
        ka_ref[:, a0:a0 + HEAD_DIM] = kn.astype(BF16)
        ka_ref[:, a0 + HEAD_DIM:a0 + AUG_DIM] = onehot


def _qkv(x, x_gain, w_in, q_gain, k_gain, seq):
    tokens = x.shape[0]
    tm = PROJ_TM
    tiles_per_batch = seq // tm
    blocks_per_batch = seq // MOBA_BLOCK
    assert blocks_per_batch <= LANES and blocks_per_batch % 8 == 0
    row = lambda width: pl.BlockSpec((tm, width), lambda i: (i, 0))
    return pl.pallas_call(
        functools.partial(_qkv_body, tiles_per_batch=tiles_per_batch),
        grid=(tokens // tm,),
        in_specs=[row(D_MODEL), _resident((1, D_MODEL)), _resident((D_MODEL, 3 * D_ATTN)),
                  _resident((1, HEAD_DIM)), _resident((1, HEAD_DIM))],
        out_specs=(row(ATTN_HEADS * AUG_DIM), row(ATTN_HEADS * AUG_DIM), row(D_ATTN)),
        out_shape=(jax.ShapeDtypeStruct((tokens, ATTN_HEADS * AUG_DIM), BF16),
                   jax.ShapeDtypeStruct((tokens, ATTN_HEADS * AUG_DIM), BF16),
                   jax.ShapeDtypeStruct((tokens, D_ATTN), BF16)),
        scratch_shapes=[pltpu.VMEM((blocks_per_batch, D_ATTN), F32)],
        compiler_params=_params("arbitrary"),
        name="qkv_gate",
    )(x, x_gain, w_in, q_gain, k_gain)


def _sgu_body(x_ref, gx_ref, wu_ref, wg_ref, lng_ref, lnb_ref, ws_ref, bs_ref, og_ref, o_ref, sg_scr):
    tm = PROJ_TM
    h = _rmsnorm(x_ref[...], gx_ref[...]).astype(BF16)
    u = _gelu_tanh(_dot(h, wu_ref[...]))
    g = _gelu_tanh(_dot(h, wg_ref[...]))
    r = lax.broadcasted_iota(jnp.int32, (SGU_CHUNK, SGU_CHUNK), 0)
    c = lax.broadcasted_iota(jnp.int32, (SGU_CHUNK, SGU_CHUNK), 1)
    for gi in range(SGU_GROUPS):
        sl = slice(gi * SGU_GROUP_DIM, (gi + 1) * SGU_GROUP_DIM)
        gg = g[:, sl]
        d = gg - jnp.mean(gg, axis=-1, keepdims=True)
        var = jnp.mean(d * d, axis=-1, keepdims=True)
        gn = (d * lax.rsqrt(var + EPS) * lng_ref[:, sl] + lnb_ref[:, sl]).astype(BF16)
        w_causal = jnp.where(r >= c, ws_ref[gi], 0.0).astype(BF16)
        b_col = bs_ref[:, gi:gi + 1]
        for ci in range(tm // SGU_CHUNK):
            rows = slice(ci * SGU_CHUNK, (ci + 1) * SGU_CHUNK)
            mixed = _dot(w_causal, gn[rows]) + b_col
            sg_scr[rows, sl] = u[rows, sl] * mixed
    o_ref[...] = _rmsnorm(sg_scr[...], og_ref[...]).astype(BF16)


def _sgu(x, x_gain, w_in, ln_gain, ln_bias, w_spatial, b_spatial_t, out_gain):
    tokens = x.shape[0]
    tm = PROJ_TM
    w_block = lambda c: pl.BlockSpec((D_MODEL, D_SGU), lambda i: (0, c), pipeline_mode=pl.Buffered(1))
    first = 3 * D_ATTN // D_SGU
    return pl.pallas_call(
        _sgu_body,
        grid=(tokens // tm,),
        in_specs=[pl.BlockSpec((tm, D_MODEL), lambda i: (i, 0)), _resident((1, D_MODEL)),
                  w_block(first), w_block(first + 1),
                  _resident((1, D_SGU)), _resident((1, D_SGU)),
                  _resident((SGU_GROUPS, SGU_CHUNK, SGU_CHUNK)),
                  _resident((SGU_CHUNK, SGU_GROUPS)),
                  _resident((1, D_SGU))],
        out_specs=pl.BlockSpec((tm, D_SGU), lambda i: (i, 0)),
        out_shape=jax.ShapeDtypeStruct((tokens, D_SGU), BF16),
        scratch_shapes=[pltpu.VMEM((tm, D_SGU), F32)],
        compiler_params=_params("parallel"),
        name="sgu",
    )(x, x_gain, w_in, w_in, ln_gain, ln_bias, w_spatial, b_spatial_t, out_gain)


def _attn_scores(t, q_ref, kt_ref, s_ref, m_ref):
    tq, kc = ATTN_TQ, ATTN_KV_CHUNK
    m_part = None
    for c in range(t + 1):
        s = _dot(q_ref[...], kt_ref[:, c * kc:(c + 1) * kc])
        if c == t:
            r = lax.broadcasted_iota(jnp.int32, (tq, kc), 0)
            col = lax.broadcasted_iota(jnp.int32, (tq, kc), 1)
            s = jnp.where(col <= r, s, NEG_INF)
        s_ref[:, c * kc:(c + 1) * kc] = s
        for j in range(kc // LANES):
            sj = s[:, j * LANES:(j + 1) * LANES]
            m_part = sj if m_part is None else jnp.maximum(m_part, sj)
    m_ref[...] = jnp.broadcast_to(jnp.max(m_part, axis=-1, keepdims=True), (tq, LANES))


def _attn_values(t, va_ref, o_ref, s_ref, m_ref):
    tq, kc = ATTN_TQ, ATTN_KV_CHUNK
    m = m_ref[...]
    acc = jnp.zeros((tq, AUG_DIM), F32)
    for c in range(t + 1):
        p = jnp.concatenate(
            [jnp.exp(s_ref[:, c * kc + j * LANES:c * kc + (j + 1) * LANES] - m).astype(BF16)
             for j in range(kc // LANES)], axis=1)
        acc = acc + _dot(p, va_ref[c * kc:(c + 1) * kc, :])
    o_ref[...] = acc[:, 0:HEAD_DIM] / acc[:, HEAD_DIM:AUG_DIM]


def _attn_body(q_ref, k_ref, v_ref, o_ref, s_scr, m_scr, va_scr, kt_scr):
    t = pl.program_id(2)
    n_tiles = k_ref.shape[0] // ATTN_TQ

    def step(ts):
        if ts == 0:
            o_ref[...] = jnp.zeros_like(o_ref)
            va_scr[:, 0:HEAD_DIM] = v_ref[...]
            va_scr[:, HEAD_DIM:AUG_DIM] = jnp.ones((va_scr.shape[0], HEAD_DIM), BF16)
            for c in range(n_tiles):
                rows = slice(c * ATTN_KV_CHUNK, (c + 1) * ATTN_KV_CHUNK)
                kt_scr[:, rows] = k_ref[rows, :].T
        else:
            _attn_values(ts - 1, va_scr, o_ref, s_scr.at[(ts - 1) % 2], m_scr.at[(ts - 1) % 2])
        if ts < n_tiles:
            _attn_scores(ts, q_ref, kt_scr, s_scr.at[ts % 2], m_scr.at[ts % 2])

    for ts in range(n_tiles + 1):
        pl.when(t == ts)(functools.partial(step, ts))


def _attention(qa, ka, v, batch, seq):
    tokens = batch * seq
    q_tiles = seq // ATTN_TQ
    return pl.pallas_call(
        _attn_body,
        grid=(batch, ATTN_HEADS, q_tiles + 1),
        in_specs=[pl.BlockSpec((ATTN_TQ, AUG_DIM),
                               lambda b, h, i: (b * q_tiles + jnp.minimum(i, q_tiles - 1), h)),
                  pl.BlockSpec((seq, AUG_DIM), lambda b, h, i: (b, h)),
                  pl.BlockSpec((seq, HEAD_DIM), lambda b, h, i: (b, h))],
        out_specs=pl.BlockSpec((ATTN_TQ, HEAD_DIM),
                               lambda b, h, i: (b * q_tiles + jnp.maximum(i - 1, 0), h)),
        out_shape=jax.ShapeDtypeStruct((tokens, D_ATTN), F32),
        scratch_shapes=[pltpu.VMEM((2, ATTN_TQ, seq), F32), pltpu.VMEM((2, ATTN_TQ, LANES), F32),
                        pltpu.VMEM((seq, AUG_DIM), BF16), pltpu.VMEM((AUG_DIM, seq), BF16)],
        compiler_params=_params("parallel", "parallel", "arbitrary"),
        name="moba_attn",
    )(qa, ka, v)


def _outproj_body(a_ref, s_ref, x_ref, ga_ref, w_ref, o_ref):
    an = _rmsnorm(a_ref[...], ga_ref[...]).astype(BF16)
    o_ref[...] = (x_ref[...] + _dot(an, w_ref[0:D_ATTN, :])
                  + _dot(s_ref[...], w_ref[D_ATTN:D_ATTN + D_SGU, :]))


def _outproj(attn, sgu, x, attn_gain, w_out):
    tokens = x.shape[0]
    tm = PROJ_TM
    row = lambda width: pl.BlockSpec((tm, width), lambda i: (i, 0))
    return pl.pallas_call(
        _outproj_body,
        grid=(tokens // tm,),
        in_specs=[row(D_ATTN), row(D_SGU), row(D_MODEL), _resident((1, D_ATTN)),
                  _resident((D_ATTN + D_SGU, D_MODEL))],
        out_specs=row(D_MODEL),
        out_shape=jax.ShapeDtypeStruct((tokens, D_MODEL), F32),
        compiler_params=_params("parallel"),
        name="outproj",
    )(attn, sgu, x, attn_gain, w_out)


def kernel(x, ffn1_norm, ffn1_w_gate, ffn1_w_up, ffn1_w_down, mix_norm, w_in, q_norm, k_norm, sgu_ln_gain, sgu_ln_bias, sgu_w_spatial, sgu_b_spatial, attn_out_gain, sgu_out_gain, w_out, ffn2_norm, ffn2_w_gate, ffn2_w_up, ffn2_w_down):
    batch, seq, d_model = x.shape
    depth = ffn1_norm.shape[0]
    assert d_model == D_MODEL and seq % PROJ_TM == 0 and (batch * seq) % FFN_TM == 0
    xt = x.reshape(batch * seq, d_model)
    vec = lambda a: a.reshape(1, -1)
    for l in range(depth):
        xt, w_in_bf, w_out_bf = _ffn(xt, vec(ffn1_norm[l]), ffn1_w_gate[l], ffn1_w_up[l], ffn1_w_down[l],
                                     side=(w_in[l], w_out[l]))
        qa, ka, v = _qkv(xt, vec(mix_norm[l]), w_in_bf, vec(q_norm[l]), vec(k_norm[l]), seq)
        sgu = _sgu(xt, vec(mix_norm[l]), w_in_bf, vec(sgu_ln_gain[l]), vec(sgu_ln_bias[l]),
                   sgu_w_spatial[l], sgu_b_spatial[l].T, vec(sgu_out_gain[l]))
        attn = _attention(qa, ka, v, batch, seq)
        xt = _outproj(attn, sgu, xt, vec(attn_out_gain[l]), w_out_bf)
        xt = _ffn(xt, vec(ffn2_norm[l]), ffn2_w_gate[l], ffn2_w_up[l], ffn2_w_down[l])
    return xt.reshape(batch, seq, d_model)
```

```python
import functools

import jax
import jax.numpy as jnp
from jax import lax
from jax.experimental import pallas as pl
from jax.experimental.pallas import tpu as pltpu

D_MODEL = 2048
D_FF = 5632
ATTN_HEADS = 8
HEAD_DIM = 128
D_ATTN = ATTN_HEADS * HEAD_DIM
SGU_GROUPS = 8
SGU_GROUP_DIM = 128
D_SGU = SGU_GROUPS * SGU_GROUP_DIM
MOBA_BLOCK = 256
MOBA_TOPK = 3
SGU_CHUNK = 128
EPS = 1e-6
NEG_INF = -1e30

LANES = 128
AUG_DIM = 2 * HEAD_DIM
VMEM_LIMIT_BYTES = 60 * 1024 * 1024

FFN_TM = 1024
FFN_HEAD_TF = 256
FFN_TF = 512
PROJ_TM = 512
ATTN_TQ = 512
ATTN_KV_CHUNK = 512

F32 = jnp.float32
BF16 = jnp.bfloat16


def _rmsnorm(x, gain):
    return x * lax.rsqrt(jnp.mean(x * x, axis=-1, keepdims=True) + EPS) * gain


def _gelu_tanh(x):
    cdf = 0.5 * (1.0 + jnp.tanh(0.7978845608028654 * (x + 0.044715 * (x * x * x))))
    return x * cdf


def _dot(a, b):
    return jnp.dot(a, b, preferred_element_type=F32)


def _dot_nt(a, b, precision=None):
    return lax.dot_general(a, b, (((1,), (1,)), ((), ())), precision=precision,
                           preferred_element_type=F32)


def _params(*semantics):
    return pltpu.CompilerParams(dimension_semantics=semantics, vmem_limit_bytes=VMEM_LIMIT_BYTES)


def _resident(shape):
    return pl.BlockSpec(shape, lambda *_: (0,) * len(shape), pipeline_mode=pl.Buffered(1))


def _ffn_start(x_ref, g_ref, o_ref, h_scr):
    x = x_ref[...]
    h_scr[...] = _rmsnorm(x, g_ref[...]).astype(BF16)
    o_ref[...] = x


def _ffn_accumulate(h_scr, wg, wu, wd, o_ref):
    h = h_scr[...]
    gate = _dot(h, wg)
    up = _dot(h, wu)
    act = (gate / (1.0 + jnp.exp(-gate))) * up * 0.5
    o_ref[...] += _dot(act.astype(BF16), wd)


def _ffn_head_body(x_ref, g_ref, wg_ref, wu_ref, wd_ref, o_ref, wg_bf_ref, wu_bf_ref, wd_bf_ref, h_scr):
    @pl.when(pl.program_id(0) == 0)
    def _():
        _ffn_start(x_ref, g_ref, o_ref, h_scr)

    wg_bf_ref[...] = wg_ref[...].astype(BF16)
    wu_bf_ref[...] = wu_ref[...].astype(BF16)
    wd_bf_ref[...] = wd_ref[...].astype(BF16)
    _ffn_accumulate(h_scr, wg_bf_ref[...], wu_bf_ref[...], wd_bf_ref[...], o_ref)


def _ffn_tail_body(x_ref, g_ref, wg_ref, wu_ref, wd_ref, head_hbm, *rest, n_side):
    side_in, (o_ref, *side_out), h_scr = rest[:n_side], rest[n_side:-1], rest[-1]
    i = pl.program_id(0)
    j = pl.program_id(1)

    for src, dst in zip(side_in, side_out):
        dst[...] = src[...].astype(BF16)

    @pl.when((i == 0) & (j == 0))
    def _():
        pltpu.sync_copy(head_hbm, o_ref)

    @pl.when((i > 0) & (j == 0))
    def _():
        _ffn_start(x_ref, g_ref, o_ref, h_scr)

    @pl.when(i > 0)
    def _():
        _ffn_accumulate(h_scr, wg_ref[...], wu_ref[...], wd_ref[...], o_ref)


def _ffn(x, gain, w_gate, w_up, w_down, side=()):
    tokens = x.shape[0]
    n_tiles = tokens // FFN_TM
    vec = pl.BlockSpec((1, D_MODEL), lambda *_: (0, 0))

    tf = FFN_HEAD_TF
    col = pl.BlockSpec((D_MODEL, tf), lambda j: (0, j))
    rowblk = pl.BlockSpec((tf, D_MODEL), lambda j: (j, 0))
    tile0 = pl.BlockSpec((FFN_TM, D_MODEL), lambda j: (0, 0))
    head, wg_bf, wu_bf, wd_bf = pl.pallas_call(
        _ffn_head_body,
        grid=(D_FF // tf,),
        in_specs=[tile0, vec, col, col, rowblk],
        out_specs=(tile0, col, col, rowblk),
        out_shape=(jax.ShapeDtypeStruct((FFN_TM, D_MODEL), F32),
                   jax.ShapeDtypeStruct((D_MODEL, D_FF), BF16),
                   jax.ShapeDtypeStruct((D_MODEL, D_FF), BF16),
                   jax.ShapeDtypeStruct((D_FF, D_MODEL), BF16)),
        scratch_shapes=[pltpu.VMEM((FFN_TM, D_MODEL), BF16)],
        compiler_params=_params("arbitrary"),
        name="ffn_head",
    )(x, gain, w_gate, w_up, w_down)

    tf = FFN_TF
    wcol = lambda i, j: (0, jnp.where(i == 0, 0, j))
    wrow = lambda i, j: (jnp.where(i == 0, 0, j), 0)
    row = pl.BlockSpec((FFN_TM, D_MODEL), lambda i, j: (i, 0))
    n_steps = D_FF // tf
    side_specs, start = [], 0
    for w in side:
        n_side_tiles = w.shape[1] // LANES
        side_specs.append(pl.BlockSpec(
            (w.shape[0], LANES),
            lambda i, j, start=start, last=n_side_tiles - 1: (0, jnp.clip(i * n_steps + j - start, 0, last))))
        start += n_side_tiles
    assert start <= n_tiles * n_steps
    outs = pl.pallas_call(
        functools.partial(_ffn_tail_body, n_side=len(side)),
        grid=(n_tiles, n_steps),
        in_specs=[row, vec, pl.BlockSpec((D_MODEL, tf), wcol), pl.BlockSpec((D_MODEL, tf), wcol),
                  pl.BlockSpec((tf, D_MODEL), wrow), pl.BlockSpec(memory_space=pl.ANY), *side_specs],
        out_specs=(row, *side_specs),
        out_shape=(jax.ShapeDtypeStruct((tokens, D_MODEL), F32),
                   *(jax.ShapeDtypeStruct(w.shape, BF16) for w in side)),
        scratch_shapes=[pltpu.VMEM((FFN_TM, D_MODEL), BF16)],
        compiler_params=_params("arbitrary", "arbitrary"),
        name="ffn_tail",
    )(x, gain, wg_bf, wu_bf, wd_bf, head, *side)
    return outs if side else outs[0]


def _qkv_body(x_ref, gx_ref, w_ref, gq_ref, gk_ref, qa_ref, ka_ref, v_ref, km_scr, *, tiles_per_batch):
    tm = PROJ_TM
    blocks_per_batch = tiles_per_batch * (tm // MOBA_BLOCK)
    i = pl.program_id(0)
    n0 = (i % tiles_per_batch) * (tm // MOBA_BLOCK)

    @pl.when(i == 0)
    def _init():
        km_scr[...] = jnp.zeros_like(km_scr)

    h = _rmsnorm(x_ref[...], gx_ref[...]).astype(BF16)
    q = _dot(h, w_ref[:, 0:D_ATTN])
    k = _dot(h, w_ref[:, D_ATTN:2 * D_ATTN])
    v_ref[...] = _dot(h, w_ref[:, 2 * D_ATTN:3 * D_ATTN]).astype(BF16)

    scale = HEAD_DIM ** -0.5
    blk_rows = lax.broadcasted_iota(jnp.int32, (blocks_per_batch, HEAD_DIM), 0)
    n_iota = lax.broadcasted_iota(jnp.int32, (blocks_per_batch, tm), 0)
    lane_tok = lax.broadcasted_iota(jnp.int32, (blocks_per_batch, tm), 1)
    q_blk = n0 + lane_tok // MOBA_BLOCK
    valid = n_iota < q_blk
    tok_blk = n0 + lax.broadcasted_iota(jnp.int32, (tm, LANES), 0) // MOBA_BLOCK
    onehot = jnp.where(lax.broadcasted_iota(jnp.int32, (tm, LANES), 1) == tok_blk, 1.0, 0.0).astype(BF16)
    pad_rows = jnp.zeros((LANES - blocks_per_batch, tm), F32)

    for hh in range(ATTN_HEADS):
        sl = slice(hh * HEAD_DIM, (hh + 1) * HEAD_DIM)
        qn = _rmsnorm(q[:, sl], gq_ref[...])
        kn = _rmsnorm(k[:, sl], gk_ref[...])

        kmean = km_scr[:, sl]
        for b in range(tm // MOBA_BLOCK):
            km = jnp.mean(kn[b * MOBA_BLOCK:(b + 1) * MOBA_BLOCK], axis=0, keepdims=True)
            kmean = jnp.where(blk_rows == n0 + b, km, kmean)
        km_scr[:, sl] = kmean

        gate = _dot_nt(kmean, qn, precision=lax.Precision.HIGHEST)
        cnt = jnp.zeros((blocks_per_batch, tm), jnp.int32)
        for m in range(blocks_per_batch - 1):
            gm = gate[m:m + 1, :]
            beats = (gm > gate) | ((gm == gate) & (m < n_iota))
            cnt = cnt + jnp.where(beats & (m < q_blk), 1, 0)
        sel = ((cnt < MOBA_TOPK) & valid) | (n_iota == q_blk)
        bias_t = jnp.concatenate([jnp.where(sel, 0.0, NEG_INF), pad_rows], axis=0)
        bias = bias_t.T

        a0 = hh * AUG_DIM
        qa_ref[:, a0:a0 + HEAD_DIM] = (qn * scale).astype(BF16)
        qa_ref[:, a0 + HEAD_DIM:a0 + AUG_DIM] = bias.astype(BF16)
        ka_ref[:, a0:a0 + HEAD_DIM] = kn.astype(BF16)
        ka_ref[:, a0 + HEAD_DIM:a0 + AUG_DIM] = onehot


def _qkv(x, x_gain, w_in, q_gain, k_gain, seq):
    tokens = x.shape[0]
    tm = PROJ_TM
    tiles_per_batch = seq // tm
    blocks_per_batch = seq // MOBA_BLOCK
    assert blocks_per_batch <= LANES and blocks_per_batch % 8 == 0
    row = lambda width: pl.BlockSpec((tm, width), lambda i: (i, 0))
    return pl.pallas_call(
        functools.partial(_qkv_body, tiles_per_batch=tiles_per_batch),
        grid=(tokens // tm,),
        in_specs=[row(D_MODEL), _resident((1, D_MODEL)), _resident((D_MODEL, 3 * D_ATTN)),
                  _resident((1, HEAD_DIM)), _resident((1, HEAD_DIM))],
        out_specs=(row(ATTN_HEADS * AUG_DIM), row(ATTN_HEADS * AUG_DIM), row(D_ATTN)),
        out_shape=(jax.ShapeDtypeStruct((tokens, ATTN_HEADS * AUG_DIM), BF16),
                   jax.ShapeDtypeStruct((tokens, ATTN_HEADS * AUG_DIM), BF16),
                   jax.ShapeDtypeStruct((tokens, D_ATTN), BF16)),
        scratch_shapes=[pltpu.VMEM((blocks_per_batch, D_ATTN), F32)],
        compiler_params=_params("arbitrary"),
        name="qkv_gate",
    )(x, x_gain, w_in, q_gain, k_gain)


def _sgu_body(x_ref, gx_ref, wu_ref, wg_ref, lng_ref, lnb_ref, ws_ref, bs_ref, og_ref, o_ref, sg_scr):
    tm = PROJ_TM
    h = _rmsnorm(x_ref[...], gx_ref[...]).astype(BF16)
    u = _gelu_tanh(_dot(h, wu_ref[...]))
    g = _gelu_tanh(_dot(h, wg_ref[...]))
    r = lax.broadcasted_iota(jnp.int32, (SGU_CHUNK, SGU_CHUNK), 0)
    c = lax.broadcasted_iota(jnp.int32, (SGU_CHUNK, SGU_CHUNK), 1)
    for gi in range(SGU_GROUPS):
        sl = slice(gi * SGU_GROUP_DIM, (gi + 1) * SGU_GROUP_DIM)
        gg = g[:, sl]
        d = gg - jnp.mean(gg, axis=-1, keepdims=True)
        var = jnp.mean(d * d, axis=-1, keepdims=True)
        gn = (d * lax.rsqrt(var + EPS) * lng_ref[:, sl] + lnb_ref[:, sl]).astype(BF16)
        w_causal = jnp.where(r >= c, ws_ref[gi], 0.0).astype(BF16)
        b_col = bs_ref[:, gi:gi + 1]
        for ci in range(tm // SGU_CHUNK):
            rows = slice(ci * SGU_CHUNK, (ci + 1) * SGU_CHUNK)
            mixed = _dot(w_causal, gn[rows]) + b_col
            sg_scr[rows, sl] = u[rows, sl] * mixed
    o_ref[...] = _rmsnorm(sg_scr[...], og_ref[...]).astype(BF16)


def _sgu(x, x_gain, w_in, ln_gain, ln_bias, w_spatial, b_spatial_t, out_gain):
    tokens = x.shape[0]
    tm = PROJ_TM
    w_block = lambda c: pl.BlockSpec((D_MODEL, D_SGU), lambda i: (0, c), pipeline_mode=pl.Buffered(1))
    first = 3 * D_ATTN // D_SGU
    return pl.pallas_call(
        _sgu_body,
        grid=(tokens // tm,),
        in_specs=[pl.BlockSpec((tm, D_MODEL), lambda i: (i, 0)), _resident((1, D_MODEL)),
                  w_block(first), w_block(first + 1),
                  _resident((1, D_SGU)), _resident((1, D_SGU)),
                  _resident((SGU_GROUPS, SGU_CHUNK, SGU_CHUNK)),
                  _resident((SGU_CHUNK, SGU_GROUPS)),
                  _resident((1, D_SGU))],
        out_specs=pl.BlockSpec((tm, D_SGU), lambda i: (i, 0)),
        out_shape=jax.ShapeDtypeStruct((tokens, D_SGU), BF16),
        scratch_shapes=[pltpu.VMEM((tm, D_SGU), F32)],
        compiler_params=_params("parallel"),
        name="sgu",
    )(x, x_gain, w_in, w_in, ln_gain, ln_bias, w_spatial, b_spatial_t, out_gain)


def _attn_scores(t, q_ref, k_ref, s_ref, m_ref):
    tq, kc = ATTN_TQ, ATTN_KV_CHUNK
    m_part = None
    for c in range(t + 1):
        s = _dot_nt(q_ref[...], k_ref[c * kc:(c + 1) * kc, :])
        if c == t:
            r = lax.broadcasted_iota(jnp.int32, (tq, kc), 0)
            col = lax.broadcasted_iota(jnp.int32, (tq, kc), 1)
            s = jnp.where(col <= r, s, NEG_INF)
        s_ref[:, c * kc:(c + 1) * kc] = s
        for j in range(kc // LANES):
            sj = s[:, j * LANES:(j + 1) * LANES]
            m_part = sj if m_part is None else jnp.maximum(m_part, sj)
    m_ref[...] = jnp.broadcast_to(jnp.max(m_part, axis=-1, keepdims=True), (tq, LANES))


def _attn_values(t, va_ref, o_ref, s_ref, m_ref):
    tq, kc = ATTN_TQ, ATTN_KV_CHUNK
    m = m_ref[...]
    acc = jnp.zeros((tq, AUG_DIM), F32)
    for c in range(t + 1):
        p = jnp.concatenate(
            [jnp.exp(s_ref[:, c * kc + j * LANES:c * kc + (j + 1) * LANES] - m).astype(BF16)
             for j in range(kc // LANES)], axis=1)
        acc = acc + _dot(p, va_ref[c * kc:(c + 1) * kc, :])
    o_ref[...] = acc[:, 0:HEAD_DIM] / acc[:, HEAD_DIM:AUG_DIM]


def _attn_body(q_ref, k_ref, v_ref, o_ref, s_scr, m_scr, va_scr):
    n_tiles = k_ref.shape[0] // ATTN_TQ
    tile = lambda t: slice(t * ATTN_TQ, (t + 1) * ATTN_TQ)
    va_scr[:, 0:HEAD_DIM] = v_ref[...]
    va_scr[:, HEAD_DIM:AUG_DIM] = jnp.ones((va_scr.shape[0], HEAD_DIM), BF16)

    def sub_step(ts):
        if ts >= 1:
            _attn_values(ts - 1, va_scr, o_ref.at[tile(ts - 1), :], s_scr.at[(ts - 1) % 2], m_scr.at[(ts - 1) % 2])
        if ts < n_tiles:
            _attn_scores(ts, q_ref.at[tile(ts), :], k_ref, s_scr.at[ts % 2], m_scr.at[ts % 2])

    for ts in range(n_tiles + 1):
        pl.when(pl.program_id(1) + ts >= 0)(functools.partial(sub_step, ts))


def _attention(qa, ka, v, batch, seq):
    tokens = batch * seq
    return pl.pallas_call(
        _attn_body,
        grid=(batch, ATTN_HEADS),
        in_specs=[pl.BlockSpec((seq, AUG_DIM), lambda b, h: (b, h)),
                  pl.BlockSpec((seq, AUG_DIM), lambda b, h: (b, h)),
                  pl.BlockSpec((seq, HEAD_DIM), lambda b, h: (b, h))],
        out_specs=pl.BlockSpec((seq, HEAD_DIM), lambda b, h: (b, h)),
        out_shape=jax.ShapeDtypeStruct((tokens, D_ATTN), F32),
        scratch_shapes=[pltpu.VMEM((2, ATTN_TQ, seq), F32), pltpu.VMEM((2, ATTN_TQ, LANES), F32),
                        pltpu.VMEM((seq, AUG_DIM), BF16)],
        compiler_params=_params("parallel", "parallel"),
        name="moba_attn",
    )(qa, ka, v)


def _outproj_body(a_ref, s_ref, x_ref, ga_ref, w_ref, o_ref):
    an = _rmsnorm(a_ref[...], ga_ref[...]).astype(BF16)
    o_ref[...] = (x_ref[...] + _dot(an, w_ref[0:D_ATTN, :])
                  + _dot(s_ref[...], w_ref[D_ATTN:D_ATTN + D_SGU, :]))


def _outproj(attn, sgu, x, attn_gain, w_out):
    tokens = x.shape[0]
    tm = PROJ_TM
    row = lambda width: pl.BlockSpec((tm, width), lambda i: (i, 0))
    return pl.pallas_call(
        _outproj_body,
        grid=(tokens // tm,),
        in_specs=[row(D_ATTN), row(D_SGU), row(D_MODEL), _resident((1, D_ATTN)),
                  _resident((D_ATTN + D_SGU, D_MODEL))],
        out_specs=row(D_MODEL),
        out_shape=jax.ShapeDtypeStruct((tokens, D_MODEL), F32),
        compiler_params=_params("parallel"),
        name="outproj",
    )(attn, sgu, x, attn_gain, w_out)


def kernel(x, ffn1_norm, ffn1_w_gate, ffn1_w_up, ffn1_w_down, mix_norm, w_in, q_norm, k_norm, sgu_ln_gain, sgu_ln_bias, sgu_w_spatial, sgu_b_spatial, attn_out_gain, sgu_out_gain, w_out, ffn2_norm, ffn2_w_gate, ffn2_w_up, ffn2_w_down):
    batch, seq, d_model = x.shape
    depth = ffn1_norm.shape[0]
    assert d_model == D_MODEL and seq % PROJ_TM == 0 and (batch * seq) % FFN_TM == 0
    xt = x.reshape(batch * seq, d_model)
    vec = lambda a: a.reshape(1, -1)
    for l in range(depth):
        xt, w_in_bf, w_out_bf = _ffn(xt, vec(ffn1_norm[l]), ffn1_w_gate[l], ffn1_w_up[l], ffn1_w_down[l],
                                     side=(w_in[l], w_out[l]))
        qa, ka, v = _qkv(xt, vec(mix_norm[l]), w_in_bf, vec(q_norm[l]), vec(k_norm[l]), seq)
        sgu = _sgu(xt, vec(mix_norm[l]), w_in_bf, vec(sgu_ln_gain[l]), vec(sgu_ln_bias[l]),
                   sgu_w_spatial[l], sgu_b_spatial[l].T, vec(sgu_out_gain[l]))
        attn = _attention(qa, ka, v, batch, seq)
        xt = _outproj(attn, sgu, xt, vec(attn_out_gain[l]), w_out_bf)
        xt = _ffn(xt, vec(ffn2_norm[l]), ffn2_w_gate[l], ffn2_w_up[l], ffn2_w_down[l])
    return xt.reshape(batch, seq, d_model)
```

```python
import functools

import jax
import jax.numpy as jnp
from jax import lax
from jax.experimental import pallas as pl
from jax.experimental.pallas import tpu as pltpu

D_MODEL = 2048
D_FF = 5632
ATTN_HEADS = 8
HEAD_DIM = 128
D_ATTN = ATTN_HEADS * HEAD_DIM
SGU_GROUPS = 8
SGU_GROUP_DIM = 128
D_SGU = SGU_GROUPS * SGU_GROUP_DIM
MOBA_BLOCK = 256
MOBA_TOPK = 3
SGU_CHUNK = 128
EPS = 1e-6
NEG_INF = -1e30

LANES = 128
AUG_DIM = 2 * HEAD_DIM
VMEM_LIMIT_BYTES = 60 * 1024 * 1024

FFN_TM = 1024
FFN_HEAD_TF = 256
FFN_TF = 512
PROJ_TM = 512
ATTN_TQ = 512
ATTN_KV_CHUNK = 512

F32 = jnp.float32
BF16 = jnp.bfloat16


def _rmsnorm(x, gain):
    return x * lax.rsqrt(jnp.mean(x * x, axis=-1, keepdims=True) + EPS) * gain


def _gelu_tanh(x):
    cdf = 0.5 * (1.0 + jnp.tanh(0.7978845608028654 * (x + 0.044715 * (x * x * x))))
    return x * cdf


def _dot(a, b):
    return jnp.dot(a, b, preferred_element_type=F32)


def _dot_nt(a, b, precision=None):
    return lax.dot_general(a, b, (((1,), (1,)), ((), ())), precision=precision,
                           preferred_element_type=F32)


def _params(*semantics):
    return pltpu.CompilerParams(dimension_semantics=semantics, vmem_limit_bytes=VMEM_LIMIT_BYTES)


def _resident(shape):
    return pl.BlockSpec(shape, lambda *_: (0,) * len(shape), pipeline_mode=pl.Buffered(1))


def _ffn_accumulate(h_scr, wg, wu, wd, base_ref, o_ref):
    h = h_scr[...]
    gate = _dot(h, wg)
    up = _dot(h, wu)
    act = (gate / (1.0 + jnp.exp(-gate))) * up * 0.5
    o_ref[...] = base_ref[...] + _dot(act.astype(BF16), wd)


def _ffn_step(first, x_ref, g_ref, h_scr, wg, wu, wd, o_ref):
    if first:
        h_scr[...] = _rmsnorm(x_ref[...], g_ref[...]).astype(BF16)
    _ffn_accumulate(h_scr, wg, wu, wd, x_ref if first else o_ref, o_ref)


def _ffn_head_body(x_ref, g_ref, wg_ref, wu_ref, wd_ref, o_ref, wg_bf_ref, wu_bf_ref, wd_bf_ref, h_scr):
    j = pl.program_id(0)
    for first in (True, False):
        @pl.when((j == 0) if first else (j > 0))
        def _(first=first):
            wg_bf_ref[...] = wg_ref[...].astype(BF16)
            wu_bf_ref[...] = wu_ref[...].astype(BF16)
            wd_bf_ref[...] = wd_ref[...].astype(BF16)
            _ffn_step(first, x_ref, g_ref, h_scr, wg_bf_ref[...], wu_bf_ref[...], wd_bf_ref[...], o_ref)


def _ffn_tail_body(x_ref, g_ref, wg_ref, wu_ref, wd_ref, head_hbm, *rest, side_windows):
    n_side = len(side_windows)
    side_in, (o_ref, *side_out), h_scr = rest[:n_side], rest[n_side:-1], rest[-1]
    i = pl.program_id(0)
    j = pl.program_id(1)

    step = i * pl.num_programs(1) + j
    for src, dst, (start, n) in zip(side_in, side_out, side_windows):
        @pl.when((step >= start) & (step < start + n))
        def _(src=src, dst=dst):
            dst[...] = src[...].astype(BF16)

    @pl.when((i == 0) & (j == 0))
    def _():
        pltpu.sync_copy(head_hbm, o_ref)

    for first in (True, False):
        @pl.when((i > 0) & ((j == 0) if first else (j > 0)))
        def _(first=first):
            _ffn_step(first, x_ref, g_ref, h_scr, wg_ref[...], wu_ref[...], wd_ref[...], o_ref)


def _ffn(x, gain, w_gate, w_up, w_down, side=()):
    tokens = x.shape[0]
    n_tiles = tokens // FFN_TM
    vec = pl.BlockSpec((1, D_MODEL), lambda *_: (0, 0))

    tf = FFN_HEAD_TF
    col = pl.BlockSpec((D_MODEL, tf), lambda j: (0, j))
    rowblk = pl.BlockSpec((tf, D_MODEL), lambda j: (j, 0))
    tile0 = pl.BlockSpec((FFN_TM, D_MODEL), lambda j: (0, 0), pipeline_mode=pl.Buffered(1))
    head, wg_bf, wu_bf, wd_bf = pl.pallas_call(
        _ffn_head_body,
        grid=(D_FF // tf,),
        in_specs=[tile0, vec, col, col, rowblk],
        out_specs=(tile0, col, col, rowblk),
        out_shape=(jax.ShapeDtypeStruct((FFN_TM, D_MODEL), F32),
                   jax.ShapeDtypeStruct((D_MODEL, D_FF), BF16),
                   jax.ShapeDtypeStruct((D_MODEL, D_FF), BF16),
                   jax.ShapeDtypeStruct((D_FF, D_MODEL), BF16)),
        scratch_shapes=[pltpu.VMEM((FFN_TM, D_MODEL), BF16)],
        compiler_params=_params("arbitrary"),
        name="ffn_head",
    )(x, gain, w_gate, w_up, w_down)

    tf = FFN_TF
    wcol = lambda i, j: (0, jnp.where(i == 0, 0, j))
    wrow = lambda i, j: (jnp.where(i == 0, 0, j), 0)
    row = pl.BlockSpec((FFN_TM, D_MODEL), lambda i, j: (i, 0))
    n_steps = D_FF // tf
    side_specs, side_windows, start = [], [], 0
    for w in side:
        n_side_tiles = w.shape[1] // LANES
        side_specs.append(pl.BlockSpec(
            (w.shape[0], LANES),
            lambda i, j, start=start, last=n_side_tiles - 1: (0, jnp.clip(i * n_steps + j - start, 0, last))))
        side_windows.append((start, n_side_tiles))
        start += n_side_tiles
    assert start <= n_tiles * n_steps
    outs = pl.pallas_call(
        functools.partial(_ffn_tail_body, side_windows=tuple(side_windows)),
        grid=(n_tiles, n_steps),
        in_specs=[row, vec, pl.BlockSpec((D_MODEL, tf), wcol), pl.BlockSpec((D_MODEL, tf), wcol),
                  pl.BlockSpec((tf, D_MODEL), wrow), pl.BlockSpec(memory_space=pl.ANY), *side_specs],
        out_specs=(row, *side_specs),
        out_shape=(jax.ShapeDtypeStruct((tokens, D_MODEL), F32),
                   *(jax.ShapeDtypeStruct(w.shape, BF16) for w in side)),
        scratch_shapes=[pltpu.VMEM((FFN_TM, D_MODEL), BF16)],
        compiler_params=_params("arbitrary", "arbitrary"),
        name="ffn_tail",
    )(x, gain, wg_bf, wu_bf, wd_bf, head, *side)
    return outs if side else outs[0]


def _qkv_body(x_ref, gx_ref, w_ref, gq_ref, gk_ref, qa_ref, ka_ref, v_ref, km_scr, *, tiles_per_batch):
    tm = PROJ_TM
    blocks_per_batch = tiles_per_batch * (tm // MOBA_BLOCK)
    i = pl.program_id(0)
    n0 = (i % tiles_per_batch) * (tm // MOBA_BLOCK)

    @pl.when(i == 0)
    def _init():
        km_scr[...] = jnp.zeros_like(km_scr)

    h = _rmsnorm(x_ref[...], gx_ref[...]).astype(BF16)
    q = _dot(h, w_ref[:, 0:D_ATTN])
    k = _dot(h, w_ref[:, D_ATTN:2 * D_ATTN])
    v_ref[...] = _dot(h, w_ref[:, 2 * D_ATTN:3 * D_ATTN]).astype(BF16)

    scale = HEAD_DIM ** -0.5
    blk_rows = lax.broadcasted_iota(jnp.int32, (blocks_per_batch, HEAD_DIM), 0)
    n_iota = lax.broadcasted_iota(jnp.int32, (blocks_per_batch, tm), 0)
    lane_tok = lax.broadcasted_iota(jnp.int32, (blocks_per_batch, tm), 1)
    q_blk = n0 + lane_tok // MOBA_BLOCK
    valid = n_iota < q_blk
    tok_blk = n0 + lax.broadcasted_iota(jnp.int32, (tm, LANES), 0) // MOBA_BLOCK
    onehot = jnp.where(lax.broadcasted_iota(jnp.int32, (tm, LANES), 1) == tok_blk, 1.0, 0.0).astype(BF16)
    pad_rows = jnp.zeros((LANES - blocks_per_batch, tm), F32)

    for hh in range(ATTN_HEADS):
        sl = slice(hh * HEAD_DIM, (hh + 1) * HEAD_DIM)
        qn = _rmsnorm(q[:, sl], gq_ref[...])
        kn = _rmsnorm(k[:, sl], gk_ref[...])

        kmean = km_scr[:, sl]
        for b in range(tm // MOBA_BLOCK):
            km = jnp.mean(kn[b * MOBA_BLOCK:(b + 1) * MOBA_BLOCK], axis=0, keepdims=True)
            kmean = jnp.where(blk_rows == n0 + b, km, kmean)
        km_scr[:, sl] = kmean

        gate = _dot_nt(kmean, qn, precision=lax.Precision.HIGHEST)
        cnt = jnp.zeros((blocks_per_batch, tm), jnp.int32)
        for m in range(blocks_per_batch - 1):
            gm = gate[m:m + 1, :]
            beats = (gm > gate) | ((gm == gate) & (m < n_iota))
            cnt = cnt + jnp.where(beats & (m < q_blk), 1, 0)
        sel = ((cnt < MOBA_TOPK) & valid) | (n_iota == q_blk)
        bias_t = jnp.concatenate([jnp.where(sel, 0.0, NEG_INF), pad_rows], axis=0)
        bias = bias_t.T

        a0 = hh * AUG_DIM
        qa_ref[:, a0:a0 + HEAD_DIM] = (qn * scale).astype(BF16)
        qa_ref[:, a0 + HEAD_DIM:a0 + AUG_DIM] = bias.astype(BF16)
        ka_ref[:, a0:a0 + HEAD_DIM] = kn.astype(BF16)
        ka_ref[:, a0 + HEAD_DIM:a0 + AUG_DIM] = onehot


def _qkv(x, x_gain, w_in, q_gain, k_gain, seq):
    tokens = x.shape[0]
    tm = PROJ_TM
    tiles_per_batch = seq // tm
    blocks_per_batch = seq // MOBA_BLOCK
    assert blocks_per_batch <= LANES and blocks_per_batch % 8 == 0
    row = lambda width: pl.BlockSpec((tm, width), lambda i: (i, 0))
    return pl.pallas_call(
        functools.partial(_qkv_body, tiles_per_batch=tiles_per_batch),
        grid=(tokens // tm,),
        in_specs=[row(D_MODEL), _resident((1, D_MODEL)), _resident((D_MODEL, 3 * D_ATTN)),
                  _resident((1, HEAD_DIM)), _resident((1, HEAD_DIM))],
        out_specs=(row(ATTN_HEADS * AUG_DIM), row(ATTN_HEADS * AUG_DIM), row(D_ATTN)),
        out_shape=(jax.ShapeDtypeStruct((tokens, ATTN_HEADS * AUG_DIM), BF16),
                   jax.ShapeDtypeStruct((tokens, ATTN_HEADS * AUG_DIM), BF16),
                   jax.ShapeDtypeStruct((tokens, D_ATTN), BF16)),
        scratch_shapes=[pltpu.VMEM((blocks_per_batch, D_ATTN), F32)],
        compiler_params=_params("arbitrary"),
        name="qkv_gate",
    )(x, x_gain, w_in, q_gain, k_gain)


def _sgu_body(x_ref, gx_ref, wu_ref, wg_ref, lng_ref, lnb_ref, ws_ref, bs_ref, og_ref, o_ref, sg_scr):
    tm = PROJ_TM
    h = _rmsnorm(x_ref[...], gx_ref[...]).astype(BF16)
    u = _gelu_tanh(_dot(h, wu_ref[...]))
    g = _gelu_tanh(_dot(h, wg_ref[...]))
    r = lax.broadcasted_iota(jnp.int32, (SGU_CHUNK, SGU_CHUNK), 0)
    c = lax.broadcasted_iota(jnp.int32, (SGU_CHUNK, SGU_CHUNK), 1)
    for gi in range(SGU_GROUPS):
        sl = slice(gi * SGU_GROUP_DIM, (gi + 1) * SGU_GROUP_DIM)
        gg = g[:, sl]
        d = gg - jnp.mean(gg, axis=-1, keepdims=True)
        var = jnp.mean(d * d, axis=-1, keepdims=True)
        gn = (d * lax.rsqrt(var + EPS) * lng_ref[:, sl] + lnb_ref[:, sl]).astype(BF16)
        w_causal = jnp.where(r >= c, ws_ref[gi], 0.0).astype(BF16)
        b_col = bs_ref[:, gi:gi + 1]
        for ci in range(tm // SGU_CHUNK):
            rows = slice(ci * SGU_CHUNK, (ci + 1) * SGU_CHUNK)
            mixed = _dot(w_causal, gn[rows]) + b_col
            sg_scr[rows, sl] = u[rows, sl] * mixed
    o_ref[...] = _rmsnorm(sg_scr[...], og_ref[...]).astype(BF16)


def _sgu(x, x_gain, w_in, ln_gain, ln_bias, w_spatial, b_spatial_t, out_gain):
    tokens = x.shape[0]
    tm = PROJ_TM
    w_block = lambda c: pl.BlockSpec((D_MODEL, D_SGU), lambda i: (0, c), pipeline_mode=pl.Buffered(1))
    first = 3 * D_ATTN // D_SGU
    return pl.pallas_call(
        _sgu_body,
        grid=(tokens // tm,),
        in_specs=[pl.BlockSpec((tm, D_MODEL), lambda i: (i, 0)), _resident((1, D_MODEL)),
                  w_block(first), w_block(first + 1),
                  _resident((1, D_SGU)), _resident((1, D_SGU)),
                  _resident((SGU_GROUPS, SGU_CHUNK, SGU_CHUNK)),
                  _resident((SGU_CHUNK, SGU_GROUPS)),
                  _resident((1, D_SGU))],
        out_specs=pl.BlockSpec((tm, D_SGU), lambda i: (i, 0)),
        out_shape=jax.ShapeDtypeStruct((tokens, D_SGU), BF16),
        scratch_shapes=[pltpu.VMEM((tm, D_SGU), F32)],
        compiler_params=_params("parallel"),
        name="sgu",
    )(x, x_gain, w_in, w_in, ln_gain, ln_bias, w_spatial, b_spatial_t, out_gain)


def _attn_scores(t, q_ref, k_ref, s_ref, m_ref):
    tq, kc = ATTN_TQ, ATTN_KV_CHUNK
    m_part = None
    for c in range(t + 1):
        s = _dot_nt(q_ref[...], k_ref[c * kc:(c + 1) * kc, :])
        if c == t:
            r = lax.broadcasted_iota(jnp.int32, (tq, kc), 0)
            col = lax.broadcasted_iota(jnp.int32, (tq, kc), 1)
            s = jnp.where(col <= r, s, NEG_INF)
        s_ref[:, c * kc:(c + 1) * kc] = s
        for j in range(kc // LANES):
            sj = s[:, j * LANES:(j + 1) * LANES]
            m_part = sj if m_part is None else jnp.maximum(m_part, sj)
    m_ref[...] = jnp.broadcast_to(jnp.max(m_part, axis=-1, keepdims=True), (tq, LANES))


def _attn_values(t, va_ref, o_ref, s_ref, m_ref):
    tq, kc = ATTN_TQ, ATTN_KV_CHUNK
    m = m_ref[...]
    acc = jnp.zeros((tq, AUG_DIM), F32)
    for c in range(t + 1):
        p = jnp.concatenate(
            [jnp.exp(s_ref[:, c * kc + j * LANES:c * kc + (j + 1) * LANES] - m).astype(BF16)
             for j in range(kc // LANES)], axis=1)
        acc = acc + _dot(p, va_ref[c * kc:(c + 1) * kc, :])
    o_ref[...] = acc[:, 0:HEAD_DIM] / acc[:, HEAD_DIM:AUG_DIM]


def _attn_body(q_ref, k_ref, v_ref, o_ref, s_scr, m_scr, va_scr):
    n_tiles = k_ref.shape[0] // ATTN_TQ
    tile = lambda t: slice(t * ATTN_TQ, (t + 1) * ATTN_TQ)
    va_scr[:, 0:HEAD_DIM] = v_ref[...]
    va_scr[:, HEAD_DIM:AUG_DIM] = jnp.ones((va_scr.shape[0], HEAD_DIM), BF16)

    def sub_step(ts):
        if ts >= 1:
            _attn_values(ts - 1, va_scr, o_ref.at[tile(ts - 1), :], s_scr.at[(ts - 1) % 2], m_scr.at[(ts - 1) % 2])
        if ts < n_tiles:
            _attn_scores(ts, q_ref.at[tile(ts), :], k_ref, s_scr.at[ts % 2], m_scr.at[ts % 2])

    for ts in range(n_tiles + 1):
        pl.when(pl.program_id(1) + ts >= 0)(functools.partial(sub_step, ts))


def _attention(qa, ka, v, batch, seq):
    tokens = batch * seq
    return pl.pallas_call(
        _attn_body,
        grid=(batch, ATTN_HEADS),
        in_specs=[pl.BlockSpec((seq, AUG_DIM), lambda b, h: (b, h)),
                  pl.BlockSpec((seq, AUG_DIM), lambda b, h: (b, h)),
                  pl.BlockSpec((seq, HEAD_DIM), lambda b, h: (b, h))],
        out_specs=pl.BlockSpec((seq, HEAD_DIM), lambda b, h: (b, h)),
        out_shape=jax.ShapeDtypeStruct((tokens, D_ATTN), F32),
        scratch_shapes=[pltpu.VMEM((2, ATTN_TQ, seq), F32), pltpu.VMEM((2, ATTN_TQ, LANES), F32),
                        pltpu.VMEM((seq, AUG_DIM), BF16)],
        compiler_params=_params("parallel", "parallel"),
        name="moba_attn",
    )(qa, ka, v)


def _outproj_body(a_ref, s_ref, x_ref, ga_ref, w_ref, o_ref):
    an = _rmsnorm(a_ref[...], ga_ref[...]).astype(BF16)
    o_ref[...] = (x_ref[...] + _dot(an, w_ref[0:D_ATTN, :])
                  + _dot(s_ref[...], w_ref[D_ATTN:D_ATTN + D_SGU, :]))


def _outproj(attn, sgu, x, attn_gain, w_out):
    tokens = x.shape[0]
    tm = PROJ_TM
    row = lambda width: pl.BlockSpec((tm, width), lambda i: (i, 0))
    return pl.pallas_call(
        _outproj_body,
        grid=(tokens // tm,),
        in_specs=[row(D_ATTN), row(D_SGU), row(D_MODEL), _resident((1, D_ATTN)),
                  _resident((D_ATTN + D_SGU, D_MODEL))],
        out_specs=row(D_MODEL),
        out_shape=jax.ShapeDtypeStruct((tokens, D_MODEL), F32),
        compiler_params=_params("parallel"),
        name="outproj",
    )(attn, sgu, x, attn_gain, w_out)


def kernel(x, ffn1_norm, ffn1_w_gate, ffn1_w_up, ffn1_w_down, mix_norm, w_in, q_norm, k_norm, sgu_ln_gain, sgu_ln_bias, sgu_w_spatial, sgu_b_spatial, attn_out_gain, sgu_out_gain, w_out, ffn2_norm, ffn2_w_gate, ffn2_w_up, ffn2_w_down):
    batch, seq, d_model = x.shape
    depth = ffn1_norm.shape[0]
    assert d_model == D_MODEL and seq % PROJ_TM == 0 and (batch * seq) % FFN_TM == 0
    xt = x.reshape(batch * seq, d_model)
    vec = lambda a: a.reshape(1, -1)
    for l in range(depth):
        xt, w_in_bf, w_out_bf = _ffn(xt, vec(ffn1_norm[l]), ffn1_w_gate[l], ffn1_w_up[l], ffn1_w_down[l],
                                     side=(w_in[l], w_out[l]))
        qa, ka, v = _qkv(xt, vec(mix_norm[l]), w_in_bf, vec(q_norm[l]), vec(k_norm[l]), seq)
        sgu = _sgu(xt, vec(mix_norm[l]), w_in_bf, vec(sgu_ln_gain[l]), vec(sgu_ln_bias[l]),
                   sgu_w_spatial[l], sgu_b_spatial[l].T, vec(sgu_out_gain[l]))
        attn = _attention(qa, ka, v, batch, seq)
        xt = _outproj(attn, sgu, xt, vec(attn_out_gain[l]), w_out_bf)
        xt = _ffn(xt, vec(ffn2_norm[l]), ffn2_w_gate[l], ffn2_w_up[l], ffn2_w_down[l])
    return xt.reshape(batch, seq, d_model)
```

```python
import functools

import jax
import jax.numpy as jnp
from jax import lax
from jax.experimental import pallas as pl
from jax.experimental.pallas import tpu as pltpu

D_MODEL = 2048
D_FF = 5632
ATTN_HEADS = 8
HEAD_DIM = 128
D_ATTN = ATTN_HEADS * HEAD_DIM
SGU_GROUPS = 8
SGU_GROUP_DIM = 128
D_SGU = SGU_GROUPS * SGU_GROUP_DIM
MOBA_BLOCK = 256
MOBA_TOPK = 3
SGU_CHUNK = 128
EPS = 1e-6
NEG_INF = -1e30

LANES = 128
AUG_DIM = 2 * HEAD_DIM
VMEM_LIMIT_BYTES = 60 * 1024 * 1024

FFN_TM = 1024
FFN_HEAD_TF = 256
FFN_TF = 512
PROJ_TM = 512
ATTN_TQ = 512
ATTN_KV_CHUNK = 512

F32 = jnp.float32
BF16 = jnp.bfloat16


def _rmsnorm(x, gain):
    return x * lax.rsqrt(jnp.mean(x * x, axis=-1, keepdims=True) + EPS) * gain


def _gelu_tanh(x):
    cdf = 0.5 * (1.0 + jnp.tanh(0.7978845608028654 * (x + 0.044715 * (x * x * x))))
    return x * cdf


def _dot(a, b):
    return jnp.dot(a, b, preferred_element_type=F32)


def _dot_nt(a, b, precision=None):
    return lax.dot_general(a, b, (((1,), (1,)), ((), ())), precision=precision,
                           preferred_element_type=F32)


def _params(*semantics):
    return pltpu.CompilerParams(dimension_semantics=semantics, vmem_limit_bytes=VMEM_LIMIT_BYTES)


def _resident(shape):
    return pl.BlockSpec(shape, lambda *_: (0,) * len(shape), pipeline_mode=pl.Buffered(1))


def _ffn_accumulate(h_scr, wg, wu, wd, base_ref, o_ref):
    h = h_scr[...]
    gate = _dot(h, wg)
    up = _dot(h, wu)
    act = (gate / (1.0 + jnp.exp(-gate))) * up * 0.5
    o_ref[...] = base_ref[...] + _dot(act.astype(BF16), wd)


def _ffn_step(first, x_ref, g_ref, h_scr, wg, wu, wd, o_ref):
    if first:
        h_scr[...] = _rmsnorm(x_ref[...], g_ref[...]).astype(BF16)
    _ffn_accumulate(h_scr, wg, wu, wd, x_ref if first else o_ref, o_ref)


def _ffn_head_body(x_ref, g_ref, wg_ref, wu_ref, wd_ref, o_ref, wg_bf_ref, wu_bf_ref, wd_bf_ref, h_scr):
    @pl.when(pl.program_id(0) == 0)
    def _():
        x = x_ref[...]
        h_scr[...] = _rmsnorm(x, g_ref[...]).astype(BF16)
        o_ref[...] = x

    wg_bf_ref[...] = wg_ref[...].astype(BF16)
    wu_bf_ref[...] = wu_ref[...].astype(BF16)
    wd_bf_ref[...] = wd_ref[...].astype(BF16)
    _ffn_accumulate(h_scr, wg_bf_ref[...], wu_bf_ref[...], wd_bf_ref[...], o_ref, o_ref)


def _ffn_tail_body(x_ref, g_ref, wg_ref, wu_ref, wd_ref, head_hbm, *rest, side_windows):
    n_side = len(side_windows)
    side_in, (o_ref, *side_out), h_scr = rest[:n_side], rest[n_side:-1], rest[-1]
    i = pl.program_id(0)
    j = pl.program_id(1)

    step = i * pl.num_programs(1) + j
    for src, dst, (start, n) in zip(side_in, side_out, side_windows):
        @pl.when((step >= start) & (step < start + n))
        def _(src=src, dst=dst):
            dst[...] = src[...].astype(BF16)

    @pl.when((i == 0) & (j == 0))
    def _():
        pltpu.sync_copy(head_hbm, o_ref)

    for first in (True, False):
        @pl.when((i > 0) & ((j == 0) if first else (j > 0)))
        def _(first=first):
            _ffn_step(first, x_ref, g_ref, h_scr, wg_ref[...], wu_ref[...], wd_ref[...], o_ref)


def _ffn(x, gain, w_gate, w_up, w_down, side=()):
    tokens = x.shape[0]
    n_tiles = tokens // FFN_TM
    vec = pl.BlockSpec((1, D_MODEL), lambda *_: (0, 0))

    tf = FFN_HEAD_TF
    col = pl.BlockSpec((D_MODEL, tf), lambda j: (0, j))
    rowblk = pl.BlockSpec((tf, D_MODEL), lambda j: (j, 0))
    tile0 = pl.BlockSpec((FFN_TM, D_MODEL), lambda j: (0, 0))
    head, wg_bf, wu_bf, wd_bf = pl.pallas_call(
        _ffn_head_body,
        grid=(D_FF // tf,),
        in_specs=[tile0, vec, col, col, rowblk],
        out_specs=(tile0, col, col, rowblk),
        out_shape=(jax.ShapeDtypeStruct((FFN_TM, D_MODEL), F32),
                   jax.ShapeDtypeStruct((D_MODEL, D_FF), BF16),
                   jax.ShapeDtypeStruct((D_MODEL, D_FF), BF16),
                   jax.ShapeDtypeStruct((D_FF, D_MODEL), BF16)),
        scratch_shapes=[pltpu.VMEM((FFN_TM, D_MODEL), BF16)],
        compiler_params=_params("arbitrary"),
        name="ffn_head",
    )(x, gain, w_gate, w_up, w_down)

    tf = FFN_TF
    wcol = lambda i, j: (0, jnp.where(i == 0, 0, j))
    wrow = lambda i, j: (jnp.where(i == 0, 0, j), 0)
    row = pl.BlockSpec((FFN_TM, D_MODEL), lambda i, j: (i, 0))
    n_steps = D_FF // tf
    side_specs, side_windows, start = [], [], 0
    for w in side:
        n_side_tiles = w.shape[1] // LANES
        side_specs.append(pl.BlockSpec(
            (w.shape[0], LANES),
            lambda i, j, start=start, last=n_side_tiles - 1: (0, jnp.clip(i * n_steps + j - start, 0, last))))
        side_windows.append((start, n_side_tiles))
        start += n_side_tiles
    assert start <= n_tiles * n_steps
    outs = pl.pallas_call(
        functools.partial(_ffn_tail_body, side_windows=tuple(side_windows)),
        grid=(n_tiles, n_steps),
        in_specs=[row, vec, pl.BlockSpec((D_MODEL, tf), wcol), pl.BlockSpec((D_MODEL, tf), wcol),
                  pl.BlockSpec((tf, D_MODEL), wrow), pl.BlockSpec(memory_space=pl.ANY), *side_specs],
        out_specs=(row, *side_specs),
        out_shape=(jax.ShapeDtypeStruct((tokens, D_MODEL), F32),
                   *(jax.ShapeDtypeStruct(w.shape, BF16) for w in side)),
        scratch_shapes=[pltpu.VMEM((FFN_TM, D_MODEL), BF16)],
        compiler_params=_params("arbitrary", "arbitrary"),
        name="ffn_tail",
    )(x, gain, wg_bf, wu_bf, wd_bf, head, *side)
    return outs if side else outs[0]


def _qkv_body(x_ref, gx_ref, w_ref, gq_ref, gk_ref, qa_ref, ka_ref, v_ref, km_scr, *, tiles_per_batch):
    tm = PROJ_TM
    blocks_per_batch = tiles_per_batch * (tm // MOBA_BLOCK)
    i = pl.program_id(0)
    n0 = (i % tiles_per_batch) * (tm // MOBA_BLOCK)

    @pl.when(i == 0)
    def _init():
        km_scr[...] = jnp.zeros_like(km_scr)

    h = _rmsnorm(x_ref[...], gx_ref[...]).astype(BF16)
    q = _dot(h, w_ref[:, 0:D_ATTN])
    k = _dot(h, w_ref[:, D_ATTN:2 * D_ATTN])
    v_ref[...] = _dot(h, w_ref[:, 2 * D_ATTN:3 * D_ATTN]).astype(BF16)

    scale = HEAD_DIM ** -0.5
    blk_rows = lax.broadcasted_iota(jnp.int32, (blocks_per_batch, HEAD_DIM), 0)
    n_iota = lax.broadcasted_iota(jnp.int32, (blocks_per_batch, tm), 0)
    lane_tok = lax.broadcasted_iota(jnp.int32, (blocks_per_batch, tm), 1)
    q_blk = n0 + lane_tok // MOBA_BLOCK
    valid = n_iota < q_blk
    tok_blk = n0 + lax.broadcasted_iota(jnp.int32, (tm, LANES), 0) // MOBA_BLOCK
    onehot = jnp.where(lax.broadcasted_iota(jnp.int32, (tm, LANES), 1) == tok_blk, 1.0, 0.0).astype(BF16)
    pad_rows = jnp.zeros((LANES - blocks_per_batch, tm), F32)

    for hh in range(ATTN_HEADS):
        sl = slice(hh * HEAD_DIM, (hh + 1) * HEAD_DIM)
        qn = _rmsnorm(q[:, sl], gq_ref[...])
        kn = _rmsnorm(k[:, sl], gk_ref[...])

        kmean = km_scr[:, sl]
        for b in range(tm // MOBA_BLOCK):
            km = jnp.mean(kn[b * MOBA_BLOCK:(b + 1) * MOBA_BLOCK], axis=0, keepdims=True)
            kmean = jnp.where(blk_rows == n0 + b, km, kmean)
        km_scr[:, sl] = kmean

        gate = _dot_nt(kmean, qn, precision=lax.Precision.HIGHEST)
        cnt = jnp.zeros((blocks_per_batch, tm), jnp.int32)
        for m in range(blocks_per_batch - 1):
            gm = gate[m:m + 1, :]
            beats = (gm > gate) | ((gm == gate) & (m < n_iota))
            cnt = cnt + jnp.where(beats & (m < q_blk), 1, 0)
        sel = ((cnt < MOBA_TOPK) & valid) | (n_iota == q_blk)
        bias_t = jnp.concatenate([jnp.where(sel, 0.0, NEG_INF), pad_rows], axis=0)
        bias = bias_t.T

        a0 = hh * AUG_DIM
        qa_ref[:, a0:a0 + HEAD_DIM] = (qn * scale).astype(BF16)
        qa_ref[:, a0 + HEAD_DIM:a0 + AUG_DIM] = bias.astype(BF16)
        ka_ref[:, a0:a0 + HEAD_DIM] = kn.astype(BF16)
        ka_ref[:, a0 + HEAD_DIM:a0 + AUG_DIM] = onehot


def _qkv(x, x_gain, w_in, q_gain, k_gain, seq):
    tokens = x.shape[0]
    tm = PROJ_TM
    tiles_per_batch = seq // tm
    blocks_per_batch = seq // MOBA_BLOCK
    assert blocks_per_batch <= LANES and blocks_per_batch % 8 == 0
    row = lambda width: pl.BlockSpec((tm, width), lambda i: (i, 0))
    return pl.pallas_call(
        functools.partial(_qkv_body, tiles_per_batch=tiles_per_batch),
        grid=(tokens // tm,),
        in_specs=[row(D_MODEL), _resident((1, D_MODEL)), _resident((D_MODEL, 3 * D_ATTN)),
                  _resident((1, HEAD_DIM)), _resident((1, HEAD_DIM))],
        out_specs=(row(ATTN_HEADS * AUG_DIM), row(ATTN_HEADS * AUG_DIM), row(D_ATTN)),
        out_shape=(jax.ShapeDtypeStruct((tokens, ATTN_HEADS * AUG_DIM), BF16),
                   jax.ShapeDtypeStruct((tokens, ATTN_HEADS * AUG_DIM), BF16),
                   jax.ShapeDtypeStruct((tokens, D_ATTN), BF16)),
        scratch_shapes=[pltpu.VMEM((blocks_per_batch, D_ATTN), F32)],
        compiler_params=_params("arbitrary"),
        name="qkv_gate",
    )(x, x_gain, w_in, q_gain, k_gain)


def _sgu_body(x_ref, gx_ref, wu_ref, wg_ref, lng_ref, lnb_ref, ws_ref, bs_ref, og_ref, o_ref, sg_scr):
    tm = PROJ_TM
    h = _rmsnorm(x_ref[...], gx_ref[...]).astype(BF16)
    u = _gelu_tanh(_dot(h, wu_ref[...]))
    g = _gelu_tanh(_dot(h, wg_ref[...]))
    r = lax.broadcasted_iota(jnp.int32, (SGU_CHUNK, SGU_CHUNK), 0)
    c = lax.broadcasted_iota(jnp.int32, (SGU_CHUNK, SGU_CHUNK), 1)
    for gi in range(SGU_GROUPS):
        sl = slice(gi * SGU_GROUP_DIM, (gi + 1) * SGU_GROUP_DIM)
        gg = g[:, sl]
        d = gg - jnp.mean(gg, axis=-1, keepdims=True)
        var = jnp.mean(d * d, axis=-1, keepdims=True)
        gn = (d * lax.rsqrt(var + EPS) * lng_ref[:, sl] + lnb_ref[:, sl]).astype(BF16)
        w_causal = jnp.where(r >= c, ws_ref[gi], 0.0).astype(BF16)
        b_col = bs_ref[:, gi:gi + 1]
        for ci in range(tm // SGU_CHUNK):
            rows = slice(ci * SGU_CHUNK, (ci + 1) * SGU_CHUNK)
            mixed = _dot(w_causal, gn[rows]) + b_col
            sg_scr[rows, sl] = u[rows, sl] * mixed
    o_ref[...] = _rmsnorm(sg_scr[...], og_ref[...]).astype(BF16)


def _sgu(x, x_gain, w_in, ln_gain, ln_bias, w_spatial, b_spatial_t, out_gain):
    tokens = x.shape[0]
    tm = PROJ_TM
    w_block = lambda c: pl.BlockSpec((D_MODEL, D_SGU), lambda i: (0, c), pipeline_mode=pl.Buffered(1))
    first = 3 * D_ATTN // D_SGU
    return pl.pallas_call(
        _sgu_body,
        grid=(tokens // tm,),
        in_specs=[pl.BlockSpec((tm, D_MODEL), lambda i: (i, 0)), _resident((1, D_MODEL)),
                  w_block(first), w_block(first + 1),
                  _resident((1, D_SGU)), _resident((1, D_SGU)),
                  _resident((SGU_GROUPS, SGU_CHUNK, SGU_CHUNK)),
                  _resident((SGU_CHUNK, SGU_GROUPS)),
                  _resident((1, D_SGU))],
        out_specs=pl.BlockSpec((tm, D_SGU), lambda i: (i, 0)),
        out_shape=jax.ShapeDtypeStruct((tokens, D_SGU), BF16),
        scratch_shapes=[pltpu.VMEM((tm, D_SGU), F32)],
        compiler_params=_params("parallel"),
        name="sgu",
    )(x, x_gain, w_in, w_in, ln_gain, ln_bias, w_spatial, b_spatial_t, out_gain)


def _attn_scores(t, q_ref, k_ref, s_ref, m_ref):
    tq, kc = ATTN_TQ, ATTN_KV_CHUNK
    m_part = None
    for c in range(t + 1):
        s = _dot_nt(q_ref[...], k_ref[c * kc:(c + 1) * kc, :])
        if c == t:
            r = lax.broadcasted_iota(jnp.int32, (tq, kc), 0)
            col = lax.broadcasted_iota(jnp.int32, (tq, kc), 1)
            s = jnp.where(col <= r, s, NEG_INF)
        s_ref[:, c * kc:(c + 1) * kc] = s
        for j in range(kc // LANES):
            sj = s[:, j * LANES:(j + 1) * LANES]
            m_part = sj if m_part is None else jnp.maximum(m_part, sj)
    m_ref[...] = jnp.broadcast_to(jnp.max(m_part, axis=-1, keepdims=True), (tq, LANES))


def _attn_values(t, va_ref, o_ref, s_ref, m_ref):
    tq, kc = ATTN_TQ, ATTN_KV_CHUNK
    m = m_ref[...]
    acc = jnp.zeros((tq, AUG_DIM), F32)
    for c in range(t + 1):
        p = jnp.concatenate(
            [jnp.exp(s_ref[:, c * kc + j * LANES:c * kc + (j + 1) * LANES] - m).astype(BF16)
             for j in range(kc // LANES)], axis=1)
        acc = acc + _dot(p, va_ref[c * kc:(c + 1) * kc, :])
    o_ref[...] = acc[:, 0:HEAD_DIM] / acc[:, HEAD_DIM:AUG_DIM]


def _attn_body(q_ref, k_ref, v_ref, o_ref, s_scr, m_scr, va_scr):
    n_tiles = k_ref.shape[0] // ATTN_TQ
    tile = lambda t: slice(t * ATTN_TQ, (t + 1) * ATTN_TQ)
    va_scr[:, 0:HEAD_DIM] = v_ref[...]
    va_scr[:, HEAD_DIM:AUG_DIM] = jnp.ones((va_scr.shape[0], HEAD_DIM), BF16)

    def sub_step(ts):
        if ts >= 1:
            _attn_values(ts - 1, va_scr, o_ref.at[tile(ts - 1), :], s_scr.at[(ts - 1) % 2], m_scr.at[(ts - 1) % 2])
        if ts < n_tiles:
            _attn_scores(ts, q_ref.at[tile(ts), :], k_ref, s_scr.at[ts % 2], m_scr.at[ts % 2])

    for ts in range(n_tiles + 1):
        pl.when(pl.program_id(1) + ts >= 0)(functools.partial(sub_step, ts))


def _attention(qa, ka, v, batch, seq):
    tokens = batch * seq
    return pl.pallas_call(
        _attn_body,
        grid=(batch, ATTN_HEADS),
        in_specs=[pl.BlockSpec((seq, AUG_DIM), lambda b, h: (b, h)),
                  pl.BlockSpec((seq, AUG_DIM), lambda b, h: (b, h)),
                  pl.BlockSpec((seq, HEAD_DIM), lambda b, h: (b, h))],
        out_specs=pl.BlockSpec((seq, HEAD_DIM), lambda b, h: (b, h)),
        out_shape=jax.ShapeDtypeStruct((tokens, D_ATTN), F32),
        scratch_shapes=[pltpu.VMEM((2, ATTN_TQ, seq), F32), pltpu.VMEM((2, ATTN_TQ, LANES), F32),
                        pltpu.VMEM((seq, AUG_DIM), BF16)],
        compiler_params=_params("parallel", "parallel"),
        name="moba_attn",
    )(qa, ka, v)


def _outproj_body(a_ref, s_ref, x_ref, ga_ref, w_ref, o_ref):
    an = _rmsnorm(a_ref[...], ga_ref[...]).astype(BF16)
    o_ref[...] = (x_ref[...] + _dot(an, w_ref[0:D_ATTN, :])
                  + _dot(s_ref[...], w_ref[D_ATTN:D_ATTN + D_SGU, :]))


def _outproj(attn, sgu, x, attn_gain, w_out):
    tokens = x.shape[0]
    tm = PROJ_TM
    row = lambda width: pl.BlockSpec((tm, width), lambda i: (i, 0))
    return pl.pallas_call(
        _outproj_body,
        grid=(tokens // tm,),
        in_specs=[row(D_ATTN), row(D_SGU), row(D_MODEL), _resident((1, D_ATTN)),
                  _resident((D_ATTN + D_SGU, D_MODEL))],
        out_specs=row(D_MODEL),
        out_shape=jax.ShapeDtypeStruct((tokens, D_MODEL), F32),
        compiler_params=_params("parallel"),
        name="outproj",
    )(attn, sgu, x, attn_gain, w_out)


def kernel(x, ffn1_norm, ffn1_w_gate, ffn1_w_up, ffn1_w_down, mix_norm, w_in, q_norm, k_norm, sgu_ln_gain, sgu_ln_bias, sgu_w_spatial, sgu_b_spatial, attn_out_gain, sgu_out_gain, w_out, ffn2_norm, ffn2_w_gate, ffn2_w_up, ffn2_w_down):
    batch, seq, d_model = x.shape
    depth = ffn1_norm.shape[0]
    assert d_model == D_MODEL and seq % PROJ_TM == 0 and (batch * seq) % FFN_TM == 0
    xt = x.reshape(batch * seq, d_model)
    vec = lambda a: a.reshape(1, -1)
    for l in range(depth):
        xt, w_in_bf, w_out_bf = _ffn(xt, vec(ffn1_norm[l]), ffn1_w_gate[l], ffn1_w_up[l], ffn1_w_down[l],
                                     side=(w_in[l], w_out[l]))
        qa, ka, v = _qkv(xt, vec(mix_norm[l]), w_in_bf, vec(q_norm[l]), vec(k_norm[l]), seq)
        sgu = _sgu(xt, vec(mix_norm[l]), w_in_bf, vec(sgu_ln_gain[l]), vec(sgu_ln_bias[l]),
                   sgu_w_spatial[l], sgu_b_spatial[l].T, vec(sgu_out_gain[l]))
        attn = _attention(qa, ka, v, batch, seq)
        xt = _outproj(attn, sgu, xt, vec(attn_out_gain[l]), w_out_bf)
        xt = _ffn(xt, vec(ffn2_norm[l]), ffn2_w_gate[l], ffn2_w_up[l], ffn2_w_down[l])
    return xt.reshape(batch, seq, d_model)
```

```python
import functools

import jax
import jax.numpy as jnp
from jax import lax
from jax.experimental import pallas as pl
from jax.experimental.pallas import tpu as pltpu

D_MODEL = 2048
D_FF = 5632
ATTN_HEADS = 8
HEAD_DIM = 128
D_ATTN = ATTN_HEADS * HEAD_DIM
SGU_GROUPS = 8
SGU_GROUP_DIM = 128
D_SGU = SGU_GROUPS * SGU_GROUP_DIM
MOBA_BLOCK = 256
MOBA_TOPK = 3
SGU_CHUNK = 128
EPS = 1e-6
NEG_INF = -1e30

LANES = 128
AUG_DIM = 2 * HEAD_DIM
VMEM_LIMIT_BYTES = 60 * 1024 * 1024

FFN_TM = 1024
FFN_HEAD_TF = 256
FFN_TF = 512
PROJ_TM = 512
ATTN_TQ = 512
ATTN_KV_CHUNK = 512

F32 = jnp.float32
BF16 = jnp.bfloat16


def _rmsnorm(x, gain):
    return x * lax.rsqrt(jnp.mean(x * x, axis=-1, keepdims=True) + EPS) * gain


def _gelu_tanh(x):
    cdf = 0.5 * (1.0 + jnp.tanh(0.7978845608028654 * (x + 0.044715 * (x * x * x))))
    return x * cdf


def _dot(a, b):
    return jnp.dot(a, b, preferred_element_type=F32)


def _dot_nt(a, b, precision=None):
    return lax.dot_general(a, b, (((1,), (1,)), ((), ())), precision=precision,
                           preferred_element_type=F32)


def _params(*semantics):
    return pltpu.CompilerParams(dimension_semantics=semantics, vmem_limit_bytes=VMEM_LIMIT_BYTES)


def _resident(shape):
    return pl.BlockSpec(shape, lambda *_: (0,) * len(shape), pipeline_mode=pl.Buffered(1))


def _ffn_accumulate(h_scr, wg, wu, wd, base_ref, o_ref):
    h = h_scr[...]
    gate = jnp.concatenate([_dot(h, wg[b]) for b in range(wg.shape[0])], axis=1)
    up = jnp.concatenate([_dot(h, wu[b]) for b in range(wu.shape[0])], axis=1)
    act = (gate / (1.0 + jnp.exp(-gate))) * up * 0.5
    o_ref[...] = base_ref[...] + _dot(act.astype(BF16), wd)


def _ffn_step(first, x_ref, g_ref, h_scr, wg, wu, wd, o_ref):
    if first:
        h_scr[...] = _rmsnorm(x_ref[...], g_ref[...]).astype(BF16)
    _ffn_accumulate(h_scr, wg, wu, wd, x_ref if first else o_ref, o_ref)


def _ffn_head_body(x_ref, g_ref, wg_ref, wu_ref, wd_ref, o_ref, wg_bf_ref, wu_bf_ref, wd_bf_ref, h_scr):
    @pl.when(pl.program_id(0) == 0)
    def _():
        x = x_ref[...]
        h_scr[...] = _rmsnorm(x, g_ref[...]).astype(BF16)
        o_ref[...] = x

    wg_bf_ref[0] = wg_ref[...].astype(BF16)
    wu_bf_ref[0] = wu_ref[...].astype(BF16)
    wd_bf_ref[...] = wd_ref[...].astype(BF16)
    _ffn_accumulate(h_scr, wg_bf_ref, wu_bf_ref, wd_bf_ref[...], o_ref, o_ref)


def _ffn_tail_body(x_ref, g_ref, wg_ref, wu_ref, wd_ref, head_hbm, *rest, side_windows):
    n_side = len(side_windows)
    side_in, (o_ref, *side_out), h_scr = rest[:n_side], rest[n_side:-1], rest[-1]
    i = pl.program_id(0)
    j = pl.program_id(1)

    step = i * pl.num_programs(1) + j
    for src, dst, (start, n) in zip(side_in, side_out, side_windows):
        @pl.when((step >= start) & (step < start + n))
        def _(src=src, dst=dst):
            dst[...] = src[...].astype(BF16)

    @pl.when((i == 0) & (j == 0))
    def _():
        pltpu.sync_copy(head_hbm, o_ref)

    for first in (True, False):
        @pl.when((i > 0) & ((j == 0) if first else (j > 0)))
        def _(first=first):
            _ffn_step(first, x_ref, g_ref, h_scr, wg_ref, wu_ref, wd_ref[...], o_ref)


def _ffn(x, gain, w_gate, w_up, w_down, side=()):
    tokens = x.shape[0]
    n_tiles = tokens // FFN_TM
    vec = pl.BlockSpec((1, D_MODEL), lambda *_: (0, 0))

    tf = FFN_HEAD_TF
    col = pl.BlockSpec((D_MODEL, tf), lambda j: (0, j))
    colblk = pl.BlockSpec((1, D_MODEL, tf), lambda j: (j, 0, 0))
    rowblk = pl.BlockSpec((tf, D_MODEL), lambda j: (j, 0))
    tile0 = pl.BlockSpec((FFN_TM, D_MODEL), lambda j: (0, 0))
    head, wg_bf, wu_bf, wd_bf = pl.pallas_call(
        _ffn_head_body,
        grid=(D_FF // tf,),
        in_specs=[tile0, vec, col, col, rowblk],
        out_specs=(tile0, colblk, colblk, rowblk),
        out_shape=(jax.ShapeDtypeStruct((FFN_TM, D_MODEL), F32),
                   jax.ShapeDtypeStruct((D_FF // tf, D_MODEL, tf), BF16),
                   jax.ShapeDtypeStruct((D_FF // tf, D_MODEL, tf), BF16),
                   jax.ShapeDtypeStruct((D_FF, D_MODEL), BF16)),
        scratch_shapes=[pltpu.VMEM((FFN_TM, D_MODEL), BF16)],
        compiler_params=_params("arbitrary"),
        name="ffn_head",
    )(x, gain, w_gate, w_up, w_down)

    tf = FFN_TF
    wcol = pl.BlockSpec((tf // FFN_HEAD_TF, D_MODEL, FFN_HEAD_TF), lambda i, j: (jnp.where(i == 0, 0, j), 0, 0))
    wrow = lambda i, j: (jnp.where(i == 0, 0, j), 0)
    row = pl.BlockSpec((FFN_TM, D_MODEL), lambda i, j: (i, 0))
    n_steps = D_FF // tf
    side_specs, side_windows, start = [], [], 0
    for w in side:
        n_side_tiles = w.shape[1] // LANES
        side_specs.append(pl.BlockSpec(
            (w.shape[0], LANES),
            lambda i, j, start=start, last=n_side_tiles - 1: (0, jnp.clip(i * n_steps + j - start, 0, last))))
        side_windows.append((start, n_side_tiles))
        start += n_side_tiles
    assert start <= n_tiles * n_steps
    outs = pl.pallas_call(
        functools.partial(_ffn_tail_body, side_windows=tuple(side_windows)),
        grid=(n_tiles, n_steps),
        in_specs=[row, vec, wcol, wcol,
                  pl.BlockSpec((tf, D_MODEL), wrow), pl.BlockSpec(memory_space=pl.ANY), *side_specs],
        out_specs=(row, *side_specs),
        out_shape=(jax.ShapeDtypeStruct((tokens, D_MODEL), F32),
                   *(jax.ShapeDtypeStruct(w.shape, BF16) for w in side)),
        scratch_shapes=[pltpu.VMEM((FFN_TM, D_MODEL), BF16)],
        compiler_params=_params("arbitrary", "arbitrary"),
        name="ffn_tail",
    )(x, gain, wg_bf, wu_bf, wd_bf, head, *side)
    return outs if side else outs[0]


def _qkv_body(x_ref, gx_ref, w_ref, gq_ref, gk_ref, qa_ref, ka_ref, v_ref, km_scr, *, tiles_per_batch):
    tm = PROJ_TM
    blocks_per_batch = tiles_per_batch * (tm // MOBA_BLOCK)
    i = pl.program_id(0)
    n0 = (i % tiles_per_batch) * (tm // MOBA_BLOCK)

    @pl.when(i == 0)
    def _init():
        km_scr[...] = jnp.zeros_like(km_scr)

    h = _rmsnorm(x_ref[...], gx_ref[...]).astype(BF16)
    q = _dot(h, w_ref[:, 0:D_ATTN])
    k = _dot(h, w_ref[:, D_ATTN:2 * D_ATTN])
    v_ref[...] = _dot(h, w_ref[:, 2 * D_ATTN:3 * D_ATTN]).astype(BF16)

    scale = HEAD_DIM ** -0.5
    blk_rows = lax.broadcasted_iota(jnp.int32, (blocks_per_batch, HEAD_DIM), 0)
    n_iota = lax.broadcasted_iota(jnp.int32, (blocks_per_batch, tm), 0)
    lane_tok = lax.broadcasted_iota(jnp.int32, (blocks_per_batch, tm), 1)
    q_blk = n0 + lane_tok // MOBA_BLOCK
    valid = n_iota < q_blk
    tok_blk = n0 + lax.broadcasted_iota(jnp.int32, (tm, LANES), 0) // MOBA_BLOCK
    onehot = jnp.where(lax.broadcasted_iota(jnp.int32, (tm, LANES), 1) == tok_blk, 1.0, 0.0).astype(BF16)
    pad_rows = jnp.zeros((LANES - blocks_per_batch, tm), F32)

    for hh in range(ATTN_HEADS):
        sl = slice(hh * HEAD_DIM, (hh + 1) * HEAD_DIM)
        qn = _rmsnorm(q[:, sl], gq_ref[...])
        kn = _rmsnorm(k[:, sl], gk_ref[...])

        kmean = km_scr[:, sl]
        for b in range(tm // MOBA_BLOCK):
            km = jnp.mean(kn[b * MOBA_BLOCK:(b + 1) * MOBA_BLOCK], axis=0, keepdims=True)
            kmean = jnp.where(blk_rows == n0 + b, km, kmean)
        km_scr[:, sl] = kmean

        gate = _dot_nt(kmean, qn, precision=lax.Precision.HIGHEST)
        cnt = jnp.zeros((blocks_per_batch, tm), jnp.int32)
        for m in range(blocks_per_batch - 1):
            gm = gate[m:m + 1, :]
            beats = (gm > gate) | ((gm == gate) & (m < n_iota))
            cnt = cnt + jnp.where(beats & (m < q_blk), 1, 0)
        sel = ((cnt < MOBA_TOPK) & valid) | (n_iota == q_blk)
        bias_t = jnp.concatenate([jnp.where(sel, 0.0, NEG_INF), pad_rows], axis=0)
        bias = bias_t.T

        a0 = hh * AUG_DIM
        qa_ref[:, a0:a0 + HEAD_DIM] = (qn * scale).astype(BF16)
        qa_ref[:, a0 + HEAD_DIM:a0 + AUG_DIM] = bias.astype(BF16)
        ka_ref[:, a0:a0 + HEAD_DIM] = kn.astype(BF16)
        ka_ref[:, a0 + HEAD_DIM:a0 + AUG_DIM] = onehot


def _qkv(x, x_gain, w_in, q_gain, k_gain, seq):
    tokens = x.shape[0]
    tm = PROJ_TM
    tiles_per_batch = seq // tm
    blocks_per_batch = seq // MOBA_BLOCK
    assert blocks_per_batch <= LANES and blocks_per_batch % 8 == 0
    row = lambda width: pl.BlockSpec((tm, width), lambda i: (i, 0))
    return pl.pallas_call(
        functools.partial(_qkv_body, tiles_per_batch=tiles_per_batch),
        grid=(tokens // tm,),
        in_specs=[row(D_MODEL), _resident((1, D_MODEL)), _resident((D_MODEL, 3 * D_ATTN)),
                  _resident((1, HEAD_DIM)), _resident((1, HEAD_DIM))],
        out_specs=(row(ATTN_HEADS * AUG_DIM), row(ATTN_HEADS * AUG_DIM), row(D_ATTN)),
        out_shape=(jax.ShapeDtypeStruct((tokens, ATTN_HEADS * AUG_DIM), BF16),
                   jax.ShapeDtypeStruct((tokens, ATTN_HEADS * AUG_DIM), BF16),
                   jax.ShapeDtypeStruct((tokens, D_ATTN), BF16)),
        scratch_shapes=[pltpu.VMEM((blocks_per_batch, D_ATTN), F32)],
        compiler_params=_params("arbitrary"),
        name="qkv_gate",
    )(x, x_gain, w_in, q_gain, k_gain)


def _sgu_body(x_ref, gx_ref, wu_ref, wg_ref, lng_ref, lnb_ref, ws_ref, bs_ref, og_ref, o_ref, sg_scr):
    tm = PROJ_TM
    h = _rmsnorm(x_ref[...], gx_ref[...]).astype(BF16)
    u = _gelu_tanh(_dot(h, wu_ref[...]))
    g = _gelu_tanh(_dot(h, wg_ref[...]))
    r = lax.broadcasted_iota(jnp.int32, (SGU_CHUNK, SGU_CHUNK), 0)
    c = lax.broadcasted_iota(jnp.int32, (SGU_CHUNK, SGU_CHUNK), 1)
    for gi in range(SGU_GROUPS):
        sl = slice(gi * SGU_GROUP_DIM, (gi + 1) * SGU_GROUP_DIM)
        gg = g[:, sl]
        d = gg - jnp.mean(gg, axis=-1, keepdims=True)
        var = jnp.mean(d * d, axis=-1, keepdims=True)
        gn = (d * lax.rsqrt(var + EPS) * lng_ref[:, sl] + lnb_ref[:, sl]).astype(BF16)
        w_causal = jnp.where(r >= c, ws_ref[gi], 0.0).astype(BF16)
        b_col = bs_ref[:, gi:gi + 1]
        for ci in range(tm // SGU_CHUNK):
            rows = slice(ci * SGU_CHUNK, (ci + 1) * SGU_CHUNK)
            mixed = _dot(w_causal, gn[rows]) + b_col
            sg_scr[rows, sl] = u[rows, sl] * mixed
    o_ref[...] = _rmsnorm(sg_scr[...], og_ref[...]).astype(BF16)


def _sgu(x, x_gain, w_in, ln_gain, ln_bias, w_spatial, b_spatial_t, out_gain):
    tokens = x.shape[0]
    tm = PROJ_TM
    w_block = lambda c: pl.BlockSpec((D_MODEL, D_SGU), lambda i: (0, c), pipeline_mode=pl.Buffered(1))
    first = 3 * D_ATTN // D_SGU
    return pl.pallas_call(
        _sgu_body,
        grid=(tokens // tm,),
        in_specs=[pl.BlockSpec((tm, D_MODEL), lambda i: (i, 0)), _resident((1, D_MODEL)),
                  w_block(first), w_block(first + 1),
                  _resident((1, D_SGU)), _resident((1, D_SGU)),
                  _resident((SGU_GROUPS, SGU_CHUNK, SGU_CHUNK)),
                  _resident((SGU_CHUNK, SGU_GROUPS)),
                  _resident((1, D_SGU))],
        out_specs=pl.BlockSpec((tm, D_SGU), lambda i: (i, 0)),
        out_shape=jax.ShapeDtypeStruct((tokens, D_SGU), BF16),
        scratch_shapes=[pltpu.VMEM((tm, D_SGU), F32)],
        compiler_params=_params("parallel"),
        name="sgu",
    )(x, x_gain, w_in, w_in, ln_gain, ln_bias, w_spatial, b_spatial_t, out_gain)


def _attn_scores(t, q_ref, k_ref, s_ref, m_ref):
    tq, kc = ATTN_TQ, ATTN_KV_CHUNK
    m_part = None
    for c in range(t + 1):
        s = _dot_nt(q_ref[...], k_ref[c * kc:(c + 1) * kc, :])
        if c == t:
            r = lax.broadcasted_iota(jnp.int32, (tq, kc), 0)
            col = lax.broadcasted_iota(jnp.int32, (tq, kc), 1)
            s = jnp.where(col <= r, s, NEG_INF)
        s_ref[:, c * kc:(c + 1) * kc] = s
        for j in range(kc // LANES):
            sj = s[:, j * LANES:(j + 1) * LANES]
            m_part = sj if m_part is None else jnp.maximum(m_part, sj)
    m_ref[...] = jnp.broadcast_to(jnp.max(m_part, axis=-1, keepdims=True), (tq, LANES))


def _attn_values(t, va_ref, o_ref, s_ref, m_ref):
    tq, kc = ATTN_TQ, ATTN_KV_CHUNK
    m = m_ref[...]
    acc = jnp.zeros((tq, AUG_DIM), F32)
    for c in range(t + 1):
        p = jnp.concatenate(
            [jnp.exp(s_ref[:, c * kc + j * LANES:c * kc + (j + 1) * LANES] - m).astype(BF16)
             for j in range(kc // LANES)], axis=1)
        acc = acc + _dot(p, va_ref[c * kc:(c + 1) * kc, :])
    o_ref[...] = acc[:, 0:HEAD_DIM] / acc[:, HEAD_DIM:AUG_DIM]


def _attn_body(q_ref, k_ref, v_ref, o_ref, s_scr, m_scr, va_scr):
    n_tiles = k_ref.shape[0] // ATTN_TQ
    tile = lambda t: slice(t * ATTN_TQ, (t + 1) * ATTN_TQ)
    va_scr[:, 0:HEAD_DIM] = v_ref[...]
    va_scr[:, HEAD_DIM:AUG_DIM] = jnp.ones((va_scr.shape[0], HEAD_DIM), BF16)

    def sub_step(ts):
        if ts >= 1:
            _attn_values(ts - 1, va_scr, o_ref.at[tile(ts - 1), :], s_scr.at[(ts - 1) % 2], m_scr.at[(ts - 1) % 2])
        if ts < n_tiles:
            _attn_scores(ts, q_ref.at[tile(ts), :], k_ref, s_scr.at[ts % 2], m_scr.at[ts % 2])

    for ts in range(n_tiles + 1):
        pl.when(pl.program_id(1) + ts >= 0)(functools.partial(sub_step, ts))


def _attention(qa, ka, v, batch, seq):
    tokens = batch * seq
    return pl.pallas_call(
        _attn_body,
        grid=(batch, ATTN_HEADS),
        in_specs=[pl.BlockSpec((seq, AUG_DIM), lambda b, h: (b, h)),
                  pl.BlockSpec((seq, AUG_DIM), lambda b, h: (b, h)),
                  pl.BlockSpec((seq, HEAD_DIM), lambda b, h: (b, h))],
        out_specs=pl.BlockSpec((seq, HEAD_DIM), lambda b, h: (b, h)),
        out_shape=jax.ShapeDtypeStruct((tokens, D_ATTN), F32),
        scratch_shapes=[pltpu.VMEM((2, ATTN_TQ, seq), F32), pltpu.VMEM((2, ATTN_TQ, LANES), F32),
                        pltpu.VMEM((seq, AUG_DIM), BF16)],
        compiler_params=_params("parallel", "parallel"),
        name="moba_attn",
    )(qa, ka, v)


def _outproj_body(a_ref, s_ref, x_ref, ga_ref, w_ref, o_ref):
    an = _rmsnorm(a_ref[...], ga_ref[...]).astype(BF16)
    o_ref[...] = (x_ref[...] + _dot(an, w_ref[0:D_ATTN, :])
                  + _dot(s_ref[...], w_ref[D_ATTN:D_ATTN + D_SGU, :]))


def _outproj(attn, sgu, x, attn_gain, w_out):
    tokens = x.shape[0]
    tm = PROJ_TM
    row = lambda width: pl.BlockSpec((tm, width), lambda i: (i, 0))
    return pl.pallas_call(
        _outproj_body,
        grid=(tokens // tm,),
        in_specs=[row(D_ATTN), row(D_SGU), row(D_MODEL), _resident((1, D_ATTN)),
                  _resident((D_ATTN + D_SGU, D_MODEL))],
        out_specs=row(D_MODEL),
        out_shape=jax.ShapeDtypeStruct((tokens, D_MODEL), F32),
        compiler_params=_params("parallel"),
        name="outproj",
    )(attn, sgu, x, attn_gain, w_out)


def kernel(x, ffn1_norm, ffn1_w_gate, ffn1_w_up, ffn1_w_down, mix_norm, w_in, q_norm, k_norm, sgu_ln_gain, sgu_ln_bias, sgu_w_spatial, sgu_b_spatial, attn_out_gain, sgu_out_gain, w_out, ffn2_norm, ffn2_w_gate, ffn2_w_up, ffn2_w_down):
    batch, seq, d_model = x.shape
    depth = ffn1_norm.shape[0]
    assert d_model == D_MODEL and seq % PROJ_TM == 0 and (batch * seq) % FFN_TM == 0
    xt = x.reshape(batch * seq, d_model)
    vec = lambda a: a.reshape(1, -1)
    for l in range(depth):
        xt, w_in_bf, w_out_bf = _ffn(xt, vec(ffn1_norm[l]), ffn1_w_gate[l], ffn1_w_up[l], ffn1_w_down[l],
                                     side=(w_in[l], w_out[l]))
        qa, ka, v = _qkv(xt, vec(mix_norm[l]), w_in_bf, vec(q_norm[l]), vec(k_norm[l]), seq)
        sgu = _sgu(xt, vec(mix_norm[l]), w_in_bf, vec(sgu_ln_gain[l]), vec(sgu_ln_bias[l]),
                   sgu_w_spatial[l], sgu_b_spatial[l].T, vec(sgu_out_gain[l]))
        attn = _attention(qa, ka, v, batch, seq)
        xt = _outproj(attn, sgu, xt, vec(attn_out_gain[l]), w_out_bf)
        xt = _ffn(xt, vec(ffn2_norm[l]), ffn2_w_gate[l], ffn2_w_up[l], ffn2_w_down[l])
    return xt.reshape(batch, seq, d_model)
```

```python
import functools

import jax
import jax.numpy as jnp
from jax import lax
from jax.experimental import pallas as pl
from jax.experimental.pallas import tpu as pltpu

D_MODEL = 2048
D_FF = 5632
ATTN_HEADS = 8
HEAD_DIM = 128
D_ATTN = ATTN_HEADS * HEAD_DIM
SGU_GROUPS = 8
SGU_GROUP_DIM = 128
D_SGU = SGU_GROUPS * SGU_GROUP_DIM
MOBA_BLOCK = 256
MOBA_TOPK = 3
SGU_CHUNK = 128
EPS = 1e-6
NEG_INF = -1e30

LANES = 128
AUG_DIM = 2 * HEAD_DIM
VMEM_LIMIT_BYTES = 60 * 1024 * 1024

FFN_TM = 1024
FFN_HEAD_TF = 256
FFN_TF = 512
PROJ_TM = 512
ATTN_TQ = 512
ATTN_KV_CHUNK = 512

F32 = jnp.float32
BF16 = jnp.bfloat16


def _rmsnorm(x, gain):
    return x * lax.rsqrt(jnp.mean(x * x, axis=-1, keepdims=True) + EPS) * gain


def _gelu_tanh(x):
    cdf = 0.5 * (1.0 + jnp.tanh(0.7978845608028654 * (x + 0.044715 * (x * x * x))))
    return x * cdf


def _dot(a, b):
    return jnp.dot(a, b, preferred_element_type=F32)


def _dot_nt(a, b, precision=None):
    return lax.dot_general(a, b, (((1,), (1,)), ((), ())), precision=precision,
                           preferred_element_type=F32)


def _params(*semantics):
    return pltpu.CompilerParams(dimension_semantics=semantics, vmem_limit_bytes=VMEM_LIMIT_BYTES)


def _resident(shape):
    return pl.BlockSpec(shape, lambda *_: (0,) * len(shape), pipeline_mode=pl.Buffered(1))


def _ffn_accumulate(h_scr, wg, wu, wd, base_ref, o_ref):
    h = h_scr[...]
    gate = jnp.concatenate([_dot(h, wg[b]) for b in range(wg.shape[0])], axis=1)
    up = jnp.concatenate([_dot(h, wu[b]) for b in range(wu.shape[0])], axis=1)
    act = (gate / (1.0 + jnp.exp(-gate))) * up * 0.5
    o_ref[...] = base_ref[...] + _dot(act.astype(BF16), wd)


def _ffn_step(first, x_ref, g_ref, h_scr, wg, wu, wd, o_ref):
    if first:
        h_scr[...] = _rmsnorm(x_ref[...], g_ref[...]).astype(BF16)
    _ffn_accumulate(h_scr, wg, wu, wd, x_ref if first else o_ref, o_ref)


def _ffn_head_body(x_ref, g_ref, wg_ref, wu_ref, wd_ref, o_ref, wg_bf_ref, wu_bf_ref, wd_bf_ref, h_scr):
    @pl.when(pl.program_id(0) == 0)
    def _():
        x = x_ref[...]
        h_scr[...] = _rmsnorm(x, g_ref[...]).astype(BF16)
        o_ref[...] = x

    wg_bf_ref[0] = wg_ref[...].astype(BF16)
    wu_bf_ref[0] = wu_ref[...].astype(BF16)
    wd_bf_ref[...] = wd_ref[...].astype(BF16)
    _ffn_accumulate(h_scr, wg_bf_ref, wu_bf_ref, wd_bf_ref[...], o_ref, o_ref)


def _ffn_tail_body(x_ref, g_ref, wg_ref, wu_ref, wd_ref, head_hbm, *rest, side_windows):
    n_side = len(side_windows)
    side_in, (o_ref, *side_out), h_scr = rest[:n_side], rest[n_side:-1], rest[-1]
    i = pl.program_id(0)
    j = pl.program_id(1)

    step = i * pl.num_programs(1) + j
    for src, dst, (start, n) in zip(side_in, side_out, side_windows):
        @pl.when((step >= start) & (step < start + n))
        def _(src=src, dst=dst):
            dst[...] = src[...].astype(BF16)

    @pl.when((i == 0) & (j == 0))
    def _():
        pltpu.sync_copy(head_hbm, o_ref)

    for first in (True, False):
        @pl.when((i > 0) & ((j == 0) if first else (j > 0)))
        def _(first=first):
            _ffn_step(first, x_ref, g_ref, h_scr, wg_ref, wu_ref, wd_ref[...], o_ref)


def _ffn(x, gain, w_gate, w_up, w_down, side=()):
    tokens = x.shape[0]
    n_tiles = tokens // FFN_TM
    vec = pl.BlockSpec((1, D_MODEL), lambda *_: (0, 0))

    tf = FFN_HEAD_TF
    col = pl.BlockSpec((D_MODEL, tf), lambda j: (0, j))
    colblk = pl.BlockSpec((1, D_MODEL, tf), lambda j: (j, 0, 0))
    rowblk = pl.BlockSpec((tf, D_MODEL), lambda j: (j, 0))
    tile0 = pl.BlockSpec((FFN_TM, D_MODEL), lambda j: (0, 0))
    head, wg_bf, wu_bf, wd_bf = pl.pallas_call(
        _ffn_head_body,
        grid=(D_FF // tf,),
        in_specs=[tile0, vec, col, col, rowblk],
        out_specs=(tile0, colblk, colblk, rowblk),
        out_shape=(jax.ShapeDtypeStruct((FFN_TM, D_MODEL), F32),
                   jax.ShapeDtypeStruct((D_FF // tf, D_MODEL, tf), BF16),
                   jax.ShapeDtypeStruct((D_FF // tf, D_MODEL, tf), BF16),
                   jax.ShapeDtypeStruct((D_FF, D_MODEL), BF16)),
        scratch_shapes=[pltpu.VMEM((FFN_TM, D_MODEL), BF16)],
        compiler_params=_params("arbitrary"),
        name="ffn_head",
    )(x, gain, w_gate, w_up, w_down)

    tf = FFN_TF
    wcol = pl.BlockSpec((tf // FFN_HEAD_TF, D_MODEL, FFN_HEAD_TF), lambda i, j: (jnp.where(i == 0, 0, j), 0, 0))
    wrow = lambda i, j: (jnp.where(i == 0, 0, j), 0)
    row = pl.BlockSpec((FFN_TM, D_MODEL), lambda i, j: (i, 0))
    n_steps = D_FF // tf
    side_specs, side_windows, start = [], [], 0
    for w in side:
        n_side_tiles = w.shape[1] // LANES
        side_specs.append(pl.BlockSpec(
            (w.shape[0], LANES),
            lambda i, j, start=start, last=n_side_tiles - 1: (0, jnp.clip(i * n_steps + j - start, 0, last))))
        side_windows.append((start, n_side_tiles))
        start += n_side_tiles
    assert start <= n_tiles * n_steps
    outs = pl.pallas_call(
        functools.partial(_ffn_tail_body, side_windows=tuple(side_windows)),
        grid=(n_tiles, n_steps),
        in_specs=[row, vec, wcol, wcol,
                  pl.BlockSpec((tf, D_MODEL), wrow), pl.BlockSpec(memory_space=pl.ANY), *side_specs],
        out_specs=(row, *side_specs),
        out_shape=(jax.ShapeDtypeStruct((tokens, D_MODEL), F32),
                   *(jax.ShapeDtypeStruct(w.shape, BF16) for w in side)),
        scratch_shapes=[pltpu.VMEM((FFN_TM, D_MODEL), BF16)],
        compiler_params=_params("arbitrary", "arbitrary"),
        name="ffn_tail",
    )(x, gain, wg_bf, wu_bf, wd_bf, head, *side)
    return outs if side else outs[0]


def _moba_gate(n0, q, k, gq_ref, gk_ref, qa_ref, ka_ref, km_scr):
    tm = PROJ_TM
    blocks_per_batch = km_scr.shape[0]
    scale = HEAD_DIM ** -0.5
    blk_rows = lax.broadcasted_iota(jnp.int32, (blocks_per_batch, HEAD_DIM), 0)
    n_iota = lax.broadcasted_iota(jnp.int32, (blocks_per_batch, tm), 0)
    lane_tok = lax.broadcasted_iota(jnp.int32, (blocks_per_batch, tm), 1)
    q_blk = n0 + lane_tok // MOBA_BLOCK
    valid = n_iota < q_blk
    tok_blk = n0 + lax.broadcasted_iota(jnp.int32, (tm, LANES), 0) // MOBA_BLOCK
    onehot = jnp.where(lax.broadcasted_iota(jnp.int32, (tm, LANES), 1) == tok_blk, 1.0, 0.0).astype(BF16)
    pad_rows = jnp.zeros((LANES - blocks_per_batch, tm), F32)

    for hh in range(ATTN_HEADS):
        sl = slice(hh * HEAD_DIM, (hh + 1) * HEAD_DIM)
        qn = _rmsnorm(q[:, sl], gq_ref[...])
        kn = _rmsnorm(k[:, sl], gk_ref[...])

        kmean = km_scr[:, sl]
        for b in range(tm // MOBA_BLOCK):
            km = jnp.mean(kn[b * MOBA_BLOCK:(b + 1) * MOBA_BLOCK], axis=0, keepdims=True)
            kmean = jnp.where(blk_rows == n0 + b, km, kmean)
        km_scr[:, sl] = kmean

        gate = _dot_nt(kmean, qn, precision=lax.Precision.HIGHEST)
        cnt = jnp.zeros((blocks_per_batch, tm), jnp.int32)
        for m in range(blocks_per_batch - 1):
            gm = gate[m:m + 1, :]
            beats = (gm > gate) | ((gm == gate) & (m < n_iota))
            cnt = cnt + jnp.where(beats & (m < q_blk), 1, 0)
        sel = ((cnt < MOBA_TOPK) & valid) | (n_iota == q_blk)
        bias_t = jnp.concatenate([jnp.where(sel, 0.0, NEG_INF), pad_rows], axis=0)
        bias = bias_t.T

        a0 = hh * AUG_DIM
        qa_ref[:, a0:a0 + HEAD_DIM] = (qn * scale).astype(BF16)
        qa_ref[:, a0 + HEAD_DIM:a0 + AUG_DIM] = bias.astype(BF16)
        ka_ref[:, a0:a0 + HEAD_DIM] = kn.astype(BF16)
        ka_ref[:, a0 + HEAD_DIM:a0 + AUG_DIM] = onehot


def _spatial_gate(u, g, lng_ref, lnb_ref, ws_ref, bs_ref, og_ref, o_ref, sg_scr):
    tm = PROJ_TM
    r = lax.broadcasted_iota(jnp.int32, (SGU_CHUNK, SGU_CHUNK), 0)
    c = lax.broadcasted_iota(jnp.int32, (SGU_CHUNK, SGU_CHUNK), 1)
    for gi in range(SGU_GROUPS):
        sl = slice(gi * SGU_GROUP_DIM, (gi + 1) * SGU_GROUP_DIM)
        gg = g[:, sl]
        d = gg - jnp.mean(gg, axis=-1, keepdims=True)
        var = jnp.mean(d * d, axis=-1, keepdims=True)
        gn = (d * lax.rsqrt(var + EPS) * lng_ref[:, sl] + lnb_ref[:, sl]).astype(BF16)
        w_causal = jnp.where(r >= c, ws_ref[gi], 0.0).astype(BF16)
        b_col = bs_ref[:, gi:gi + 1]
        for ci in range(tm // SGU_CHUNK):
            rows = slice(ci * SGU_CHUNK, (ci + 1) * SGU_CHUNK)
            mixed = _dot(w_causal, gn[rows]) + b_col
            sg_scr[rows, sl] = u[rows, sl] * mixed
    o_ref[...] = _rmsnorm(sg_scr[...], og_ref[...]).astype(BF16)


def _mix_body(x_ref, gx_ref, w_ref, gq_ref, gk_ref, lng_ref, lnb_ref, ws_ref, bs_ref, og_ref,
              qa_ref, ka_ref, v_ref, o_ref, km_scr, sg_scr, *, tiles_per_batch):
    i = pl.program_id(0)
    n0 = (i % tiles_per_batch) * (PROJ_TM // MOBA_BLOCK)

    @pl.when(i == 0)
    def _init():
        km_scr[...] = jnp.zeros_like(km_scr)

    h = _rmsnorm(x_ref[...], gx_ref[...]).astype(BF16)
    col = lambda c0, width: w_ref[:, c0:c0 + width]
    q = _dot(h, col(0, D_ATTN))
    k = _dot(h, col(D_ATTN, D_ATTN))
    v_ref[...] = _dot(h, col(2 * D_ATTN, D_ATTN)).astype(BF16)
    _moba_gate(n0, q, k, gq_ref, gk_ref, qa_ref, ka_ref, km_scr)
    u = _gelu_tanh(_dot(h, col(3 * D_ATTN, D_SGU)))
    g = _gelu_tanh(_dot(h, col(3 * D_ATTN + D_SGU, D_SGU)))
    _spatial_gate(u, g, lng_ref, lnb_ref, ws_ref, bs_ref, og_ref, o_ref, sg_scr)


def _mix(x, x_gain, w_in, q_gain, k_gain, ln_gain, ln_bias, w_spatial, b_spatial_t, out_gain, seq):
    tokens = x.shape[0]
    tm = PROJ_TM
    tiles_per_batch = seq // tm
    blocks_per_batch = seq // MOBA_BLOCK
    assert blocks_per_batch <= LANES and blocks_per_batch % 8 == 0
    row = lambda width: pl.BlockSpec((tm, width), lambda i: (i, 0))
    return pl.pallas_call(
        functools.partial(_mix_body, tiles_per_batch=tiles_per_batch),
        grid=(tokens // tm,),
        in_specs=[row(D_MODEL), _resident((1, D_MODEL)), _resident((D_MODEL, 3 * D_ATTN + 2 * D_SGU)),
                  _resident((1, HEAD_DIM)), _resident((1, HEAD_DIM)),
                  _resident((1, D_SGU)), _resident((1, D_SGU)),
                  _resident((SGU_GROUPS, SGU_CHUNK, SGU_CHUNK)), _resident((SGU_CHUNK, SGU_GROUPS)),
                  _resident((1, D_SGU))],
        out_specs=(row(ATTN_HEADS * AUG_DIM), row(ATTN_HEADS * AUG_DIM), row(D_ATTN), row(D_SGU)),
        out_shape=(jax.ShapeDtypeStruct((tokens, ATTN_HEADS * AUG_DIM), BF16),
                   jax.ShapeDtypeStruct((tokens, ATTN_HEADS * AUG_DIM), BF16),
                   jax.ShapeDtypeStruct((tokens, D_ATTN), BF16),
                   jax.ShapeDtypeStruct((tokens, D_SGU), BF16)),
        scratch_shapes=[pltpu.VMEM((blocks_per_batch, D_ATTN), F32), pltpu.VMEM((tm, D_SGU), F32)],
        compiler_params=_params("arbitrary"),
        name="mix_proj",
    )(x, x_gain, w_in, q_gain, k_gain, ln_gain, ln_bias, w_spatial, b_spatial_t, out_gain)


def _attn_scores(t, q_ref, k_ref, s_ref, m_ref):
    tq, kc = ATTN_TQ, ATTN_KV_CHUNK
    m_part = None
    for c in range(t + 1):
        s = _dot_nt(q_ref[...], k_ref[c * kc:(c + 1) * kc, :])
        if c == t:
            r = lax.broadcasted_iota(jnp.int32, (tq, kc), 0)
            col = lax.broadcasted_iota(jnp.int32, (tq, kc), 1)
            s = jnp.where(col <= r, s, NEG_INF)
        s_ref[:, c * kc:(c + 1) * kc] = s
        for j in range(kc // LANES):
            sj = s[:, j * LANES:(j + 1) * LANES]
            m_part = sj if m_part is None else jnp.maximum(m_part, sj)
    m_ref[...] = jnp.broadcast_to(jnp.max(m_part, axis=-1, keepdims=True), (tq, LANES))


def _attn_values(t, va_ref, o_ref, s_ref, m_ref):
    tq, kc = ATTN_TQ, ATTN_KV_CHUNK
    m = m_ref[...]
    acc = jnp.zeros((tq, AUG_DIM), F32)
    for c in range(t + 1):
        p = jnp.concatenate(
            [jnp.exp(s_ref[:, c * kc + j * LANES:c * kc + (j + 1) * LANES] - m).astype(BF16)
             for j in range(kc // LANES)], axis=1)
        acc = acc + _dot(p, va_ref[c * kc:(c + 1) * kc, :])
    o_ref[...] = acc[:, 0:HEAD_DIM] / acc[:, HEAD_DIM:AUG_DIM]


def _attn_body(q_ref, k_ref, v_ref, o_ref, s_scr, m_scr, va_scr):
    n_tiles = k_ref.shape[0] // ATTN_TQ
    tile = lambda t: slice(t * ATTN_TQ, (t + 1) * ATTN_TQ)
    va_scr[:, 0:HEAD_DIM] = v_ref[...]
    va_scr[:, HEAD_DIM:AUG_DIM] = jnp.ones((va_scr.shape[0], HEAD_DIM), BF16)

    def sub_step(ts):
        if ts >= 1:
            _attn_values(ts - 1, va_scr, o_ref.at[tile(ts - 1), :], s_scr.at[(ts - 1) % 2], m_scr.at[(ts - 1) % 2])
        if ts < n_tiles:
            _attn_scores(ts, q_ref.at[tile(ts), :], k_ref, s_scr.at[ts % 2], m_scr.at[ts % 2])

    for ts in range(n_tiles + 1):
        pl.when(pl.program_id(1) + ts >= 0)(functools.partial(sub_step, ts))


def _attention(qa, ka, v, batch, seq):
    tokens = batch * seq
    return pl.pallas_call(
        _attn_body,
        grid=(batch, ATTN_HEADS),
        in_specs=[pl.BlockSpec((seq, AUG_DIM), lambda b, h: (b, h)),
                  pl.BlockSpec((seq, AUG_DIM), lambda b, h: (b, h)),
                  pl.BlockSpec((seq, HEAD_DIM), lambda b, h: (b, h))],
        out_specs=pl.BlockSpec((seq, HEAD_DIM), lambda b, h: (b, h)),
        out_shape=jax.ShapeDtypeStruct((tokens, D_ATTN), F32),
        scratch_shapes=[pltpu.VMEM((2, ATTN_TQ, seq), F32), pltpu.VMEM((2, ATTN_TQ, LANES), F32),
                        pltpu.VMEM((seq, AUG_DIM), BF16)],
        compiler_params=_params("parallel", "parallel"),
        name="moba_attn",
    )(qa, ka, v)


def _outproj_body(a_ref, s_ref, x_ref, ga_ref, w_ref, o_ref):
    an = _rmsnorm(a_ref[...], ga_ref[...]).astype(BF16)
    o_ref[...] = (x_ref[...] + _dot(an, w_ref[0:D_ATTN, :])
                  + _dot(s_ref[...], w_ref[D_ATTN:D_ATTN + D_SGU, :]))


def _outproj(attn, sgu, x, attn_gain, w_out):
    tokens = x.shape[0]
    tm = PROJ_TM
    row = lambda width: pl.BlockSpec((tm, width), lambda i: (i, 0))
    return pl.pallas_call(
        _outproj_body,
        grid=(tokens // tm,),
        in_specs=[row(D_ATTN), row(D_SGU), row(D_MODEL), _resident((1, D_ATTN)),
                  _resident((D_ATTN + D_SGU, D_MODEL))],
        out_specs=row(D_MODEL),
        out_shape=jax.ShapeDtypeStruct((tokens, D_MODEL), F32),
        compiler_params=_params("parallel"),
        name="outproj",
    )(attn, sgu, x, attn_gain, w_out)


def kernel(x, ffn1_norm, ffn1_w_gate, ffn1_w_up, ffn1_w_down, mix_norm, w_in, q_norm, k_norm, sgu_ln_gain, sgu_ln_bias, sgu_w_spatial, sgu_b_spatial, attn_out_gain, sgu_out_gain, w_out, ffn2_norm, ffn2_w_gate, ffn2_w_up, ffn2_w_down):
    batch, seq, d_model = x.shape
    depth = ffn1_norm.shape[0]
    assert d_model == D_MODEL and seq % PROJ_TM == 0 and (batch * seq) % FFN_TM == 0
    xt = x.reshape(batch * seq, d_model)
    vec = lambda a: a.reshape(1, -1)
    for l in range(depth):
        xt, w_in_bf, w_out_bf = _ffn(xt, vec(ffn1_norm[l]), ffn1_w_gate[l], ffn1_w_up[l], ffn1_w_down[l],
                                     side=(w_in[l], w_out[l]))
        qa, ka, v, sgu = _mix(xt, vec(mix_norm[l]), w_in_bf, vec(q_norm[l]), vec(k_norm[l]),
                              vec(sgu_ln_gain[l]), vec(sgu_ln_bias[l]), sgu_w_spatial[l], sgu_b_spatial[l].T,
                              vec(sgu_out_gain[l]), seq)
        attn = _attention(qa, ka, v, batch, seq)
        xt = _outproj(attn, sgu, xt, vec(attn_out_gain[l]), w_out_bf)
        xt = _ffn(xt, vec(ffn2_norm[l]), ffn2_w_gate[l], ffn2_w_up[l], ffn2_w_down[l])
    return xt.reshape(batch, seq, d_model)
```

```python
import functools

import jax
import jax.numpy as jnp
from jax import lax
from jax.experimental import pallas as pl
from jax.experimental.pallas import tpu as pltpu

D_MODEL = 2048
D_FF = 5632
ATTN_HEADS = 8
HEAD_DIM = 128
D_ATTN = ATTN_HEADS * HEAD_DIM
SGU_GROUPS = 8
SGU_GROUP_DIM = 128
D_SGU = SGU_GROUPS * SGU_GROUP_DIM
MOBA_BLOCK = 256
MOBA_TOPK = 3
SGU_CHUNK = 128
EPS = 1e-6
NEG_INF = -1e30

LANES = 128
AUG_DIM = 2 * HEAD_DIM
VMEM_LIMIT_BYTES = 60 * 1024 * 1024

FFN_TM = 1024
FFN_HEAD_TF = 256
FFN_TF = 512
PROJ_TM = 512
ATTN_TQ = 512
ATTN_KV_CHUNK = 512

F32 = jnp.float32
BF16 = jnp.bfloat16


def _rmsnorm(x, gain):
    return x * lax.rsqrt(jnp.mean(x * x, axis=-1, keepdims=True) + EPS) * gain


def _gelu_tanh(x):
    cdf = 0.5 * (1.0 + jnp.tanh(0.7978845608028654 * (x + 0.044715 * (x * x * x))))
    return x * cdf


def _dot(a, b):
    return jnp.dot(a, b, preferred_element_type=F32)


def _dot_nt(a, b, precision=None):
    return lax.dot_general(a, b, (((1,), (1,)), ((), ())), precision=precision,
                           preferred_element_type=F32)


def _dot_nt_3x(a, b):
    a_hi, b_hi = a.astype(BF16), b.astype(BF16)
    a_lo = (a - a_hi.astype(F32)).astype(BF16)
    b_lo = (b - b_hi.astype(F32)).astype(BF16)
    return _dot_nt(a_hi, b_hi) + (_dot_nt(a_hi, b_lo) + _dot_nt(a_lo, b_hi))


def _params(*semantics):
    return pltpu.CompilerParams(dimension_semantics=semantics, vmem_limit_bytes=VMEM_LIMIT_BYTES)


def _resident(shape):
    return pl.BlockSpec(shape, lambda *_: (0,) * len(shape), pipeline_mode=pl.Buffered(1))


def _ffn_accumulate(h_scr, wg, wu, wd, base_ref, o_ref):
    h = h_scr[...]
    acts = []
    for b in range(wg.shape[0]):
        gate = _dot(h, wg[b])
        up = _dot(h, wu[b])
        acts.append(((gate / (1.0 + jnp.exp(-gate))) * up * 0.5).astype(BF16))
    o_ref[...] = base_ref[...] + _dot(jnp.concatenate(acts, axis=1), wd)


def _ffn_step(first, x_ref, g_ref, h_scr, wg, wu, wd, o_ref):
    if first:
        h_scr[...] = _rmsnorm(x_ref[...], g_ref[...]).astype(BF16)
    _ffn_accumulate(h_scr, wg, wu, wd, x_ref if first else o_ref, o_ref)


def _ffn_head_body(x_ref, g_ref, wg_ref, wu_ref, wd_ref, o_ref, wg_bf_ref, wu_bf_ref, wd_bf_ref, h_scr):
    @pl.when(pl.program_id(0) == 0)
    def _():
        x = x_ref[...]
        h_scr[...] = _rmsnorm(x, g_ref[...]).astype(BF16)
        o_ref[...] = x

    wg_bf_ref[0] = wg_ref[...].astype(BF16)
    wu_bf_ref[0] = wu_ref[...].astype(BF16)
    wd_bf_ref[...] = wd_ref[...].astype(BF16)
    _ffn_accumulate(h_scr, wg_bf_ref, wu_bf_ref, wd_bf_ref[...], o_ref, o_ref)


def _ffn_tail_body(x_ref, g_ref, wg_ref, wu_ref, wd_ref, head_hbm, *rest, side_windows):
    n_side = len(side_windows)
    side_in, (o_ref, *side_out), h_scr = rest[:n_side], rest[n_side:-1], rest[-1]
    i = pl.program_id(0)
    j = pl.program_id(1)

    step = i * pl.num_programs(1) + j
    for src, dst, (start, n) in zip(side_in, side_out, side_windows):
        @pl.when((step >= start) & (step < start + n))
        def _(src=src, dst=dst):
            dst[...] = src[...].astype(BF16)

    @pl.when((i == 0) & (j == 0))
    def _():
        pltpu.sync_copy(head_hbm, o_ref)

    for first in (True, False):
        @pl.when((i > 0) & ((j == 0) if first else (j > 0)))
        def _(first=first):
            _ffn_step(first, x_ref, g_ref, h_scr, wg_ref, wu_ref, wd_ref[...], o_ref)


def _ffn(x, gain, w_gate, w_up, w_down, side=()):
    tokens = x.shape[0]
    n_tiles = tokens // FFN_TM
    vec = pl.BlockSpec((1, D_MODEL), lambda *_: (0, 0))

    tf = FFN_HEAD_TF
    col = pl.BlockSpec((D_MODEL, tf), lambda j: (0, j))
    colblk = pl.BlockSpec((1, D_MODEL, tf), lambda j: (j, 0, 0))
    rowblk = pl.BlockSpec((tf, D_MODEL), lambda j: (j, 0))
    tile0 = pl.BlockSpec((FFN_TM, D_MODEL), lambda j: (0, 0))
    head, wg_bf, wu_bf, wd_bf = pl.pallas_call(
        _ffn_head_body,
        grid=(D_FF // tf,),
        in_specs=[tile0, vec, col, col, rowblk],
        out_specs=(tile0, colblk, colblk, rowblk),
        out_shape=(jax.ShapeDtypeStruct((FFN_TM, D_MODEL), F32),
                   jax.ShapeDtypeStruct((D_FF // tf, D_MODEL, tf), BF16),
                   jax.ShapeDtypeStruct((D_FF // tf, D_MODEL, tf), BF16),
                   jax.ShapeDtypeStruct((D_FF, D_MODEL), BF16)),
        scratch_shapes=[pltpu.VMEM((FFN_TM, D_MODEL), BF16)],
        compiler_params=_params("arbitrary"),
        name="ffn_head",
    )(x, gain, w_gate, w_up, w_down)

    tf = FFN_TF
    wcol = pl.BlockSpec((tf // FFN_HEAD_TF, D_MODEL, FFN_HEAD_TF), lambda i, j: (jnp.where(i == 0, 0, j), 0, 0))
    wrow = lambda i, j: (jnp.where(i == 0, 0, j), 0)
    row = pl.BlockSpec((FFN_TM, D_MODEL), lambda i, j: (i, 0))
    n_steps = D_FF // tf
    side_specs, side_windows, start = [], [], 0
    for w in side:
        n_side_tiles = w.shape[1] // LANES
        side_specs.append(pl.BlockSpec(
            (w.shape[0], LANES),
            lambda i, j, start=start, last=n_side_tiles - 1: (0, jnp.clip(i * n_steps + j - start, 0, last))))
        side_windows.append((start, n_side_tiles))
        start += n_side_tiles
    assert start <= n_tiles * n_steps
    outs = pl.pallas_call(
        functools.partial(_ffn_tail_body, side_windows=tuple(side_windows)),
        grid=(n_tiles, n_steps),
        in_specs=[row, vec, wcol, wcol,
                  pl.BlockSpec((tf, D_MODEL), wrow), pl.BlockSpec(memory_space=pl.ANY), *side_specs],
        out_specs=(row, *side_specs),
        out_shape=(jax.ShapeDtypeStruct((tokens, D_MODEL), F32),
                   *(jax.ShapeDtypeStruct(w.shape, BF16) for w in side)),
        scratch_shapes=[pltpu.VMEM((FFN_TM, D_MODEL), BF16)],
        compiler_params=_params("arbitrary", "arbitrary"),
        name="ffn_tail",
    )(x, gain, wg_bf, wu_bf, wd_bf, head, *side)
    return outs if side else outs[0]


def _moba_gate(n0, q, k, gq_ref, gk_ref, qa_ref, ka_ref, km_scr):
    tm = PROJ_TM
    blocks_per_batch = km_scr.shape[0]
    scale = HEAD_DIM ** -0.5
    blk_rows = lax.broadcasted_iota(jnp.int32, (blocks_per_batch, HEAD_DIM), 0)
    n_iota = lax.broadcasted_iota(jnp.int32, (blocks_per_batch, tm), 0)
    lane_tok = lax.broadcasted_iota(jnp.int32, (blocks_per_batch, tm), 1)
    q_blk = n0 + lane_tok // MOBA_BLOCK
    valid = n_iota < q_blk
    tok_blk = n0 + lax.broadcasted_iota(jnp.int32, (tm, LANES), 0) // MOBA_BLOCK
    onehot = jnp.where(lax.broadcasted_iota(jnp.int32, (tm, LANES), 1) == tok_blk, 1.0, 0.0).astype(BF16)
    pad_rows = jnp.zeros((LANES - blocks_per_batch, tm), F32)

    for hh in range(ATTN_HEADS):
        sl = slice(hh * HEAD_DIM, (hh + 1) * HEAD_DIM)
        qn = _rmsnorm(q[:, sl], gq_ref[...])
        kn = _rmsnorm(k[:, sl], gk_ref[...])

        kmean = km_scr[:, sl]
        for b in range(tm // MOBA_BLOCK):
            km = jnp.mean(kn[b * MOBA_BLOCK:(b + 1) * MOBA_BLOCK], axis=0, keepdims=True)
            kmean = jnp.where(blk_rows == n0 + b, km, kmean)
        km_scr[:, sl] = kmean

        gate = _dot_nt_3x(kmean, qn)
        cnt = jnp.zeros((blocks_per_batch, tm), jnp.int32)
        for m in range(blocks_per_batch - 1):
            gm = gate[m:m + 1, :]
            beats = (gm > gate) | ((gm == gate) & (m < n_iota))
            cnt = cnt + jnp.where(beats & (m < q_blk), 1, 0)
        sel = ((cnt < MOBA_TOPK) & valid) | (n_iota == q_blk)
        bias_t = jnp.concatenate([jnp.where(sel, 0.0, NEG_INF), pad_rows], axis=0)
        bias = bias_t.T

        a0 = hh * AUG_DIM
        qa_ref[:, a0:a0 + HEAD_DIM] = (qn * scale).astype(BF16)
        qa_ref[:, a0 + HEAD_DIM:a0 + AUG_DIM] = bias.astype(BF16)
        ka_ref[:, a0:a0 + HEAD_DIM] = kn.astype(BF16)
        ka_ref[:, a0 + HEAD_DIM:a0 + AUG_DIM] = onehot


def _spatial_gate(u, g, lng_ref, lnb_ref, ws_ref, bs_ref, og_ref, o_ref, sg_scr):
    tm = PROJ_TM
    r = lax.broadcasted_iota(jnp.int32, (SGU_CHUNK, SGU_CHUNK), 0)
    c = lax.broadcasted_iota(jnp.int32, (SGU_CHUNK, SGU_CHUNK), 1)
    for gi in range(SGU_GROUPS):
        sl = slice(gi * SGU_GROUP_DIM, (gi + 1) * SGU_GROUP_DIM)
        gg = g[:, sl]
        d = gg - jnp.mean(gg, axis=-1, keepdims=True)
        var = jnp.mean(d * d, axis=-1, keepdims=True)
        gn = (d * lax.rsqrt(var + EPS) * lng_ref[:, sl] + lnb_ref[:, sl]).astype(BF16)
        w_causal = jnp.where(r >= c, ws_ref[gi], 0.0).astype(BF16)
        b_col = bs_ref[:, gi:gi + 1]
        for ci in range(tm // SGU_CHUNK):
            rows = slice(ci * SGU_CHUNK, (ci + 1) * SGU_CHUNK)
            mixed = _dot(w_causal, gn[rows]) + b_col
            sg_scr[rows, sl] = u[rows, sl] * mixed
    o_ref[...] = _rmsnorm(sg_scr[...], og_ref[...]).astype(BF16)


def _mix_body(x_ref, gx_ref, w_ref, gq_ref, gk_ref, lng_ref, lnb_ref, ws_ref, bs_ref, og_ref,
              qa_ref, ka_ref, v_ref, o_ref, km_scr, sg_scr, *, tiles_per_batch):
    i = pl.program_id(0)
    n0 = (i % tiles_per_batch) * (PROJ_TM // MOBA_BLOCK)

    @pl.when(i == 0)
    def _init():
        km_scr[...] = jnp.zeros_like(km_scr)

    h = _rmsnorm(x_ref[...], gx_ref[...]).astype(BF16)
    col = lambda c0, width: w_ref[:, c0:c0 + width]
    q = _dot(h, col(0, D_ATTN))
    k = _dot(h, col(D_ATTN, D_ATTN))
    u = _gelu_tanh(_dot(h, col(3 * D_ATTN, D_SGU)))
    _moba_gate(n0, q, k, gq_ref, gk_ref, qa_ref, ka_ref, km_scr)
    g = _gelu_tanh(_dot(h, col(3 * D_ATTN + D_SGU, D_SGU)))
    v_ref[...] = _dot(h, col(2 * D_ATTN, D_ATTN)).astype(BF16)
    _spatial_gate(u, g, lng_ref, lnb_ref, ws_ref, bs_ref, og_ref, o_ref, sg_scr)


def _mix(x, x_gain, w_in, q_gain, k_gain, ln_gain, ln_bias, w_spatial, b_spatial_t, out_gain, seq):
    tokens = x.shape[0]
    tm = PROJ_TM
    tiles_per_batch = seq // tm
    blocks_per_batch = seq // MOBA_BLOCK
    assert blocks_per_batch <= LANES and blocks_per_batch % 8 == 0
    row = lambda width: pl.BlockSpec((tm, width), lambda i: (i, 0))
    return pl.pallas_call(
        functools.partial(_mix_body, tiles_per_batch=tiles_per_batch),
        grid=(tokens // tm,),
        in_specs=[row(D_MODEL), _resident((1, D_MODEL)), _resident((D_MODEL, 3 * D_ATTN + 2 * D_SGU)),
                  _resident((1, HEAD_DIM)), _resident((1, HEAD_DIM)),
                  _resident((1, D_SGU)), _resident((1, D_SGU)),
                  _resident((SGU_GROUPS, SGU_CHUNK, SGU_CHUNK)), _resident((SGU_CHUNK, SGU_GROUPS)),
                  _resident((1, D_SGU))],
        out_specs=(row(ATTN_HEADS * AUG_DIM), row(ATTN_HEADS * AUG_DIM), row(D_ATTN), row(D_SGU)),
        out_shape=(jax.ShapeDtypeStruct((tokens, ATTN_HEADS * AUG_DIM), BF16),
                   jax.ShapeDtypeStruct((tokens, ATTN_HEADS * AUG_DIM), BF16),
                   jax.ShapeDtypeStruct((tokens, D_ATTN), BF16),
                   jax.ShapeDtypeStruct((tokens, D_SGU), BF16)),
        scratch_shapes=[pltpu.VMEM((blocks_per_batch, D_ATTN), F32), pltpu.VMEM((tm, D_SGU), F32)],
        compiler_params=_params("arbitrary"),
        name="mix_proj",
    )(x, x_gain, w_in, q_gain, k_gain, ln_gain, ln_bias, w_spatial, b_spatial_t, out_gain)


def _attn_scores(t, q_ref, k_ref, s_ref, m_ref):
    tq, kc = ATTN_TQ, ATTN_KV_CHUNK
    m_part = None
    for c in range(t + 1):
        s = _dot_nt(q_ref[...], k_ref[c * kc:(c + 1) * kc, :])
        if c == t:
            r = lax.broadcasted_iota(jnp.int32, (tq, kc), 0)
            col = lax.broadcasted_iota(jnp.int32, (tq, kc), 1)
            s = jnp.where(col <= r, s, NEG_INF)
        s_ref[:, c * kc:(c + 1) * kc] = s
        for j in range(kc // LANES):
            sj = s[:, j * LANES:(j + 1) * LANES]
            m_part = sj if m_part is None else jnp.maximum(m_part, sj)
    m_ref[...] = jnp.broadcast_to(jnp.max(m_part, axis=-1, keepdims=True), (tq, LANES))


def _attn_values(t, va_ref, o_ref, s_ref, m_ref):
    tq, kc = ATTN_TQ, ATTN_KV_CHUNK
    m = m_ref[...]
    acc = jnp.zeros((tq, AUG_DIM), F32)
    for c in range(t + 1):
        p = jnp.concatenate(
            [jnp.exp(s_ref[:, c * kc + j * LANES:c * kc + (j + 1) * LANES] - m).astype(BF16)
             for j in range(kc // LANES)], axis=1)
        acc = acc + _dot(p, va_ref[c * kc:(c + 1) * kc, :])
    o_ref[...] = acc[:, 0:HEAD_DIM] / acc[:, HEAD_DIM:AUG_DIM]


def _attn_body(q_ref, k_ref, v_ref, o_ref, s_scr, m_scr, va_scr):
    n_tiles = k_ref.shape[0] // ATTN_TQ
    tile = lambda t: slice(t * ATTN_TQ, (t + 1) * ATTN_TQ)
    va_scr[:, 0:HEAD_DIM] = v_ref[...]
    va_scr[:, HEAD_DIM:AUG_DIM] = jnp.ones((va_scr.shape[0], HEAD_DIM), BF16)

    def sub_step(ts):
        if ts >= 1:
            _attn_values(ts - 1, va_scr, o_ref.at[tile(ts - 1), :], s_scr.at[(ts - 1) % 2], m_scr.at[(ts - 1) % 2])
        if ts < n_tiles:
            _attn_scores(ts, q_ref.at[tile(ts), :], k_ref, s_scr.at[ts % 2], m_scr.at[ts % 2])

    for ts in range(n_tiles + 1):
        pl.when(pl.program_id(1) + ts >= 0)(functools.partial(sub_step, ts))


def _attention(qa, ka, v, batch, seq):
    tokens = batch * seq
    return pl.pallas_call(
        _attn_body,
        grid=(batch, ATTN_HEADS),
        in_specs=[pl.BlockSpec((seq, AUG_DIM), lambda b, h: (b, h)),
                  pl.BlockSpec((seq, AUG_DIM), lambda b, h: (b, h)),
                  pl.BlockSpec((seq, HEAD_DIM), lambda b, h: (b, h))],
        out_specs=pl.BlockSpec((seq, HEAD_DIM), lambda b, h: (b, h)),
        out_shape=jax.ShapeDtypeStruct((tokens, D_ATTN), F32),
        scratch_shapes=[pltpu.VMEM((2, ATTN_TQ, seq), F32), pltpu.VMEM((2, ATTN_TQ, LANES), F32),
                        pltpu.VMEM((seq, AUG_DIM), BF16)],
        compiler_params=_params("parallel", "parallel"),
        name="moba_attn",
    )(qa, ka, v)


def _outproj_body(a_ref, s_ref, x_ref, ga_ref, w_ref, o_ref):
    an = _rmsnorm(a_ref[...], ga_ref[...]).astype(BF16)
    o_ref[...] = (x_ref[...] + _dot(an, w_ref[0:D_ATTN, :])
                  + _dot(s_ref[...], w_ref[D_ATTN:D_ATTN + D_SGU, :]))


def _outproj(attn, sgu, x, attn_gain, w_out):
    tokens = x.shape[0]
    tm = PROJ_TM
    row = lambda width: pl.BlockSpec((tm, width), lambda i: (i, 0))
    return pl.pallas_call(
        _outproj_body,
        grid=(tokens // tm,),
        in_specs=[row(D_ATTN), row(D_SGU), row(D_MODEL), _resident((1, D_ATTN)),
                  _resident((D_ATTN + D_SGU, D_MODEL))],
        out_specs=row(D_MODEL),
        out_shape=jax.ShapeDtypeStruct((tokens, D_MODEL), F32),
        compiler_params=_params("parallel"),
        name="outproj",
    )(attn, sgu, x, attn_gain, w_out)


def kernel(x, ffn1_norm, ffn1_w_gate, ffn1_w_up, ffn1_w_down, mix_norm, w_in, q_norm, k_norm, sgu_ln_gain, sgu_ln_bias, sgu_w_spatial, sgu_b_spatial, attn_out_gain, sgu_out_gain, w_out, ffn2_norm, ffn2_w_gate, ffn2_w_up, ffn2_w_down):
    batch, seq, d_model = x.shape
    depth = ffn1_norm.shape[0]
    assert d_model == D_MODEL and seq % PROJ_TM == 0 and (batch * seq) % FFN_TM == 0
    xt = x.reshape(batch * seq, d_model)
    vec = lambda a: a.reshape(1, -1)
    for l in range(depth):
        xt, w_in_bf, w_out_bf = _ffn(xt, vec(ffn1_norm[l]), ffn1_w_gate[l], ffn1_w_up[l], ffn1_w_down[l],
                                     side=(w_in[l], w_out[l]))
        qa, ka, v, sgu = _mix(xt, vec(mix_norm[l]), w_in_bf, vec(q_norm[l]), vec(k_norm[l]),
                              vec(sgu_ln_gain[l]), vec(sgu_ln_bias[l]), sgu_w_spatial[l], sgu_b_spatial[l].T,
                              vec(sgu_out_gain[l]), seq)
        attn = _attention(qa, ka, v, batch, seq)
        xt = _outproj(attn, sgu, xt, vec(attn_out_gain[l]), w_out_bf)
        xt = _ffn(xt, vec(ffn2_norm[l]), ffn2_w_gate[l], ffn2_w_up[l], ffn2_w_down[l])
    return xt.reshape(batch, seq, d_model)
```

```python
import functools

import jax
import jax.numpy as jnp
from jax import lax
from jax.experimental import pallas as pl
from jax.experimental.pallas import tpu as pltpu

D_MODEL = 2048
D_FF = 5632
ATTN_HEADS = 8
HEAD_DIM = 128
D_ATTN = ATTN_HEADS * HEAD_DIM
SGU_GROUPS = 8
SGU_GROUP_DIM = 128
D_SGU = SGU_GROUPS * SGU_GROUP_DIM
MOBA_BLOCK = 256
MOBA_TOPK = 3
SGU_CHUNK = 128
EPS = 1e-6
NEG_INF = -1e30

LANES = 128
AUG_DIM = 2 * HEAD_DIM
VMEM_LIMIT_BYTES = 60 * 1024 * 1024

FFN_TM = 1024
FFN_HEAD_TF = 256
FFN_TF = 512
PROJ_TM = 512
ATTN_TQ = 512
ATTN_KV_CHUNK = 512

F32 = jnp.float32
BF16 = jnp.bfloat16


def _rmsnorm(x, gain):
    return x * lax.rsqrt(jnp.mean(x * x, axis=-1, keepdims=True) + EPS) * gain


def _gelu_tanh(x):
    cdf = 0.5 * (1.0 + jnp.tanh(0.7978845608028654 * (x + 0.044715 * (x * x * x))))
    return x * cdf


def _dot(a, b):
    return jnp.dot(a, b, preferred_element_type=F32)


def _dot_nt(a, b):
    return lax.dot_general(a, b, (((1,), (1,)), ((), ())), preferred_element_type=F32)


def _dot_nt_3x(a, b):
    a_hi, b_hi = a.astype(BF16), b.astype(BF16)
    a_lo = (a - a_hi.astype(F32)).astype(BF16)
    b_lo = (b - b_hi.astype(F32)).astype(BF16)
    return _dot_nt(a_hi, b_hi) + (_dot_nt(a_hi, b_lo) + _dot_nt(a_lo, b_hi))


def _params(*semantics):
    return pltpu.CompilerParams(dimension_semantics=semantics, vmem_limit_bytes=VMEM_LIMIT_BYTES)


def _resident(shape):
    return pl.BlockSpec(shape, lambda *_: (0,) * len(shape), pipeline_mode=pl.Buffered(1))


def _ffn_accumulate(h_scr, wg, wu, wd, base_ref, o_ref):
    h = h_scr[...]
    acts = []
    for b in range(wg.shape[0]):
        gate = _dot(h, wg[b])
        up = _dot(h, wu[b])
        acts.append(((gate / (1.0 + jnp.exp(-gate))) * up * 0.5).astype(BF16))
    o_ref[...] = base_ref[...] + _dot(jnp.concatenate(acts, axis=1), wd)


def _ffn_step(first, x_ref, g_ref, h_scr, wg, wu, wd, o_ref):
    if first:
        h_scr[...] = _rmsnorm(x_ref[...], g_ref[...]).astype(BF16)
    _ffn_accumulate(h_scr, wg, wu, wd, x_ref if first else o_ref, o_ref)


def _ffn_head_body(x_ref, g_ref, wg_ref, wu_ref, wd_ref, o_ref, wg_bf_ref, wu_bf_ref, wd_bf_ref, h_scr):
    @pl.when(pl.program_id(0) == 0)
    def _():
        x = x_ref[...]
        h_scr[...] = _rmsnorm(x, g_ref[...]).astype(BF16)
        o_ref[...] = x

    wg_bf_ref[0] = wg_ref[...].astype(BF16)
    wu_bf_ref[0] = wu_ref[...].astype(BF16)
    wd_bf_ref[...] = wd_ref[...].astype(BF16)
    _ffn_accumulate(h_scr, wg_bf_ref, wu_bf_ref, wd_bf_ref[...], o_ref, o_ref)


def _ffn_tail_body(x_ref, g_ref, wg_ref, wu_ref, wd_ref, head_hbm, *rest, side_windows):
    n_side = len(side_windows)
    side_in, (o_ref, *side_out), h_scr = rest[:n_side], rest[n_side:-1], rest[-1]
    i = pl.program_id(0)
    j = pl.program_id(1)

    step = i * pl.num_programs(1) + j
    for src, dst, (start, n) in zip(side_in, side_out, side_windows):
        @pl.when((step >= start) & (step < start + n))
        def _(src=src, dst=dst):
            dst[...] = src[...].astype(BF16)

    @pl.when((i == 0) & (j == 0))
    def _():
        pltpu.sync_copy(head_hbm, o_ref)

    for first in (True, False):
        @pl.when((i > 0) & ((j == 0) if first else (j > 0)))
        def _(first=first):
            _ffn_step(first, x_ref, g_ref, h_scr, wg_ref, wu_ref, wd_ref[...], o_ref)


def _ffn(x, gain, w_gate, w_up, w_down, side=()):
    tokens = x.shape[0]
    n_tiles = tokens // FFN_TM
    vec = pl.BlockSpec((1, D_MODEL), lambda *_: (0, 0))

    tf = FFN_HEAD_TF
    col = pl.BlockSpec((D_MODEL, tf), lambda j: (0, j))
    colblk = pl.BlockSpec((1, D_MODEL, tf), lambda j: (j, 0, 0))
    rowblk = pl.BlockSpec((tf, D_MODEL), lambda j: (j, 0))
    tile0 = pl.BlockSpec((FFN_TM, D_MODEL), lambda j: (0, 0))
    head, wg_bf, wu_bf, wd_bf = pl.pallas_call(
        _ffn_head_body,
        grid=(D_FF // tf,),
        in_specs=[tile0, vec, col, col, rowblk],
        out_specs=(tile0, colblk, colblk, rowblk),
        out_shape=(jax.ShapeDtypeStruct((FFN_TM, D_MODEL), F32),
                   jax.ShapeDtypeStruct((D_FF // tf, D_MODEL, tf), BF16),
                   jax.ShapeDtypeStruct((D_FF // tf, D_MODEL, tf), BF16),
                   jax.ShapeDtypeStruct((D_FF, D_MODEL), BF16)),
        scratch_shapes=[pltpu.VMEM((FFN_TM, D_MODEL), BF16)],
        compiler_params=_params("arbitrary"),
        name="ffn_head",
    )(x, gain, w_gate, w_up, w_down)

    tf = FFN_TF
    wcol = pl.BlockSpec((tf // FFN_HEAD_TF, D_MODEL, FFN_HEAD_TF), lambda i, j: (jnp.where(i == 0, 0, j), 0, 0))
    wrow = lambda i, j: (jnp.where(i == 0, 0, j), 0)
    row = pl.BlockSpec((FFN_TM, D_MODEL), lambda i, j: (i, 0))
    n_steps = D_FF // tf
    side_specs, side_windows, start = [], [], 0
    for w in side:
        n_side_tiles = w.shape[1] // LANES
        side_specs.append(pl.BlockSpec(
            (w.shape[0], LANES),
            lambda i, j, start=start, last=n_side_tiles - 1: (0, jnp.clip(i * n_steps + j - start, 0, last))))
        side_windows.append((start, n_side_tiles))
        start += n_side_tiles
    assert start <= n_tiles * n_steps
    outs = pl.pallas_call(
        functools.partial(_ffn_tail_body, side_windows=tuple(side_windows)),
        grid=(n_tiles, n_steps),
        in_specs=[row, vec, wcol, wcol,
                  pl.BlockSpec((tf, D_MODEL), wrow), pl.BlockSpec(memory_space=pl.ANY), *side_specs],
        out_specs=(row, *side_specs),
        out_shape=(jax.ShapeDtypeStruct((tokens, D_MODEL), F32),
                   *(jax.ShapeDtypeStruct(w.shape, BF16) for w in side)),
        scratch_shapes=[pltpu.VMEM((FFN_TM, D_MODEL), BF16)],
        compiler_params=_params("arbitrary", "arbitrary"),
        name="ffn_tail",
    )(x, gain, wg_bf, wu_bf, wd_bf, head, *side)
    return outs if side else outs[0]


def _moba_gate(n0, q, k, gq_ref, gk_ref, qa_ref, ka_ref, km_scr):
    tm = PROJ_TM
    blocks_per_batch = km_scr.shape[0]
    scale = HEAD_DIM ** -0.5
    blk_rows = lax.broadcasted_iota(jnp.int32, (blocks_per_batch, HEAD_DIM), 0)
    n_iota = lax.broadcasted_iota(jnp.int32, (blocks_per_batch, tm), 0)
    lane_tok = lax.broadcasted_iota(jnp.int32, (blocks_per_batch, tm), 1)
    q_blk = n0 + lane_tok // MOBA_BLOCK
    valid = n_iota < q_blk
    tok_blk = n0 + lax.broadcasted_iota(jnp.int32, (tm, LANES), 0) // MOBA_BLOCK
    onehot = jnp.where(lax.broadcasted_iota(jnp.int32, (tm, LANES), 1) == tok_blk, 1.0, 0.0).astype(BF16)
    pad_rows = jnp.zeros((LANES - blocks_per_batch, tm), F32)

    for hh in range(ATTN_HEADS):
        sl = slice(hh * HEAD_DIM, (hh + 1) * HEAD_DIM)
        qn = _rmsnorm(q[:, sl], gq_ref[...])
        kn = _rmsnorm(k[:, sl], gk_ref[...])

        kmean = km_scr[:, sl]
        for b in range(tm // MOBA_BLOCK):
            km = jnp.mean(kn[b * MOBA_BLOCK:(b + 1) * MOBA_BLOCK], axis=0, keepdims=True)
            kmean = jnp.where(blk_rows == n0 + b, km, kmean)
        km_scr[:, sl] = kmean

        gate = _dot_nt_3x(kmean, qn)
        cnt = jnp.zeros((blocks_per_batch, tm), jnp.int32)
        for m in range(blocks_per_batch - 1):
            gm = gate[m:m + 1, :]
            beats = (gm > gate) | ((gm == gate) & (m < n_iota))
            cnt = cnt + jnp.where(beats & (m < q_blk), 1, 0)
        sel = ((cnt < MOBA_TOPK) & valid) | (n_iota == q_blk)
        bias_t = jnp.concatenate([jnp.where(sel, 0.0, NEG_INF), pad_rows], axis=0)
        bias = bias_t.T

        a0 = hh * AUG_DIM
        qa_ref[:, a0:a0 + HEAD_DIM] = (qn * scale).astype(BF16)
        qa_ref[:, a0 + HEAD_DIM:a0 + AUG_DIM] = bias.astype(BF16)
        ka_ref[:, a0:a0 + HEAD_DIM] = kn.astype(BF16)
        ka_ref[:, a0 + HEAD_DIM:a0 + AUG_DIM] = onehot


def _spatial_gate(u, g, lng_ref, lnb_ref, ws_ref, bs_ref, og_ref, o_ref, sg_scr):
    tm = PROJ_TM
    r = lax.broadcasted_iota(jnp.int32, (SGU_CHUNK, SGU_CHUNK), 0)
    c = lax.broadcasted_iota(jnp.int32, (SGU_CHUNK, SGU_CHUNK), 1)
    for gi in range(SGU_GROUPS):
        sl = slice(gi * SGU_GROUP_DIM, (gi + 1) * SGU_GROUP_DIM)
        gg = g[:, sl]
        d = gg - jnp.mean(gg, axis=-1, keepdims=True)
        var = jnp.mean(d * d, axis=-1, keepdims=True)
        gn = (d * lax.rsqrt(var + EPS) * lng_ref[:, sl] + lnb_ref[:, sl]).astype(BF16)
        w_causal = jnp.where(r >= c, ws_ref[gi], 0.0).astype(BF16)
        b_col = bs_ref[:, gi:gi + 1]
        for ci in range(tm // SGU_CHUNK):
            rows = slice(ci * SGU_CHUNK, (ci + 1) * SGU_CHUNK)
            mixed = _dot(w_causal, gn[rows]) + b_col
            sg_scr[rows, sl] = u[rows, sl] * mixed
    o_ref[...] = _rmsnorm(sg_scr[...], og_ref[...]).astype(BF16)


def _mix_body(x_ref, gx_ref, w_ref, gq_ref, gk_ref, lng_ref, lnb_ref, ws_ref, bs_ref, og_ref,
              qa_ref, ka_ref, v_ref, o_ref, km_scr, sg_scr, *, tiles_per_batch):
    i = pl.program_id(0)
    n0 = (i % tiles_per_batch) * (PROJ_TM // MOBA_BLOCK)

    @pl.when(i == 0)
    def _init():
        km_scr[...] = jnp.zeros_like(km_scr)

    h = _rmsnorm(x_ref[...], gx_ref[...]).astype(BF16)
    col = lambda c0, width: w_ref[:, c0:c0 + width]
    q = _dot(h, col(0, D_ATTN))
    k = _dot(h, col(D_ATTN, D_ATTN))
    u = _gelu_tanh(_dot(h, col(3 * D_ATTN, D_SGU)))
    _moba_gate(n0, q, k, gq_ref, gk_ref, qa_ref, ka_ref, km_scr)
    g = _gelu_tanh(_dot(h, col(3 * D_ATTN + D_SGU, D_SGU)))
    v_ref[...] = _dot(h, col(2 * D_ATTN, D_ATTN)).astype(BF16)
    _spatial_gate(u, g, lng_ref, lnb_ref, ws_ref, bs_ref, og_ref, o_ref, sg_scr)


def _mix(x, x_gain, w_in, q_gain, k_gain, ln_gain, ln_bias, w_spatial, b_spatial_t, out_gain, seq):
    tokens = x.shape[0]
    tm = PROJ_TM
    tiles_per_batch = seq // tm
    blocks_per_batch = seq // MOBA_BLOCK
    assert blocks_per_batch <= LANES and blocks_per_batch % 8 == 0
    row = lambda width: pl.BlockSpec((tm, width), lambda i: (i, 0))
    return pl.pallas_call(
        functools.partial(_mix_body, tiles_per_batch=tiles_per_batch),
        grid=(tokens // tm,),
        in_specs=[row(D_MODEL), _resident((1, D_MODEL)), _resident((D_MODEL, 3 * D_ATTN + 2 * D_SGU)),
                  _resident((1, HEAD_DIM)), _resident((1, HEAD_DIM)),
                  _resident((1, D_SGU)), _resident((1, D_SGU)),
                  _resident((SGU_GROUPS, SGU_CHUNK, SGU_CHUNK)), _resident((SGU_CHUNK, SGU_GROUPS)),
                  _resident((1, D_SGU))],
        out_specs=(row(ATTN_HEADS * AUG_DIM), row(ATTN_HEADS * AUG_DIM), row(D_ATTN), row(D_SGU)),
        out_shape=(jax.ShapeDtypeStruct((tokens, ATTN_HEADS * AUG_DIM), BF16),
                   jax.ShapeDtypeStruct((tokens, ATTN_HEADS * AUG_DIM), BF16),
                   jax.ShapeDtypeStruct((tokens, D_ATTN), BF16),
                   jax.ShapeDtypeStruct((tokens, D_SGU), BF16)),
        scratch_shapes=[pltpu.VMEM((blocks_per_batch, D_ATTN), F32), pltpu.VMEM((tm, D_SGU), F32)],
        compiler_params=_params("arbitrary"),
        name="mix_proj",
    )(x, x_gain, w_in, q_gain, k_gain, ln_gain, ln_bias, w_spatial, b_spatial_t, out_gain)


def _attn_scores(t, q_ref, k_ref, s_ref, m_ref):
    tq, kc = ATTN_TQ, ATTN_KV_CHUNK
    m_part = None
    for c in range(t + 1):
        s = _dot_nt(q_ref[...], k_ref[c * kc:(c + 1) * kc, :])
        if c == t:
            r = lax.broadcasted_iota(jnp.int32, (tq, kc), 0)
            col = lax.broadcasted_iota(jnp.int32, (tq, kc), 1)
            s = jnp.where(col <= r, s, NEG_INF)
        s_ref[:, c * kc:(c + 1) * kc] = s
        for j in range(kc // LANES):
            sj = s[:, j * LANES:(j + 1) * LANES]
            m_part = sj if m_part is None else jnp.maximum(m_part, sj)
    m_ref[...] = jnp.broadcast_to(jnp.max(m_part, axis=-1, keepdims=True), (tq, LANES))


def _attn_values(t, va_ref, o_ref, s_ref, m_ref):
    tq, kc = ATTN_TQ, ATTN_KV_CHUNK
    m = m_ref[...]
    acc = jnp.zeros((tq, AUG_DIM), F32)
    for c in range(t + 1):
        p = jnp.concatenate(
            [jnp.exp(s_ref[:, c * kc + j * LANES:c * kc + (j + 1) * LANES] - m).astype(BF16)
             for j in range(kc // LANES)], axis=1)
        acc = acc + _dot(p, va_ref[c * kc:(c + 1) * kc, :])
    o_ref[...] = acc[:, 0:HEAD_DIM] / acc[:, HEAD_DIM:AUG_DIM]


def _attn_body(q_ref, k_ref, v_ref, o_ref, s_scr, m_scr, va_scr):
    n_tiles = k_ref.shape[0] // ATTN_TQ
    tile = lambda t: slice(t * ATTN_TQ, (t + 1) * ATTN_TQ)
    va_scr[:, 0:HEAD_DIM] = v_ref[...]
    va_scr[:, HEAD_DIM:AUG_DIM] = jnp.ones((va_scr.shape[0], HEAD_DIM), BF16)

    def sub_step(ts):
        if ts >= 1:
            _attn_values(ts - 1, va_scr, o_ref.at[tile(ts - 1), :], s_scr.at[(ts - 1) % 2], m_scr.at[(ts - 1) % 2])
        if ts < n_tiles:
            _attn_scores(ts, q_ref.at[tile(ts), :], k_ref, s_scr.at[ts % 2], m_scr.at[ts % 2])

    for ts in range(n_tiles + 1):
        pl.when(pl.program_id(1) + ts >= 0)(functools.partial(sub_step, ts))


def _attention(qa, ka, v, batch, seq):
    tokens = batch * seq
    return pl.pallas_call(
        _attn_body,
        grid=(batch, ATTN_HEADS),
        in_specs=[pl.BlockSpec((seq, AUG_DIM), lambda b, h: (b, h)),
                  pl.BlockSpec((seq, AUG_DIM), lambda b, h: (b, h)),
                  pl.BlockSpec((seq, HEAD_DIM), lambda b, h: (b, h))],
        out_specs=pl.BlockSpec((seq, HEAD_DIM), lambda b, h: (b, h)),
        out_shape=jax.ShapeDtypeStruct((tokens, D_ATTN), F32),
        scratch_shapes=[pltpu.VMEM((2, ATTN_TQ, seq), F32), pltpu.VMEM((2, ATTN_TQ, LANES), F32),
                        pltpu.VMEM((seq, AUG_DIM), BF16)],
        compiler_params=_params("parallel", "parallel"),
        name="moba_attn",
    )(qa, ka, v)


def _outproj_body(a_ref, s_ref, x_ref, ga_ref, w_ref, o_ref):
    an = _rmsnorm(a_ref[...], ga_ref[...]).astype(BF16)
    o_ref[...] = (x_ref[...] + _dot(an, w_ref[0:D_ATTN, :])
                  + _dot(s_ref[...], w_ref[D_ATTN:D_ATTN + D_SGU, :]))


def _outproj(attn, sgu, x, attn_gain, w_out):
    tokens = x.shape[0]
    tm = PROJ_TM
    row = lambda width: pl.BlockSpec((tm, width), lambda i: (i, 0))
    return pl.pallas_call(
        _outproj_body,
        grid=(tokens // tm,),
        in_specs=[row(D_ATTN), row(D_SGU), row(D_MODEL), _resident((1, D_ATTN)),
                  _resident((D_ATTN + D_SGU, D_MODEL))],
        out_specs=row(D_MODEL),
        out_shape=jax.ShapeDtypeStruct((tokens, D_MODEL), F32),
        compiler_params=_params("parallel"),
        name="outproj",
    )(attn, sgu, x, attn_gain, w_out)


def kernel(x, ffn1_norm, ffn1_w_gate, ffn1_w_up, ffn1_w_down, mix_norm, w_in, q_norm, k_norm, sgu_ln_gain, sgu_ln_bias, sgu_w_spatial, sgu_b_spatial, attn_out_gain, sgu_out_gain, w_out, ffn2_norm, ffn2_w_gate, ffn2_w_up, ffn2_w_down):
    batch, seq, d_model = x.shape
    depth = ffn1_norm.shape[0]
    assert d_model == D_MODEL and seq % PROJ_TM == 0 and (batch * seq) % FFN_TM == 0
    xt = x.reshape(batch * seq, d_model)
    vec = lambda a: a.reshape(1, -1)
    for l in range(depth):
        xt, w_in_bf, w_out_bf = _ffn(xt, vec(ffn1_norm[l]), ffn1_w_gate[l], ffn1_w_up[l], ffn1_w_down[l],
                                     side=(w_in[l], w_out[l]))
        qa, ka, v, sgu = _mix(xt, vec(mix_norm[l]), w_in_bf, vec(q_norm[l]), vec(k_norm[l]),
                              vec(sgu_ln_gain[l]), vec(sgu_ln_bias[l]), sgu_w_spatial[l], sgu_b_spatial[l].T,
                              vec(sgu_out_gain[l]), seq)
        attn = _attention(qa, ka, v, batch, seq)
        xt = _outproj(attn, sgu, xt, vec(attn_out_gain[l]), w_out_bf)
        xt = _ffn(xt, vec(ffn2_norm[l]), ffn2_w_gate[l], ffn2_w_up[l], ffn2_w_down[l])
    return xt.reshape(batch, seq, d_model)
```

```python
import functools

import jax
import jax.numpy as jnp
from jax import lax
from jax.experimental import pallas as pl
from jax.experimental.pallas import tpu as pltpu

D_MODEL = 2048
D_FF = 5632
ATTN_HEADS = 8
HEAD_DIM = 128
D_ATTN = ATTN_HEADS * HEAD_DIM
SGU_GROUPS = 8
SGU_GROUP_DIM = 128
D_SGU = SGU_GROUPS * SGU_GROUP_DIM
MOBA_BLOCK = 256
MOBA_TOPK = 3
SGU_CHUNK = 128
EPS = 1e-6
NEG_INF = -1e30

LANES = 128
AUG_DIM = 2 * HEAD_DIM
VMEM_LIMIT_BYTES = 60 * 1024 * 1024

FFN_TM = 1024
FFN_HEAD_TF = 256
FFN_TF = 512
PROJ_TM = 512
ATTN_TQ = 512
ATTN_KV_CHUNK = 512

F32 = jnp.float32
BF16 = jnp.bfloat16


def _rmsnorm(x, gain):
    return x * lax.rsqrt(jnp.mean(x * x, axis=-1, keepdims=True) + EPS) * gain


def _gelu_tanh(x):
    cdf = 0.5 * (1.0 + jnp.tanh(0.7978845608028654 * (x + 0.044715 * (x * x * x))))
    return x * cdf


def _dot(a, b):
    return jnp.dot(a, b, preferred_element_type=F32)


def _dot_nt(a, b):
    return lax.dot_general(a, b, (((1,), (1,)), ((), ())), preferred_element_type=F32)


def _dot_nt_3x(a, b):
    a_hi, b_hi = a.astype(BF16), b.astype(BF16)
    a_lo = (a - a_hi.astype(F32)).astype(BF16)
    b_lo = (b - b_hi.astype(F32)).astype(BF16)
    return _dot_nt(a_hi, b_hi) + (_dot_nt(a_hi, b_lo) + _dot_nt(a_lo, b_hi))


def _params(*semantics):
    return pltpu.CompilerParams(dimension_semantics=semantics, vmem_limit_bytes=VMEM_LIMIT_BYTES)


def _resident(shape):
    return pl.BlockSpec(shape, lambda *_: (0,) * len(shape), pipeline_mode=pl.Buffered(1))


def _ffn_accumulate(h_scr, wg, wu, wd, base_ref, o_ref):
    h = h_scr[...]
    acts = []
    for b in range(wg.shape[0]):
        gate = _dot(h, wg[b])
        up = _dot(h, wu[b])
        acts.append(((gate / (1.0 + jnp.exp(-gate))) * up * 0.5).astype(BF16))
    o_ref[...] = base_ref[...] + _dot(jnp.concatenate(acts, axis=1), wd)


def _ffn_step(first, x_ref, g_ref, h_scr, wg, wu, wd, o_ref):
    if first:
        h_scr[...] = _rmsnorm(x_ref[...], g_ref[...]).astype(BF16)
    _ffn_accumulate(h_scr, wg, wu, wd, x_ref if first else o_ref, o_ref)


def _ffn_head_body(x_ref, g_ref, wg_ref, wu_ref, wd_ref, o_ref, wg_bf_ref, wu_bf_ref, wd_bf_ref, h_scr):
    @pl.when(pl.program_id(0) == 0)
    def _():
        x = x_ref[...]
        h_scr[...] = _rmsnorm(x, g_ref[...]).astype(BF16)
        o_ref[...] = x

    wg_bf_ref[0] = wg_ref[...].astype(BF16)
    wu_bf_ref[0] = wu_ref[...].astype(BF16)
    wd_bf_ref[...] = wd_ref[...].astype(BF16)
    _ffn_accumulate(h_scr, wg_bf_ref, wu_bf_ref, wd_bf_ref[...], o_ref, o_ref)


def _ffn_tail_body(x_ref, g_ref, wg_ref, wu_ref, wd_ref, head_hbm, *rest, side_windows):
    n_side = len(side_windows)
    side_in, (o_ref, *side_out), h_scr = rest[:n_side], rest[n_side:-1], rest[-1]
    i = pl.program_id(0)
    j = pl.program_id(1)

    step = i * pl.num_programs(1) + j
    for src, dst, (start, n) in zip(side_in, side_out, side_windows):
        @pl.when((step >= start) & (step < start + n))
        def _(src=src, dst=dst):
            dst[...] = src[...].astype(BF16)

    @pl.when((i == 0) & (j == 0))
    def _():
        pltpu.sync_copy(head_hbm, o_ref)

    for first in (True, False):
        @pl.when((i > 0) & ((j == 0) if first else (j > 0)))
        def _(first=first):
            _ffn_step(first, x_ref, g_ref, h_scr, wg_ref, wu_ref, wd_ref[...], o_ref)


def _ffn(x, gain, w_gate, w_up, w_down, side=()):
    tokens = x.shape[0]
    n_tiles = tokens // FFN_TM
    vec = pl.BlockSpec((1, D_MODEL), lambda *_: (0, 0))

    tf = FFN_HEAD_TF
    col = pl.BlockSpec((D_MODEL, tf), lambda j: (0, j))
    colblk = pl.BlockSpec((1, D_MODEL, tf), lambda j: (j, 0, 0))
    rowblk = pl.BlockSpec((tf, D_MODEL), lambda j: (j, 0))
    tile0 = pl.BlockSpec((FFN_TM, D_MODEL), lambda j: (0, 0))
    head, wg_bf, wu_bf, wd_bf = pl.pallas_call(
        _ffn_head_body,
        grid=(D_FF // tf,),
        in_specs=[tile0, vec, col, col, rowblk],
        out_specs=(tile0, colblk, colblk, rowblk),
        out_shape=(jax.ShapeDtypeStruct((FFN_TM, D_MODEL), F32),
                   jax.ShapeDtypeStruct((D_FF // tf, D_MODEL, tf), BF16),
                   jax.ShapeDtypeStruct((D_FF // tf, D_MODEL, tf), BF16),
                   jax.ShapeDtypeStruct((D_FF, D_MODEL), BF16)),
        scratch_shapes=[pltpu.VMEM((FFN_TM, D_MODEL), BF16)],
        compiler_params=_params("arbitrary"),
        name="ffn_head",
    )(x, gain, w_gate, w_up, w_down)

    tf = FFN_TF
    wcol = pl.BlockSpec((tf // FFN_HEAD_TF, D_MODEL, FFN_HEAD_TF), lambda i, j: (jnp.where(i == 0, 0, j), 0, 0))
    wrow = lambda i, j: (jnp.where(i == 0, 0, j), 0)
    row = pl.BlockSpec((FFN_TM, D_MODEL), lambda i, j: (i, 0))
    n_steps = D_FF // tf
    side_specs, side_windows, start = [], [], 0
    for w in side:
        n_side_tiles = w.shape[1] // LANES
        side_specs.append(pl.BlockSpec(
            (w.shape[0], LANES),
            lambda i, j, start=start, last=n_side_tiles - 1: (0, jnp.clip(i * n_steps + j - start, 0, last))))
        side_windows.append((start, n_side_tiles))
        start += n_side_tiles
    assert start <= n_tiles * n_steps
    outs = pl.pallas_call(
        functools.partial(_ffn_tail_body, side_windows=tuple(side_windows)),
        grid=(n_tiles, n_steps),
        in_specs=[row, vec, wcol, wcol,
                  pl.BlockSpec((tf, D_MODEL), wrow), pl.BlockSpec(memory_space=pl.ANY), *side_specs],
        out_specs=(row, *side_specs),
        out_shape=(jax.ShapeDtypeStruct((tokens, D_MODEL), F32),
                   *(jax.ShapeDtypeStruct(w.shape, BF16) for w in side)),
        scratch_shapes=[pltpu.VMEM((FFN_TM, D_MODEL), BF16)],
        compiler_params=_params("arbitrary", "arbitrary"),
        name="ffn_tail",
    )(x, gain, wg_bf, wu_bf, wd_bf, head, *side)
    return outs if side else outs[0]


def _moba_gate(n0, q, k, gq_ref, gk_ref, qa_ref, ka_ref, km_scr):
    tm = PROJ_TM
    blocks_per_batch = km_scr.shape[0]
    scale = HEAD_DIM ** -0.5
    blk_rows = lax.broadcasted_iota(jnp.int32, (blocks_per_batch, HEAD_DIM), 0)
    n_iota = lax.broadcasted_iota(jnp.int32, (blocks_per_batch, tm), 0)
    lane_tok = lax.broadcasted_iota(jnp.int32, (blocks_per_batch, tm), 1)
    q_blk = n0 + lane_tok // MOBA_BLOCK
    valid = n_iota < q_blk
    tok_blk = n0 + lax.broadcasted_iota(jnp.int32, (tm, LANES), 0) // MOBA_BLOCK
    onehot = jnp.where(lax.broadcasted_iota(jnp.int32, (tm, LANES), 1) == tok_blk, 1.0, 0.0).astype(BF16)
    pad_rows = jnp.zeros((LANES - blocks_per_batch, tm), F32)

    for hh in range(ATTN_HEADS):
        sl = slice(hh * HEAD_DIM, (hh + 1) * HEAD_DIM)
        qn = _rmsnorm(q[:, sl], gq_ref[...])
        kn = _rmsnorm(k[:, sl], gk_ref[...])

        kmean = km_scr[:, sl]
        for b in range(tm // MOBA_BLOCK):
            km = jnp.mean(kn[b * MOBA_BLOCK:(b + 1) * MOBA_BLOCK], axis=0, keepdims=True)
            kmean = jnp.where(blk_rows == n0 + b, km, kmean)
        km_scr[:, sl] = kmean

        gate = _dot_nt_3x(kmean, qn)
        cnt = jnp.zeros((blocks_per_batch, tm), jnp.int32)
        for m in range(blocks_per_batch - 1):
            gm = gate[m:m + 1, :]
            beats = (gm > gate) | ((gm == gate) & (m < n_iota))
            cnt = cnt + jnp.where(beats & (m < q_blk), 1, 0)
        sel = ((cnt < MOBA_TOPK) & valid) | (n_iota == q_blk)
        bias_t = jnp.concatenate([jnp.where(sel, 0.0, NEG_INF), pad_rows], axis=0)
        bias = bias_t.T

        a0 = hh * AUG_DIM
        qa_ref[:, a0:a0 + HEAD_DIM] = (qn * scale).astype(BF16)
        qa_ref[:, a0 + HEAD_DIM:a0 + AUG_DIM] = bias.astype(BF16)
        ka_ref[:, a0:a0 + HEAD_DIM] = kn.astype(BF16)
        ka_ref[:, a0 + HEAD_DIM:a0 + AUG_DIM] = onehot


def _spatial_gate(u, g, lng_ref, lnb_ref, ws_ref, bs_ref, og_ref, o_ref, sg_scr):
    tm = PROJ_TM
    r = lax.broadcasted_iota(jnp.int32, (SGU_CHUNK, SGU_CHUNK), 0)
    c = lax.broadcasted_iota(jnp.int32, (SGU_CHUNK, SGU_CHUNK), 1)
    for gi in range(SGU_GROUPS):
        sl = slice(gi * SGU_GROUP_DIM, (gi + 1) * SGU_GROUP_DIM)
        gg = g[:, sl]
        d = gg - jnp.mean(gg, axis=-1, keepdims=True)
        var = jnp.mean(d * d, axis=-1, keepdims=True)
        gn = (d * lax.rsqrt(var + EPS) * lng_ref[:, sl] + lnb_ref[:, sl]).astype(BF16)
        w_causal = jnp.where(r >= c, ws_ref[gi], 0.0).astype(BF16)
        b_col = bs_ref[:, gi:gi + 1]
        for ci in range(tm // SGU_CHUNK):
            rows = slice(ci * SGU_CHUNK, (ci + 1) * SGU_CHUNK)
            mixed = _dot(w_causal, gn[rows]) + b_col
            sg_scr[rows, sl] = u[rows, sl] * mixed
    o_ref[...] = _rmsnorm(sg_scr[...], og_ref[...]).astype(BF16)


def _mix_body(x_ref, gx_ref, w_ref, gq_ref, gk_ref, lng_ref, lnb_ref, ws_ref, bs_ref, og_ref,
              qa_ref, ka_ref, v_ref, o_ref, km_scr, sg_scr, *, tiles_per_batch):
    i = pl.program_id(0)
    n0 = (i % tiles_per_batch) * (PROJ_TM // MOBA_BLOCK)

    @pl.when(i == 0)
    def _init():
        km_scr[...] = jnp.zeros_like(km_scr)

    h = _rmsnorm(x_ref[...], gx_ref[...]).astype(BF16)
    col = lambda c0, width: w_ref[:, c0:c0 + width]
    q = _dot(h, col(0, D_ATTN))
    k = _dot(h, col(D_ATTN, D_ATTN))
    u = _gelu_tanh(_dot(h, col(3 * D_ATTN, D_SGU)))
    _moba_gate(n0, q, k, gq_ref, gk_ref, qa_ref, ka_ref, km_scr)
    g = _gelu_tanh(_dot(h, col(3 * D_ATTN + D_SGU, D_SGU)))
    v_ref[...] = _dot(h, col(2 * D_ATTN, D_ATTN)).astype(BF16)
    _spatial_gate(u, g, lng_ref, lnb_ref, ws_ref, bs_ref, og_ref, o_ref, sg_scr)


def _mix(x, x_gain, w_in, q_gain, k_gain, ln_gain, ln_bias, w_spatial, b_spatial_t, out_gain, seq):
    tokens = x.shape[0]
    tm = PROJ_TM
    tiles_per_batch = seq // tm
    blocks_per_batch = seq // MOBA_BLOCK
    assert blocks_per_batch <= LANES and blocks_per_batch % 8 == 0
    row = lambda width: pl.BlockSpec((tm, width), lambda i: (i, 0))
    return pl.pallas_call(
        functools.partial(_mix_body, tiles_per_batch=tiles_per_batch),
        grid=(tokens // tm,),
        in_specs=[row(D_MODEL), _resident((1, D_MODEL)), _resident((D_MODEL, 3 * D_ATTN + 2 * D_SGU)),
                  _resident((1, HEAD_DIM)), _resident((1, HEAD_DIM)),
                  _resident((1, D_SGU)), _resident((1, D_SGU)),
                  _resident((SGU_GROUPS, SGU_CHUNK, SGU_CHUNK)), _resident((SGU_CHUNK, SGU_GROUPS)),
                  _resident((1, D_SGU))],
        out_specs=(row(ATTN_HEADS * AUG_DIM), row(ATTN_HEADS * AUG_DIM), row(D_ATTN), row(D_SGU)),
        out_shape=(jax.ShapeDtypeStruct((tokens, ATTN_HEADS * AUG_DIM), BF16),
                   jax.ShapeDtypeStruct((tokens, ATTN_HEADS * AUG_DIM), BF16),
                   jax.ShapeDtypeStruct((tokens, D_ATTN), BF16),
                   jax.ShapeDtypeStruct((tokens, D_SGU), BF16)),
        scratch_shapes=[pltpu.VMEM((blocks_per_batch, D_ATTN), F32), pltpu.VMEM((tm, D_SGU), F32)],
        compiler_params=_params("arbitrary"),
        name="mix_proj",
    )(x, x_gain, w_in, q_gain, k_gain, ln_gain, ln_bias, w_spatial, b_spatial_t, out_gain)


def _attn_scores(t, q_ref, k_ref, s_ref, m_ref):
    tq, kc = ATTN_TQ, ATTN_KV_CHUNK
    half = MOBA_BLOCK
    m_part = None

    def fold(m_part, s):
        for j in range(s.shape[1] // LANES):
            sj = s[:, j * LANES:(j + 1) * LANES]
            m_part = sj if m_part is None else jnp.maximum(m_part, sj)
        return m_part

    for c in range(t):
        s = _dot_nt(q_ref[...], k_ref[c * kc:(c + 1) * kc, :])
        s_ref[:, c * kc:(c + 1) * kc] = s
        m_part = fold(m_part, s)
    k0 = t * kc
    r = lax.broadcasted_iota(jnp.int32, (half, half), 0)
    col = lax.broadcasted_iota(jnp.int32, (half, half), 1)
    s_top = jnp.where(col <= r, _dot_nt(q_ref[0:half, :], k_ref[k0:k0 + half, :]), NEG_INF)
    s_ref[0:half, k0:k0 + half] = s_top
    r2 = lax.broadcasted_iota(jnp.int32, (half, kc), 0) + half
    col2 = lax.broadcasted_iota(jnp.int32, (half, kc), 1)
    s_bot = jnp.where(col2 <= r2, _dot_nt(q_ref[half:tq, :], k_ref[k0:k0 + kc, :]), NEG_INF)
    s_ref[half:tq, k0:k0 + kc] = s_bot
    m_top = fold(None if m_part is None else m_part[0:half], s_top)
    m_bot = fold(None if m_part is None else m_part[half:tq], s_bot)
    m_part = jnp.concatenate([m_top, m_bot], axis=0)
    m_ref[...] = jnp.broadcast_to(jnp.max(m_part, axis=-1, keepdims=True), (tq, LANES))


def _attn_values(t, va_ref, o_ref, s_ref, m_ref):
    tq, kc = ATTN_TQ, ATTN_KV_CHUNK
    half = MOBA_BLOCK
    m = m_ref[...]

    def probs(rows, c0, width):
        return jnp.concatenate(
            [jnp.exp(s_ref[rows, c0 + j * LANES:c0 + (j + 1) * LANES] - m[rows]).astype(BF16)
             for j in range(width // LANES)], axis=1)

    acc = jnp.zeros((tq, AUG_DIM), F32)
    for c in range(t):
        acc = acc + _dot(probs(slice(0, tq), c * kc, kc), va_ref[c * kc:(c + 1) * kc, :])
    k0 = t * kc
    top = acc[0:half] + _dot(probs(slice(0, half), k0, half), va_ref[k0:k0 + half, :])
    bot = acc[half:tq] + _dot(probs(slice(half, tq), k0, kc), va_ref[k0:k0 + kc, :])
    acc = jnp.concatenate([top, bot], axis=0)
    o_ref[...] = acc[:, 0:HEAD_DIM] / acc[:, HEAD_DIM:AUG_DIM]


def _attn_body(q_ref, k_ref, v_ref, o_ref, s_scr, m_scr, va_scr):
    n_tiles = k_ref.shape[0] // ATTN_TQ
    tile = lambda t: slice(t * ATTN_TQ, (t + 1) * ATTN_TQ)
    va_scr[:, 0:HEAD_DIM] = v_ref[...]
    va_scr[:, HEAD_DIM:AUG_DIM] = jnp.ones((va_scr.shape[0], HEAD_DIM), BF16)

    def sub_step(ts):
        if ts >= 1:
            _attn_values(ts - 1, va_scr, o_ref.at[tile(ts - 1), :], s_scr.at[(ts - 1) % 2], m_scr.at[(ts - 1) % 2])
        if ts < n_tiles:
            _attn_scores(ts, q_ref.at[tile(ts), :], k_ref, s_scr.at[ts % 2], m_scr.at[ts % 2])

    for ts in range(n_tiles + 1):
        pl.when(pl.program_id(1) + ts >= 0)(functools.partial(sub_step, ts))


def _attention(qa, ka, v, batch, seq):
    tokens = batch * seq
    return pl.pallas_call(
        _attn_body,
        grid=(batch, ATTN_HEADS),
        in_specs=[pl.BlockSpec((seq, AUG_DIM), lambda b, h: (b, h)),
                  pl.BlockSpec((seq, AUG_DIM), lambda b, h: (b, h)),
                  pl.BlockSpec((seq, HEAD_DIM), lambda b, h: (b, h))],
        out_specs=pl.BlockSpec((seq, HEAD_DIM), lambda b, h: (b, h)),
        out_shape=jax.ShapeDtypeStruct((tokens, D_ATTN), F32),
        scratch_shapes=[pltpu.VMEM((2, ATTN_TQ, seq), F32), pltpu.VMEM((2, ATTN_TQ, LANES), F32),
                        pltpu.VMEM((seq, AUG_DIM), BF16)],
        compiler_params=_params("parallel", "parallel"),
        name="moba_attn",
    )(qa, ka, v)


def _outproj_body(a_ref, s_ref, x_ref, ga_ref, w_ref, o_ref):
    an = _rmsnorm(a_ref[...], ga_ref[...]).astype(BF16)
    o_ref[...] = (x_ref[...] + _dot(an, w_ref[0:D_ATTN, :])
                  + _dot(s_ref[...], w_ref[D_ATTN:D_ATTN + D_SGU, :]))


def _outproj(attn, sgu, x, attn_gain, w_out):
    tokens = x.shape[0]
    tm = PROJ_TM
    row = lambda width: pl.BlockSpec((tm, width), lambda i: (i, 0))
    return pl.pallas_call(
        _outproj_body,
        grid=(tokens // tm,),
        in_specs=[row(D_ATTN), row(D_SGU), row(D_MODEL), _resident((1, D_ATTN)),
                  _resident((D_ATTN + D_SGU, D_MODEL))],
        out_specs=row(D_MODEL),
        out_shape=jax.ShapeDtypeStruct((tokens, D_MODEL), F32),
        compiler_params=_params("parallel"),
        name="outproj",
    )(attn, sgu, x, attn_gain, w_out)


def kernel(x, ffn1_norm, ffn1_w_gate, ffn1_w_up, ffn1_w_down, mix_norm, w_in, q_norm, k_norm, sgu_ln_gain, sgu_ln_bias, sgu_w_spatial, sgu_b_spatial, attn_out_gain, sgu_out_gain, w_out, ffn2_norm, ffn2_w_gate, ffn2_w_up, ffn2_w_down):
    batch, seq, d_model = x.shape
    depth = ffn1_norm.shape[0]
    assert d_model == D_MODEL and seq % PROJ_TM == 0 and (batch * seq) % FFN_TM == 0
    xt = x.reshape(batch * seq, d_model)
    vec = lambda a: a.reshape(1, -1)
    for l in range(depth):
        xt, w_in_bf, w_out_bf = _ffn(xt, vec(ffn1_norm[l]), ffn1_w_gate[l], ffn1_w_up[l], ffn1_w_down[l],
                                     side=(w_in[l], w_out[l]))
        qa, ka, v, sgu = _mix(xt, vec(mix_norm[l]), w_in_bf, vec(q_norm[l]), vec(k_norm[l]),
                              vec(sgu_ln_gain[l]), vec(sgu_ln_bias[l]), sgu_w_spatial[l], sgu_b_spatial[l].T,
                              vec(sgu_out_gain[l]), seq)
        attn = _attention(qa, ka, v, batch, seq)
        xt = _outproj(attn, sgu, xt, vec(attn_out_gain[l]), w_out_bf)
        xt = _ffn(xt, vec(ffn2_norm[l]), ffn2_w_gate[l], ffn2_w_up[l], ffn2_w_down[l])
    return xt.reshape(batch, seq, d_model)
```

```python
import functools

import jax
import jax.numpy as jnp
from jax import lax
from jax.experimental import pallas as pl
from jax.experimental.pallas import tpu as pltpu

D_MODEL = 2048
D_FF = 5632
ATTN_HEADS = 8
HEAD_DIM = 128
D_ATTN = ATTN_HEADS * HEAD_DIM
SGU_GROUPS = 8
SGU_GROUP_DIM = 128
D_SGU = SGU_GROUPS * SGU_GROUP_DIM
MOBA_BLOCK = 256
MOBA_TOPK = 3
SGU_CHUNK = 128
EPS = 1e-6
NEG_INF = -1e30

LANES = 128
AUG_DIM = 2 * HEAD_DIM
VMEM_LIMIT_BYTES = 60 * 1024 * 1024

FFN_TM = 1024
FFN_HEAD_TF = 256
FFN_TF = 512
PROJ_TM = 512
ATTN_TQ = 512
ATTN_KV_CHUNK = 512
ATTN_HEADS_PER_STEP = 2

F32 = jnp.float32
BF16 = jnp.bfloat16


def _rmsnorm(x, gain):
    return x * lax.rsqrt(jnp.mean(x * x, axis=-1, keepdims=True) + EPS) * gain


def _gelu_tanh(x):
    cdf = 0.5 * (1.0 + jnp.tanh(0.7978845608028654 * (x + 0.044715 * (x * x * x))))
    return x * cdf


def _dot(a, b):
    return jnp.dot(a, b, preferred_element_type=F32)


def _dot_nt(a, b):
    return lax.dot_general(a, b, (((1,), (1,)), ((), ())), preferred_element_type=F32)


def _dot_nt_3x(a, b):
    a_hi, b_hi = a.astype(BF16), b.astype(BF16)
    a_lo = (a - a_hi.astype(F32)).astype(BF16)
    b_lo = (b - b_hi.astype(F32)).astype(BF16)
    return _dot_nt(a_hi, b_hi) + (_dot_nt(a_hi, b_lo) + _dot_nt(a_lo, b_hi))


def _params(*semantics):
    return pltpu.CompilerParams(dimension_semantics=semantics, vmem_limit_bytes=VMEM_LIMIT_BYTES)


def _resident(shape):
    return pl.BlockSpec(shape, lambda *_: (0,) * len(shape), pipeline_mode=pl.Buffered(1))


def _ffn_accumulate(h_scr, wg, wu, wd, base_ref, o_ref):
    h = h_scr[...]
    acts = []
    for b in range(wg.shape[0]):
        gate = _dot(h, wg[b])
        up = _dot(h, wu[b])
        acts.append(((gate / (1.0 + jnp.exp(-gate))) * up * 0.5).astype(BF16))
    o_ref[...] = base_ref[...] + _dot(jnp.concatenate(acts, axis=1), wd)


def _ffn_step(first, x_ref, g_ref, h_scr, wg, wu, wd, o_ref):
    if first:
        h_scr[...] = _rmsnorm(x_ref[...], g_ref[...]).astype(BF16)
    _ffn_accumulate(h_scr, wg, wu, wd, x_ref if first else o_ref, o_ref)


def _ffn_head_body(x_ref, g_ref, wg_ref, wu_ref, wd_ref, o_ref, wg_bf_ref, wu_bf_ref, wd_bf_ref, h_scr):
    @pl.when(pl.program_id(0) == 0)
    def _():
        x = x_ref[...]
        h_scr[...] = _rmsnorm(x, g_ref[...]).astype(BF16)
        o_ref[...] = x

    wg_bf_ref[0] = wg_ref[...].astype(BF16)
    wu_bf_ref[0] = wu_ref[...].astype(BF16)
    wd_bf_ref[...] = wd_ref[...].astype(BF16)
    _ffn_accumulate(h_scr, wg_bf_ref, wu_bf_ref, wd_bf_ref[...], o_ref, o_ref)


def _ffn_tail_body(x_ref, g_ref, wg_ref, wu_ref, wd_ref, head_hbm, *rest, side_windows):
    n_side = len(side_windows)
    side_in, (o_ref, *side_out), h_scr = rest[:n_side], rest[n_side:-1], rest[-1]
    i = pl.program_id(0)
    j = pl.program_id(1)

    step = i * pl.num_programs(1) + j
    for src, dst, (start, n) in zip(side_in, side_out, side_windows):
        @pl.when((step >= start) & (step < start + n))
        def _(src=src, dst=dst):
            dst[...] = src[...].astype(BF16)

    @pl.when((i == 0) & (j == 0))
    def _():
        pltpu.sync_copy(head_hbm, o_ref)

    for first in (True, False):
        @pl.when((i > 0) & ((j == 0) if first else (j > 0)))
        def _(first=first):
            _ffn_step(first, x_ref, g_ref, h_scr, wg_ref, wu_ref, wd_ref[...], o_ref)


def _ffn(x, gain, w_gate, w_up, w_down, side=()):
    tokens = x.shape[0]
    n_tiles = tokens // FFN_TM
    vec = pl.BlockSpec((1, D_MODEL), lambda *_: (0, 0))

    tf = FFN_HEAD_TF
    col = pl.BlockSpec((D_MODEL, tf), lambda j: (0, j))
    colblk = pl.BlockSpec((1, D_MODEL, tf), lambda j: (j, 0, 0))
    rowblk = pl.BlockSpec((tf, D_MODEL), lambda j: (j, 0))
    tile0 = pl.BlockSpec((FFN_TM, D_MODEL), lambda j: (0, 0))
    head, wg_bf, wu_bf, wd_bf = pl.pallas_call(
        _ffn_head_body,
        grid=(D_FF // tf,),
        in_specs=[tile0, vec, col, col, rowblk],
        out_specs=(tile0, colblk, colblk, rowblk),
        out_shape=(jax.ShapeDtypeStruct((FFN_TM, D_MODEL), F32),
                   jax.ShapeDtypeStruct((D_FF // tf, D_MODEL, tf), BF16),
                   jax.ShapeDtypeStruct((D_FF // tf, D_MODEL, tf), BF16),
                   jax.ShapeDtypeStruct((D_FF, D_MODEL), BF16)),
        scratch_shapes=[pltpu.VMEM((FFN_TM, D_MODEL), BF16)],
        compiler_params=_params("arbitrary"),
        name="ffn_head",
    )(x, gain, w_gate, w_up, w_down)

    tf = FFN_TF
    wcol = pl.BlockSpec((tf // FFN_HEAD_TF, D_MODEL, FFN_HEAD_TF), lambda i, j: (jnp.where(i == 0, 0, j), 0, 0))
    wrow = lambda i, j: (jnp.where(i == 0, 0, j), 0)
    row = pl.BlockSpec((FFN_TM, D_MODEL), lambda i, j: (i, 0))
    n_steps = D_FF // tf
    side_specs, side_windows, start = [], [], 0
    for w in side:
        n_side_tiles = w.shape[1] // LANES
        side_specs.append(pl.BlockSpec(
            (w.shape[0], LANES),
            lambda i, j, start=start, last=n_side_tiles - 1: (0, jnp.clip(i * n_steps + j - start, 0, last))))
        side_windows.append((start, n_side_tiles))
        start += n_side_tiles
    assert start <= n_tiles * n_steps
    outs = pl.pallas_call(
        functools.partial(_ffn_tail_body, side_windows=tuple(side_windows)),
        grid=(n_tiles, n_steps),
        in_specs=[row, vec, wcol, wcol,
                  pl.BlockSpec((tf, D_MODEL), wrow), pl.BlockSpec(memory_space=pl.ANY), *side_specs],
        out_specs=(row, *side_specs),
        out_shape=(jax.ShapeDtypeStruct((tokens, D_MODEL), F32),
                   *(jax.ShapeDtypeStruct(w.shape, BF16) for w in side)),
        scratch_shapes=[pltpu.VMEM((FFN_TM, D_MODEL), BF16)],
        compiler_params=_params("arbitrary", "arbitrary"),
        name="ffn_tail",
    )(x, gain, wg_bf, wu_bf, wd_bf, head, *side)
    return outs if side else outs[0]


def _moba_gate(n0, q, k, gq_ref, gk_ref, qa_ref, ka_ref, km_scr):
    tm = PROJ_TM
    blocks_per_batch = km_scr.shape[0]
    scale = HEAD_DIM ** -0.5
    blk_rows = lax.broadcasted_iota(jnp.int32, (blocks_per_batch, HEAD_DIM), 0)
    n_iota = lax.broadcasted_iota(jnp.int32, (blocks_per_batch, tm), 0)
    lane_tok = lax.broadcasted_iota(jnp.int32, (blocks_per_batch, tm), 1)
    q_blk = n0 + lane_tok // MOBA_BLOCK
    valid = n_iota < q_blk
    tok_blk = n0 + lax.broadcasted_iota(jnp.int32, (tm, LANES), 0) // MOBA_BLOCK
    onehot = jnp.where(lax.broadcasted_iota(jnp.int32, (tm, LANES), 1) == tok_blk, 1.0, 0.0).astype(BF16)
    pad_rows = jnp.zeros((LANES - blocks_per_batch, tm), F32)

    for hh in range(ATTN_HEADS):
        sl = slice(hh * HEAD_DIM, (hh + 1) * HEAD_DIM)
        qn = _rmsnorm(q[:, sl], gq_ref[...])
        kn = _rmsnorm(k[:, sl], gk_ref[...])

        kmean = km_scr[:, sl]
        for b in range(tm // MOBA_BLOCK):
            km = jnp.mean(kn[b * MOBA_BLOCK:(b + 1) * MOBA_BLOCK], axis=0, keepdims=True)
            kmean = jnp.where(blk_rows == n0 + b, km, kmean)
        km_scr[:, sl] = kmean

        gate = _dot_nt_3x(kmean, qn)
        cnt = jnp.zeros((blocks_per_batch, tm), jnp.int32)
        for m in range(blocks_per_batch - 1):
            gm = gate[m:m + 1, :]
            beats = (gm > gate) | ((gm == gate) & (m < n_iota))
            cnt = cnt + jnp.where(beats & (m < q_blk), 1, 0)
        sel = ((cnt < MOBA_TOPK) & valid) | (n_iota == q_blk)
        bias_t = jnp.concatenate([jnp.where(sel, 0.0, NEG_INF), pad_rows], axis=0)
        bias = bias_t.T

        a0 = hh * AUG_DIM
        qa_ref[:, a0:a0 + HEAD_DIM] = (qn * scale).astype(BF16)
        qa_ref[:, a0 + HEAD_DIM:a0 + AUG_DIM] = bias.astype(BF16)
        ka_ref[:, a0:a0 + HEAD_DIM] = kn.astype(BF16)
        ka_ref[:, a0 + HEAD_DIM:a0 + AUG_DIM] = onehot


def _spatial_gate(u, g, lng_ref, lnb_ref, ws_ref, bs_ref, og_ref, o_ref, sg_scr):
    tm = PROJ_TM
    r = lax.broadcasted_iota(jnp.int32, (SGU_CHUNK, SGU_CHUNK), 0)
    c = lax.broadcasted_iota(jnp.int32, (SGU_CHUNK, SGU_CHUNK), 1)
    for gi in range(SGU_GROUPS):
        sl = slice(gi * SGU_GROUP_DIM, (gi + 1) * SGU_GROUP_DIM)
        gg = g[:, sl]
        d = gg - jnp.mean(gg, axis=-1, keepdims=True)
        var = jnp.mean(d * d, axis=-1, keepdims=True)
        gn = (d * lax.rsqrt(var + EPS) * lng_ref[:, sl] + lnb_ref[:, sl]).astype(BF16)
        w_causal = jnp.where(r >= c, ws_ref[gi], 0.0).astype(BF16)
        b_col = bs_ref[:, gi:gi + 1]
        for ci in range(tm // SGU_CHUNK):
            rows = slice(ci * SGU_CHUNK, (ci + 1) * SGU_CHUNK)
            mixed = _dot(w_causal, gn[rows]) + b_col
            sg_scr[rows, sl] = u[rows, sl] * mixed
    o_ref[...] = _rmsnorm(sg_scr[...], og_ref[...]).astype(BF16)


def _mix_body(x_ref, gx_ref, w_ref, gq_ref, gk_ref, lng_ref, lnb_ref, ws_ref, bs_ref, og_ref,
              qa_ref, ka_ref, v_ref, o_ref, km_scr, sg_scr, *, tiles_per_batch):
    i = pl.program_id(0)
    n0 = (i % tiles_per_batch) * (PROJ_TM // MOBA_BLOCK)

    @pl.when(i == 0)
    def _init():
        km_scr[...] = jnp.zeros_like(km_scr)

    h = _rmsnorm(x_ref[...], gx_ref[...]).astype(BF16)
    col = lambda c0, width: w_ref[:, c0:c0 + width]
    q = _dot(h, col(0, D_ATTN))
    k = _dot(h, col(D_ATTN, D_ATTN))
    u = _gelu_tanh(_dot(h, col(3 * D_ATTN, D_SGU)))
    _moba_gate(n0, q, k, gq_ref, gk_ref, qa_ref, ka_ref, km_scr)
    g = _gelu_tanh(_dot(h, col(3 * D_ATTN + D_SGU, D_SGU)))
    v_ref[...] = _dot(h, col(2 * D_ATTN, D_ATTN)).astype(BF16)
    _spatial_gate(u, g, lng_ref, lnb_ref, ws_ref, bs_ref, og_ref, o_ref, sg_scr)


def _mix(x, x_gain, w_in, q_gain, k_gain, ln_gain, ln_bias, w_spatial, b_spatial_t, out_gain, seq):
    tokens = x.shape[0]
    tm = PROJ_TM
    tiles_per_batch = seq // tm
    blocks_per_batch = seq // MOBA_BLOCK
    assert blocks_per_batch <= LANES and blocks_per_batch % 8 == 0
    row = lambda width: pl.BlockSpec((tm, width), lambda i: (i, 0))
    return pl.pallas_call(
        functools.partial(_mix_body, tiles_per_batch=tiles_per_batch),
        grid=(tokens // tm,),
        in_specs=[row(D_MODEL), _resident((1, D_MODEL)), _resident((D_MODEL, 3 * D_ATTN + 2 * D_SGU)),
                  _resident((1, HEAD_DIM)), _resident((1, HEAD_DIM)),
                  _resident((1, D_SGU)), _resident((1, D_SGU)),
                  _resident((SGU_GROUPS, SGU_CHUNK, SGU_CHUNK)), _resident((SGU_CHUNK, SGU_GROUPS)),
                  _resident((1, D_SGU))],
        out_specs=(row(ATTN_HEADS * AUG_DIM), row(ATTN_HEADS * AUG_DIM), row(D_ATTN), row(D_SGU)),
        out_shape=(jax.ShapeDtypeStruct((tokens, ATTN_HEADS * AUG_DIM), BF16),
                   jax.ShapeDtypeStruct((tokens, ATTN_HEADS * AUG_DIM), BF16),
                   jax.ShapeDtypeStruct((tokens, D_ATTN), BF16),
                   jax.ShapeDtypeStruct((tokens, D_SGU), BF16)),
        scratch_shapes=[pltpu.VMEM((blocks_per_batch, D_ATTN), F32), pltpu.VMEM((tm, D_SGU), F32)],
        compiler_params=_params("arbitrary"),
        name="mix_proj",
    )(x, x_gain, w_in, q_gain, k_gain, ln_gain, ln_bias, w_spatial, b_spatial_t, out_gain)


def _attn_scores(t, q_ref, k_ref, s_ref, m_ref):
    tq, kc = ATTN_TQ, ATTN_KV_CHUNK
    half = MOBA_BLOCK
    m_part = None

    def fold(m_part, s):
        for j in range(s.shape[1] // LANES):
            sj = s[:, j * LANES:(j + 1) * LANES]
            m_part = sj if m_part is None else jnp.maximum(m_part, sj)
        return m_part

    for c in range(t):
        s = _dot_nt(q_ref[...], k_ref[c * kc:(c + 1) * kc, :])
        s_ref[:, c * kc:(c + 1) * kc] = s
        m_part = fold(m_part, s)
    k0 = t * kc
    r = lax.broadcasted_iota(jnp.int32, (half, half), 0)
    col = lax.broadcasted_iota(jnp.int32, (half, half), 1)
    s_top = jnp.where(col <= r, _dot_nt(q_ref[0:half, :], k_ref[k0:k0 + half, :]), NEG_INF)
    s_ref[0:half, k0:k0 + half] = s_top
    r2 = lax.broadcasted_iota(jnp.int32, (half, kc), 0) + half
    col2 = lax.broadcasted_iota(jnp.int32, (half, kc), 1)
    s_bot = jnp.where(col2 <= r2, _dot_nt(q_ref[half:tq, :], k_ref[k0:k0 + kc, :]), NEG_INF)
    s_ref[half:tq, k0:k0 + kc] = s_bot
    m_top = fold(None if m_part is None else m_part[0:half], s_top)
    m_bot = fold(None if m_part is None else m_part[half:tq], s_bot)
    m_part = jnp.concatenate([m_top, m_bot], axis=0)
    m_ref[...] = jnp.broadcast_to(jnp.max(m_part, axis=-1, keepdims=True), (tq, LANES))


def _attn_values(t, va_ref, o_ref, s_ref, m_ref):
    tq, kc = ATTN_TQ, ATTN_KV_CHUNK
    half = MOBA_BLOCK
    m = m_ref[...]

    def probs(rows, c0, width):
        return jnp.concatenate(
            [jnp.exp(s_ref[rows, c0 + j * LANES:c0 + (j + 1) * LANES] - m[rows]).astype(BF16)
             for j in range(width // LANES)], axis=1)

    acc = jnp.zeros((tq, AUG_DIM), F32)
    for c in range(t):
        acc = acc + _dot(probs(slice(0, tq), c * kc, kc), va_ref[c * kc:(c + 1) * kc, :])
    k0 = t * kc
    top = acc[0:half] + _dot(probs(slice(0, half), k0, half), va_ref[k0:k0 + half, :])
    bot = acc[half:tq] + _dot(probs(slice(half, tq), k0, kc), va_ref[k0:k0 + kc, :])
    acc = jnp.concatenate([top, bot], axis=0)
    o_ref[...] = acc[:, 0:HEAD_DIM] / acc[:, HEAD_DIM:AUG_DIM]


def _attn_body(q_ref, k_ref, v_ref, o_ref, s_scr, m_scr, va_scr):
    n_tiles = k_ref.shape[0] // ATTN_TQ
    tile = lambda t: slice(t * ATTN_TQ, (t + 1) * ATTN_TQ)
    for hh in range(ATTN_HEADS_PER_STEP):
        va_scr[hh, :, 0:HEAD_DIM] = v_ref[:, hh * HEAD_DIM:(hh + 1) * HEAD_DIM]
        va_scr[hh, :, HEAD_DIM:AUG_DIM] = jnp.ones((va_scr.shape[1], HEAD_DIM), BF16)
    work = [(hh, t) for hh in range(ATTN_HEADS_PER_STEP) for t in range(n_tiles)]

    def sub_step(idx):
        if idx >= 1:
            hh, t = work[idx - 1]
            _attn_values(t, va_scr.at[hh], o_ref.at[tile(t), hh * HEAD_DIM:(hh + 1) * HEAD_DIM],
                         s_scr.at[(idx - 1) % 2], m_scr.at[(idx - 1) % 2])
        if idx < len(work):
            hh, t = work[idx]
            _attn_scores(t, q_ref.at[tile(t), hh * AUG_DIM:(hh + 1) * AUG_DIM],
                         k_ref.at[:, hh * AUG_DIM:(hh + 1) * AUG_DIM], s_scr.at[idx % 2], m_scr.at[idx % 2])

    for idx in range(len(work) + 1):
        pl.when(pl.program_id(1) + idx >= 0)(functools.partial(sub_step, idx))


def _attention(qa, ka, v, batch, seq):
    tokens = batch * seq
    hp = ATTN_HEADS_PER_STEP
    return pl.pallas_call(
        _attn_body,
        grid=(batch, ATTN_HEADS // hp),
        in_specs=[pl.BlockSpec((seq, hp * AUG_DIM), lambda b, h: (b, h)),
                  pl.BlockSpec((seq, hp * AUG_DIM), lambda b, h: (b, h)),
                  pl.BlockSpec((seq, hp * HEAD_DIM), lambda b, h: (b, h))],
        out_specs=pl.BlockSpec((seq, hp * HEAD_DIM), lambda b, h: (b, h)),
        out_shape=jax.ShapeDtypeStruct((tokens, D_ATTN), F32),
        scratch_shapes=[pltpu.VMEM((2, ATTN_TQ, seq), F32), pltpu.VMEM((2, ATTN_TQ, LANES), F32),
                        pltpu.VMEM((hp, seq, AUG_DIM), BF16)],
        compiler_params=_params("parallel", "parallel"),
        name="moba_attn",
    )(qa, ka, v)


def _outproj_body(a_ref, s_ref, x_ref, ga_ref, w_ref, o_ref):
    from_sgu = _dot(s_ref[...], w_ref[D_ATTN:D_ATTN + D_SGU, :])
    an = _rmsnorm(a_ref[...], ga_ref[...]).astype(BF16)
    o_ref[...] = x_ref[...] + _dot(an, w_ref[0:D_ATTN, :]) + from_sgu


def _outproj(attn, sgu, x, attn_gain, w_out):
    tokens = x.shape[0]
    tm = PROJ_TM
    row = lambda width: pl.BlockSpec((tm, width), lambda i: (i, 0))
    return pl.pallas_call(
        _outproj_body,
        grid=(tokens // tm,),
        in_specs=[row(D_ATTN), row(D_SGU), row(D_MODEL), _resident((1, D_ATTN)),
                  _resident((D_ATTN + D_SGU, D_MODEL))],
        out_specs=row(D_MODEL),
        out_shape=jax.ShapeDtypeStruct((tokens, D_MODEL), F32),
        compiler_params=_params("parallel"),
        name="outproj",
    )(attn, sgu, x, attn_gain, w_out)


def kernel(x, ffn1_norm, ffn1_w_gate, ffn1_w_up, ffn1_w_down, mix_norm, w_in, q_norm, k_norm, sgu_ln_gain, sgu_ln_bias, sgu_w_spatial, sgu_b_spatial, attn_out_gain, sgu_out_gain, w_out, ffn2_norm, ffn2_w_gate, ffn2_w_up, ffn2_w_down):
    batch, seq, d_model = x.shape
    depth = ffn1_norm.shape[0]
    assert d_model == D_MODEL and seq % PROJ_TM == 0 and (batch * seq) % FFN_TM == 0
    xt = x.reshape(batch * seq, d_model)
    vec = lambda a: a.reshape(1, -1)
    for l in range(depth):
        xt, w_in_bf, w_out_bf = _ffn(xt, vec(ffn1_norm[l]), ffn1_w_gate[l], ffn1_w_up[l], ffn1_w_down[l],
                                     side=(w_in[l], w_out[l]))
        qa, ka, v, sgu = _mix(xt, vec(mix_norm[l]), w_in_bf, vec(q_norm[l]), vec(k_norm[l]),
                              vec(sgu_ln_gain[l]), vec(sgu_ln_bias[l]), sgu_w_spatial[l], sgu_b_spatial[l].T,
                              vec(sgu_out_gain[l]), seq)
        attn = _attention(qa, ka, v, batch, seq)
        xt = _outproj(attn, sgu, xt, vec(attn_out_gain[l]), w_out_bf)
        xt = _ffn(xt, vec(ffn2_norm[l]), ffn2_w_gate[l], ffn2_w_up[l], ffn2_w_down[l])
    return xt.reshape(batch, seq, d_model)
```

```python
import functools

import jax
import jax.numpy as jnp
from jax import lax
from jax.experimental import pallas as pl
from jax.experimental.pallas import tpu as pltpu

D_MODEL = 2048
D_FF = 5632
ATTN_HEADS = 8
HEAD_DIM = 128
D_ATTN = ATTN_HEADS * HEAD_DIM
SGU_GROUPS = 8
SGU_GROUP_DIM = 128
D_SGU = SGU_GROUPS * SGU_GROUP_DIM
MOBA_BLOCK = 256
MOBA_TOPK = 3
SGU_CHUNK = 128
EPS = 1e-6
NEG_INF = -1e30

LANES = 128
AUG_DIM = 2 * HEAD_DIM
VMEM_LIMIT_BYTES = 60 * 1024 * 1024

FFN_TM = 1024
FFN_HEAD_TF = 256
FFN_TF = 512
PROJ_TM = 512
ATTN_TQ = 512
ATTN_KV_CHUNK = 512

F32 = jnp.float32
BF16 = jnp.bfloat16


def _rmsnorm(x, gain):
    return x * lax.rsqrt(jnp.mean(x * x, axis=-1, keepdims=True) + EPS) * gain


def _gelu_tanh(x):
    cdf = 0.5 * (1.0 + jnp.tanh(0.7978845608028654 * (x + 0.044715 * (x * x * x))))
    return x * cdf


def _dot(a, b):
    return jnp.dot(a, b, preferred_element_type=F32)


def _dot_nt(a, b):
    return lax.dot_general(a, b, (((1,), (1,)), ((), ())), preferred_element_type=F32)


def _dot_nt_3x(a, b):
    a_hi, b_hi = a.astype(BF16), b.astype(BF16)
    a_lo = (a - a_hi.astype(F32)).astype(BF16)
    b_lo = (b - b_hi.astype(F32)).astype(BF16)
    return _dot_nt(a_hi, b_hi) + (_dot_nt(a_hi, b_lo) + _dot_nt(a_lo, b_hi))


def _params(*semantics):
    return pltpu.CompilerParams(dimension_semantics=semantics, vmem_limit_bytes=VMEM_LIMIT_BYTES)


def _resident(shape):
    return pl.BlockSpec(shape, lambda *_: (0,) * len(shape), pipeline_mode=pl.Buffered(1))


def _ffn_accumulate(h_scr, wg, wu, wd, base_ref, o_ref):
    h = h_scr[...]
    acts = []
    for b in range(wg.shape[0]):
        gate = _dot(h, wg[b])
        up = _dot(h, wu[b])
        acts.append(((gate / (1.0 + jnp.exp(-gate))) * up * 0.5).astype(BF16))
    o_ref[...] = base_ref[...] + _dot(jnp.concatenate(acts, axis=1), wd)


def _ffn_step(first, x_ref, g_ref, h_scr, wg, wu, wd, o_ref):
    if first:
        h_scr[...] = _rmsnorm(x_ref[...], g_ref[...]).astype(BF16)
    _ffn_accumulate(h_scr, wg, wu, wd, x_ref if first else o_ref, o_ref)


def _ffn_head_body(x_ref, g_ref, wg_ref, wu_ref, wd_ref, o_ref, wg_bf_ref, wu_bf_ref, wd_bf_ref, h_scr):
    @pl.when(pl.program_id(0) == 0)
    def _():
        x = x_ref[...]
        h_scr[...] = _rmsnorm(x, g_ref[...]).astype(BF16)
        o_ref[...] = x

    wg_bf_ref[0] = wg_ref[...].astype(BF16)
    wu_bf_ref[0] = wu_ref[...].astype(BF16)
    wd_bf_ref[...] = wd_ref[...].astype(BF16)
    _ffn_accumulate(h_scr, wg_bf_ref, wu_bf_ref, wd_bf_ref[...], o_ref, o_ref)


def _ffn_tail_body(x_ref, g_ref, wg_ref, wu_ref, wd_ref, head_hbm, *rest, side_windows):
    n_side = len(side_windows)
    side_in, (o_ref, *side_out), h_scr, copy_sem = rest[:n_side], rest[n_side:-2], rest[-2], rest[-1]
    i = pl.program_id(0)
    j = pl.program_id(1)
    head_copy = pltpu.make_async_copy(head_hbm, o_ref, copy_sem)

    step = i * pl.num_programs(1) + j
    for src, dst, (start, n) in zip(side_in, side_out, side_windows):
        @pl.when((step >= start) & (step < start + n))
        def _(src=src, dst=dst):
            dst[...] = src[...].astype(BF16)

    @pl.when((i == 0) & (j == 0))
    def _():
        head_copy.start()

    @pl.when((i == 0) & (j == pl.num_programs(1) - 1))
    def _():
        head_copy.wait()

    for first in (True, False):
        @pl.when((i > 0) & ((j == 0) if first else (j > 0)))
        def _(first=first):
            _ffn_step(first, x_ref, g_ref, h_scr, wg_ref, wu_ref, wd_ref[...], o_ref)


def _ffn(x, gain, w_gate, w_up, w_down, side=()):
    tokens = x.shape[0]
    n_tiles = tokens // FFN_TM
    vec = pl.BlockSpec((1, D_MODEL), lambda *_: (0, 0))

    tf = FFN_HEAD_TF
    col = pl.BlockSpec((D_MODEL, tf), lambda j: (0, j))
    colblk = pl.BlockSpec((1, D_MODEL, tf), lambda j: (j, 0, 0))
    rowblk = pl.BlockSpec((tf, D_MODEL), lambda j: (j, 0))
    tile0 = pl.BlockSpec((FFN_TM, D_MODEL), lambda j: (0, 0))
    head, wg_bf, wu_bf, wd_bf = pl.pallas_call(
        _ffn_head_body,
        grid=(D_FF // tf,),
        in_specs=[tile0, vec, col, col, rowblk],
        out_specs=(tile0, colblk, colblk, rowblk),
        out_shape=(jax.ShapeDtypeStruct((FFN_TM, D_MODEL), F32),
                   jax.ShapeDtypeStruct((D_FF // tf, D_MODEL, tf), BF16),
                   jax.ShapeDtypeStruct((D_FF // tf, D_MODEL, tf), BF16),
                   jax.ShapeDtypeStruct((D_FF, D_MODEL), BF16)),
        scratch_shapes=[pltpu.VMEM((FFN_TM, D_MODEL), BF16)],
        compiler_params=_params("arbitrary"),
        name="ffn_head",
    )(x, gain, w_gate, w_up, w_down)

    tf = FFN_TF
    wcol = pl.BlockSpec((tf // FFN_HEAD_TF, D_MODEL, FFN_HEAD_TF), lambda i, j: (jnp.where(i == 0, 0, j), 0, 0))
    wrow = lambda i, j: (jnp.where(i == 0, 0, j), 0)
    row = pl.BlockSpec((FFN_TM, D_MODEL), lambda i, j: (i, 0))
    n_steps = D_FF // tf
    side_specs, side_windows, start = [], [], 0
    for w in side:
        n_side_tiles = w.shape[1] // LANES
        side_specs.append(pl.BlockSpec(
            (w.shape[0], LANES),
            lambda i, j, start=start, last=n_side_tiles - 1: (0, jnp.clip(i * n_steps + j - start, 0, last))))
        side_windows.append((start, n_side_tiles))
        start += n_side_tiles
    assert start <= n_tiles * n_steps
    outs = pl.pallas_call(
        functools.partial(_ffn_tail_body, side_windows=tuple(side_windows)),
        grid=(n_tiles, n_steps),
        in_specs=[row, vec, wcol, wcol,
                  pl.BlockSpec((tf, D_MODEL), wrow), pl.BlockSpec(memory_space=pl.ANY), *side_specs],
        out_specs=(row, *side_specs),
        out_shape=(jax.ShapeDtypeStruct((tokens, D_MODEL), F32),
                   *(jax.ShapeDtypeStruct(w.shape, BF16) for w in side)),
        scratch_shapes=[pltpu.VMEM((FFN_TM, D_MODEL), BF16), pltpu.SemaphoreType.DMA(())],
        compiler_params=_params("arbitrary", "arbitrary"),
        name="ffn_tail",
    )(x, gain, wg_bf, wu_bf, wd_bf, head, *side)
    return outs if side else outs[0]


def _moba_gate(n0, q, k, gq_ref, gk_ref, qa_ref, ka_ref, km_scr):
    tm = PROJ_TM
    blocks_per_batch = km_scr.shape[0]
    scale = HEAD_DIM ** -0.5
    blk_rows = lax.broadcasted_iota(jnp.int32, (blocks_per_batch, HEAD_DIM), 0)
    n_iota = lax.broadcasted_iota(jnp.int32, (blocks_per_batch, tm), 0)
    lane_tok = lax.broadcasted_iota(jnp.int32, (blocks_per_batch, tm), 1)
    q_blk = n0 + lane_tok // MOBA_BLOCK
    valid = n_iota < q_blk
    tok_blk = n0 + lax.broadcasted_iota(jnp.int32, (tm, LANES), 0) // MOBA_BLOCK
    onehot = jnp.where(lax.broadcasted_iota(jnp.int32, (tm, LANES), 1) == tok_blk, 1.0, 0.0).astype(BF16)
    pad_rows = jnp.zeros((LANES - blocks_per_batch, tm), F32)

    for hh in range(ATTN_HEADS):
        sl = slice(hh * HEAD_DIM, (hh + 1) * HEAD_DIM)
        qn = _rmsnorm(q[:, sl], gq_ref[...])
        kn = _rmsnorm(k[:, sl], gk_ref[...])

        kmean = km_scr[:, sl]
        for b in range(tm // MOBA_BLOCK):
            km = jnp.mean(kn[b * MOBA_BLOCK:(b + 1) * MOBA_BLOCK], axis=0, keepdims=True)
            kmean = jnp.where(blk_rows == n0 + b, km, kmean)
        km_scr[:, sl] = kmean

        gate = _dot_nt_3x(kmean, qn)
        cnt = jnp.zeros((blocks_per_batch, tm), jnp.int32)
        for m in range(blocks_per_batch - 1):
            gm = gate[m:m + 1, :]
            beats = (gm > gate) | ((gm == gate) & (m < n_iota))
            cnt = cnt + jnp.where(beats & (m < q_blk), 1, 0)
        sel = ((cnt < MOBA_TOPK) & valid) | (n_iota == q_blk)
        bias_t = jnp.concatenate([jnp.where(sel, 0.0, NEG_INF), pad_rows], axis=0)
        bias = bias_t.T

        a0 = hh * AUG_DIM
        qa_ref[:, a0:a0 + HEAD_DIM] = (qn * scale).astype(BF16)
        qa_ref[:, a0 + HEAD_DIM:a0 + AUG_DIM] = bias.astype(BF16)
        ka_ref[:, a0:a0 + HEAD_DIM] = kn.astype(BF16)
        ka_ref[:, a0 + HEAD_DIM:a0 + AUG_DIM] = onehot


def _spatial_gate(u, g, lng_ref, lnb_ref, ws_ref, bs_ref, og_ref, o_ref, sg_scr):
    tm = PROJ_TM
    r = lax.broadcasted_iota(jnp.int32, (SGU_CHUNK, SGU_CHUNK), 0)
    c = lax.broadcasted_iota(jnp.int32, (SGU_CHUNK, SGU_CHUNK), 1)
    for gi in range(SGU_GROUPS):
        sl = slice(gi * SGU_GROUP_DIM, (gi + 1) * SGU_GROUP_DIM)
        gg = g[:, sl]
        d = gg - jnp.mean(gg, axis=-1, keepdims=True)
        var = jnp.mean(d * d, axis=-1, keepdims=True)
        gn = (d * lax.rsqrt(var + EPS) * lng_ref[:, sl] + lnb_ref[:, sl]).astype(BF16)
        w_causal = jnp.where(r >= c, ws_ref[gi], 0.0).astype(BF16)
        b_col = bs_ref[:, gi:gi + 1]
        for ci in range(tm // SGU_CHUNK):
            rows = slice(ci * SGU_CHUNK, (ci + 1) * SGU_CHUNK)
            mixed = _dot(w_causal, gn[rows]) + b_col
            sg_scr[rows, sl] = u[rows, sl] * mixed
    o_ref[...] = _rmsnorm(sg_scr[...], og_ref[...]).astype(BF16)


def _mix_body(x_ref, gx_ref, w_ref, gq_ref, gk_ref, lng_ref, lnb_ref, ws_ref, bs_ref, og_ref,
              qa_ref, ka_ref, v_ref, o_ref, km_scr, sg_scr, *, tiles_per_batch):
    i = pl.program_id(0)
    n0 = (i % tiles_per_batch) * (PROJ_TM // MOBA_BLOCK)

    @pl.when(i == 0)
    def _init():
        km_scr[...] = jnp.zeros_like(km_scr)

    h = _rmsnorm(x_ref[...], gx_ref[...]).astype(BF16)
    col = lambda c0, width: w_ref[:, c0:c0 + width]
    q = _dot(h, col(0, D_ATTN))
    k = _dot(h, col(D_ATTN, D_ATTN))
    u = _gelu_tanh(_dot(h, col(3 * D_ATTN, D_SGU)))
    _moba_gate(n0, q, k, gq_ref, gk_ref, qa_ref, ka_ref, km_scr)
    g = _gelu_tanh(_dot(h, col(3 * D_ATTN + D_SGU, D_SGU)))
    v_ref[...] = _dot(h, col(2 * D_ATTN, D_ATTN)).astype(BF16)
    _spatial_gate(u, g, lng_ref, lnb_ref, ws_ref, bs_ref, og_ref, o_ref, sg_scr)


def _mix(x, x_gain, w_in, q_gain, k_gain, ln_gain, ln_bias, w_spatial, b_spatial_t, out_gain, seq):
    tokens = x.shape[0]
    tm = PROJ_TM
    tiles_per_batch = seq // tm
    blocks_per_batch = seq // MOBA_BLOCK
    assert blocks_per_batch <= LANES and blocks_per_batch % 8 == 0
    row = lambda width: pl.BlockSpec((tm, width), lambda i: (i, 0))
    return pl.pallas_call(
        functools.partial(_mix_body, tiles_per_batch=tiles_per_batch),
        grid=(tokens // tm,),
        in_specs=[row(D_MODEL), _resident((1, D_MODEL)), _resident((D_MODEL, 3 * D_ATTN + 2 * D_SGU)),
                  _resident((1, HEAD_DIM)), _resident((1, HEAD_DIM)),
                  _resident((1, D_SGU)), _resident((1, D_SGU)),
                  _resident((SGU_GROUPS, SGU_CHUNK, SGU_CHUNK)), _resident((SGU_CHUNK, SGU_GROUPS)),
                  _resident((1, D_SGU))],
        out_specs=(row(ATTN_HEADS * AUG_DIM), row(ATTN_HEADS * AUG_DIM), row(D_ATTN), row(D_SGU)),
        out_shape=(jax.ShapeDtypeStruct((tokens, ATTN_HEADS * AUG_DIM), BF16),
                   jax.ShapeDtypeStruct((tokens, ATTN_HEADS * AUG_DIM), BF16),
                   jax.ShapeDtypeStruct((tokens, D_ATTN), BF16),
                   jax.ShapeDtypeStruct((tokens, D_SGU), BF16)),
        scratch_shapes=[pltpu.VMEM((blocks_per_batch, D_ATTN), F32), pltpu.VMEM((tm, D_SGU), F32)],
        compiler_params=_params("arbitrary"),
        name="mix_proj",
    )(x, x_gain, w_in, q_gain, k_gain, ln_gain, ln_bias, w_spatial, b_spatial_t, out_gain)


def _attn_scores(t, q_ref, k_ref, s_ref, m_ref):
    tq, kc = ATTN_TQ, ATTN_KV_CHUNK
    half = MOBA_BLOCK
    m_part = None

    def fold(m_part, s):
        for j in range(s.shape[1] // LANES):
            sj = s[:, j * LANES:(j + 1) * LANES]
            m_part = sj if m_part is None else jnp.maximum(m_part, sj)
        return m_part

    for c in range(t):
        s = _dot_nt(q_ref[...], k_ref[c * kc:(c + 1) * kc, :])
        s_ref[:, c * kc:(c + 1) * kc] = s
        m_part = fold(m_part, s)
    k0 = t * kc
    r = lax.broadcasted_iota(jnp.int32, (half, half), 0)
    col = lax.broadcasted_iota(jnp.int32, (half, half), 1)
    s_top = jnp.where(col <= r, _dot_nt(q_ref[0:half, :], k_ref[k0:k0 + half, :]), NEG_INF)
    s_ref[0:half, k0:k0 + half] = s_top
    r2 = lax.broadcasted_iota(jnp.int32, (half, kc), 0) + half
    col2 = lax.broadcasted_iota(jnp.int32, (half, kc), 1)
    s_bot = jnp.where(col2 <= r2, _dot_nt(q_ref[half:tq, :], k_ref[k0:k0 + kc, :]), NEG_INF)
    s_ref[half:tq, k0:k0 + kc] = s_bot
    m_top = fold(None if m_part is None else m_part[0:half], s_top)
    m_bot = fold(None if m_part is None else m_part[half:tq], s_bot)
    m_part = jnp.concatenate([m_top, m_bot], axis=0)
    m_ref[...] = jnp.broadcast_to(jnp.max(m_part, axis=-1, keepdims=True), (tq, LANES))


def _attn_values(t, va_ref, o_ref, s_ref, m_ref):
    tq, kc = ATTN_TQ, ATTN_KV_CHUNK
    half = MOBA_BLOCK
    m = m_ref[...]

    def probs(rows, c0, width):
        return jnp.concatenate(
            [jnp.exp(s_ref[rows, c0 + j * LANES:c0 + (j + 1) * LANES] - m[rows]).astype(BF16)
             for j in range(width // LANES)], axis=1)

    acc = jnp.zeros((tq, AUG_DIM), F32)
    for c in range(t):
        acc = acc + _dot(probs(slice(0, tq), c * kc, kc), va_ref[c * kc:(c + 1) * kc, :])
    k0 = t * kc
    top = acc[0:half] + _dot(probs(slice(0, half), k0, half), va_ref[k0:k0 + half, :])
    bot = acc[half:tq] + _dot(probs(slice(half, tq), k0, kc), va_ref[k0:k0 + kc, :])
    acc = jnp.concatenate([top, bot], axis=0)
    o_ref[...] = acc[:, 0:HEAD_DIM] / acc[:, HEAD_DIM:AUG_DIM]


def _attn_body(q_ref, k_ref, v_ref, o_ref, s_scr, m_scr, va_scr):
    n_tiles = k_ref.shape[0] // ATTN_TQ
    tile = lambda t: slice(t * ATTN_TQ, (t + 1) * ATTN_TQ)
    va_scr[:, 0:HEAD_DIM] = v_ref[...]
    va_scr[:, HEAD_DIM:AUG_DIM] = jnp.ones((va_scr.shape[0], HEAD_DIM), BF16)

    def sub_step(ts):
        if ts >= 1:
            _attn_values(ts - 1, va_scr, o_ref.at[tile(ts - 1), :], s_scr.at[(ts - 1) % 2], m_scr.at[(ts - 1) % 2])
        if ts < n_tiles:
            _attn_scores(ts, q_ref.at[tile(ts), :], k_ref, s_scr.at[ts % 2], m_scr.at[ts % 2])

    for ts in range(n_tiles + 1):
        pl.when(pl.program_id(1) + ts >= 0)(functools.partial(sub_step, ts))


def _attention(qa, ka, v, batch, seq):
    tokens = batch * seq
    return pl.pallas_call(
        _attn_body,
        grid=(batch, ATTN_HEADS),
        in_specs=[pl.BlockSpec((seq, AUG_DIM), lambda b, h: (b, h)),
                  pl.BlockSpec((seq, AUG_DIM), lambda b, h: (b, h)),
                  pl.BlockSpec((seq, HEAD_DIM), lambda b, h: (b, h))],
        out_specs=pl.BlockSpec((seq, HEAD_DIM), lambda b, h: (b, h)),
        out_shape=jax.ShapeDtypeStruct((tokens, D_ATTN), F32),
        scratch_shapes=[pltpu.VMEM((2, ATTN_TQ, seq), F32), pltpu.VMEM((2, ATTN_TQ, LANES), F32),
                        pltpu.VMEM((seq, AUG_DIM), BF16)],
        compiler_params=_params("parallel", "parallel"),
        name="moba_attn",
    )(qa, ka, v)


def _outproj_body(a_ref, s_ref, x_ref, ga_ref, w_ref, o_ref):
    an = _rmsnorm(a_ref[...], ga_ref[...]).astype(BF16)
    o_ref[...] = (x_ref[...] + _dot(an, w_ref[0:D_ATTN, :])
                  + _dot(s_ref[...], w_ref[D_ATTN:D_ATTN + D_SGU, :]))


def _outproj(attn, sgu, x, attn_gain, w_out):
    tokens = x.shape[0]
    tm = PROJ_TM
    row = lambda width: pl.BlockSpec((tm, width), lambda i: (i, 0))
    return pl.pallas_call(
        _outproj_body,
        grid=(tokens // tm,),
        in_specs=[row(D_ATTN), row(D_SGU), row(D_MODEL), _resident((1, D_ATTN)),
                  _resident((D_ATTN + D_SGU, D_MODEL))],
        out_specs=row(D_MODEL),
        out_shape=jax.ShapeDtypeStruct((tokens, D_MODEL), F32),
        compiler_params=_params("parallel"),
        name="outproj",
    )(attn, sgu, x, attn_gain, w_out)


def kernel(x, ffn1_norm, ffn1_w_gate, ffn1_w_up, ffn1_w_down, mix_norm, w_in, q_norm, k_norm, sgu_ln_gain, sgu_ln_bias, sgu_w_spatial, sgu_b_spatial, attn_out_gain, sgu_out_gain, w_out, ffn2_norm, ffn2_w_gate, ffn2_w_up, ffn2_w_down):
    batch, seq, d_model = x.shape
    depth = ffn1_norm.shape[0]
    assert d_model == D_MODEL and seq % PROJ_TM == 0 and (batch * seq) % FFN_TM == 0
    xt = x.reshape(batch * seq, d_model)
    vec = lambda a: a.reshape(1, -1)
    for l in range(depth):
        xt, w_in_bf, w_out_bf = _ffn(xt, vec(ffn1_norm[l]), ffn1_w_gate[l], ffn1_w_up[l], ffn1_w_down[l],
                                     side=(w_in[l], w_out[l]))
        qa, ka, v, sgu = _mix(xt, vec(mix_norm[l]), w_in_bf, vec(q_norm[l]), vec(k_norm[l]),
                              vec(sgu_ln_gain[l]), vec(sgu_ln_bias[l]), sgu_w_spatial[l], sgu_b_spatial[l].T,
                              vec(sgu_out_gain[l]), seq)
        attn = _attention(qa, ka, v, batch, seq)
        xt = _outproj(attn, sgu, xt, vec(attn_out_gain[l]), w_out_bf)
        xt = _ffn(xt, vec(ffn2_norm[l]), ffn2_w_gate[l], ffn2_w_up[l], ffn2_w_down[l])
    return xt.reshape(batch, seq, d_model)
```

```python
import functools

import jax
import jax.numpy as jnp
from jax import lax
from jax.experimental import pallas as pl
from jax.experimental.pallas import tpu as pltpu

D_MODEL = 2048
D_FF = 5632
ATTN_HEADS = 8
HEAD_DIM = 128
D_ATTN = ATTN_HEADS * HEAD_DIM
SGU_GROUPS = 8
SGU_GROUP_DIM = 128
D_SGU = SGU_GROUPS * SGU_GROUP_DIM
MOBA_BLOCK = 256
MOBA_TOPK = 3
SGU_CHUNK = 128
EPS = 1e-6
NEG_INF = -1e30

LANES = 128
AUG_DIM = 2 * HEAD_DIM
VMEM_LIMIT_BYTES = 60 * 1024 * 1024

FFN_TM = 1024
FFN_HEAD_TF = 256
FFN_TF = 512
PROJ_TM = 512
ATTN_TQ = 512
ATTN_KV_CHUNK = 512

F32 = jnp.float32
BF16 = jnp.bfloat16


def _rmsnorm(x, gain):
    return x * lax.rsqrt(jnp.mean(x * x, axis=-1, keepdims=True) + EPS) * gain


def _gelu_tanh(x):
    cdf = 0.5 * (1.0 + jnp.tanh(0.7978845608028654 * (x + 0.044715 * (x * x * x))))
    return x * cdf


def _dot(a, b):
    return jnp.dot(a, b, preferred_element_type=F32)


def _dot_nt(a, b):
    return lax.dot_general(a, b, (((1,), (1,)), ((), ())), preferred_element_type=F32)


def _dot_nt_3x(a, b):
    a_hi, b_hi = a.astype(BF16), b.astype(BF16)
    a_lo = (a - a_hi.astype(F32)).astype(BF16)
    b_lo = (b - b_hi.astype(F32)).astype(BF16)
    return _dot_nt(a_hi, b_hi) + (_dot_nt(a_hi, b_lo) + _dot_nt(a_lo, b_hi))


def _params(*semantics):
    return pltpu.CompilerParams(dimension_semantics=semantics, vmem_limit_bytes=VMEM_LIMIT_BYTES)


def _resident(shape):
    return pl.BlockSpec(shape, lambda *_: (0,) * len(shape), pipeline_mode=pl.Buffered(1))


def _ffn_accumulate(h_scr, wg, wu, wd, base_ref, o_ref):
    h = h_scr[...]
    acts = []
    for b in range(wg.shape[0]):
        gate = _dot(h, wg[b])
        up = _dot(h, wu[b])
        acts.append(((gate / (1.0 + jnp.exp(-gate))) * up * 0.5).astype(BF16))
    o_ref[...] = base_ref[...] + _dot(jnp.concatenate(acts, axis=1), wd)


def _ffn_step(first, x_ref, g_ref, h_scr, wg, wu, wd, o_ref):
    if first:
        h_scr[...] = _rmsnorm(x_ref[...], g_ref[...]).astype(BF16)
    _ffn_accumulate(h_scr, wg, wu, wd, x_ref if first else o_ref, o_ref)


def _ffn_head_body(x_ref, g_ref, wg_ref, wu_ref, wd_ref, o_ref, wg_bf_ref, wu_bf_ref, wd_bf_ref, h_scr):
    @pl.when(pl.program_id(0) == 0)
    def _():
        x = x_ref[...]
        h_scr[...] = _rmsnorm(x, g_ref[...]).astype(BF16)
        o_ref[...] = x

    wg_bf_ref[0] = wg_ref[...].astype(BF16)
    wu_bf_ref[0] = wu_ref[...].astype(BF16)
    wd_bf_ref[...] = wd_ref[...].astype(BF16)
    _ffn_accumulate(h_scr, wg_bf_ref, wu_bf_ref, wd_bf_ref[...], o_ref, o_ref)


def _ffn_tail_body(x_ref, g_ref, wg_ref, wu_ref, wd_ref, head_hbm, *rest, side_windows):
    n_side = len(side_windows)
    side_in, (o_ref, *side_out), h_scr = rest[:n_side], rest[n_side:-1], rest[-1]
    i = pl.program_id(0)
    j = pl.program_id(1)

    step = i * pl.num_programs(1) + j
    for src, dst, (start, n) in zip(side_in, side_out, side_windows):
        @pl.when((step >= start) & (step < start + n))
        def _(src=src, dst=dst):
            dst[...] = src[...].astype(BF16)

    @pl.when((i == 0) & (j == 0))
    def _():
        pltpu.sync_copy(head_hbm, o_ref)

    for first in (True, False):
        @pl.when((i > 0) & ((j == 0) if first else (j > 0)))
        def _(first=first):
            _ffn_step(first, x_ref, g_ref, h_scr, wg_ref, wu_ref, wd_ref[...], o_ref)


def _ffn(x, gain, w_gate, w_up, w_down, side=()):
    tokens = x.shape[0]
    n_tiles = tokens // FFN_TM
    vec = pl.BlockSpec((1, D_MODEL), lambda *_: (0, 0))

    tf = FFN_HEAD_TF
    col = pl.BlockSpec((D_MODEL, tf), lambda j: (0, j))
    colblk = pl.BlockSpec((1, D_MODEL, tf), lambda j: (j, 0, 0))
    rowblk = pl.BlockSpec((tf, D_MODEL), lambda j: (j, 0))
    tile0 = pl.BlockSpec((FFN_TM, D_MODEL), lambda j: (0, 0))
    head, wg_bf, wu_bf, wd_bf = pl.pallas_call(
        _ffn_head_body,
        grid=(D_FF // tf,),
        in_specs=[tile0, vec, col, col, rowblk],
        out_specs=(tile0, colblk, colblk, rowblk),
        out_shape=(jax.ShapeDtypeStruct((FFN_TM, D_MODEL), F32),
                   jax.ShapeDtypeStruct((D_FF // tf, D_MODEL, tf), BF16),
                   jax.ShapeDtypeStruct((D_FF // tf, D_MODEL, tf), BF16),
                   jax.ShapeDtypeStruct((D_FF, D_MODEL), BF16)),
        scratch_shapes=[pltpu.VMEM((FFN_TM, D_MODEL), BF16)],
        compiler_params=_params("arbitrary"),
        name="ffn_head",
    )(x, gain, w_gate, w_up, w_down)

    tf = FFN_TF
    wcol = pl.BlockSpec((tf // FFN_HEAD_TF, D_MODEL, FFN_HEAD_TF), lambda i, j: (jnp.where(i == 0, 0, j), 0, 0))
    wrow = lambda i, j: (jnp.where(i == 0, 0, j), 0)
    row = pl.BlockSpec((FFN_TM, D_MODEL), lambda i, j: (i, 0))
    n_steps = D_FF // tf
    side_specs, side_windows, start = [], [], 0
    for w in side:
        n_side_tiles = w.shape[1] // LANES
        side_specs.append(pl.BlockSpec(
            (w.shape[0], LANES),
            lambda i, j, start=start, last=n_side_tiles - 1: (0, jnp.clip(i * n_steps + j - start, 0, last))))
        side_windows.append((start, n_side_tiles))
        start += n_side_tiles
    assert start <= n_tiles * n_steps
    outs = pl.pallas_call(
        functools.partial(_ffn_tail_body, side_windows=tuple(side_windows)),
        grid=(n_tiles, n_steps),
        in_specs=[row, vec, wcol, wcol,
                  pl.BlockSpec((tf, D_MODEL), wrow), pl.BlockSpec(memory_space=pl.ANY), *side_specs],
        out_specs=(row, *side_specs),
        out_shape=(jax.ShapeDtypeStruct((tokens, D_MODEL), F32),
                   *(jax.ShapeDtypeStruct(w.shape, BF16) for w in side)),
        scratch_shapes=[pltpu.VMEM((FFN_TM, D_MODEL), BF16)],
        compiler_params=_params("arbitrary", "arbitrary"),
        name="ffn_tail",
    )(x, gain, wg_bf, wu_bf, wd_bf, head, *side)
    return outs if side else outs[0]


def _moba_gate(n0, q, k, gq_ref, gk_ref, qa_ref, ka_ref, km_scr):
    tm = PROJ_TM
    blocks_per_batch = km_scr.shape[0]
    scale = HEAD_DIM ** -0.5
    blk_rows = lax.broadcasted_iota(jnp.int32, (blocks_per_batch, HEAD_DIM), 0)
    n_iota = lax.broadcasted_iota(jnp.int32, (blocks_per_batch, tm), 0)
    lane_tok = lax.broadcasted_iota(jnp.int32, (blocks_per_batch, tm), 1)
    q_blk = n0 + lane_tok // MOBA_BLOCK
    valid = n_iota < q_blk
    tok_blk = n0 + lax.broadcasted_iota(jnp.int32, (tm, LANES), 0) // MOBA_BLOCK
    onehot = jnp.where(lax.broadcasted_iota(jnp.int32, (tm, LANES), 1) == tok_blk, 1.0, 0.0).astype(BF16)
    pad_rows = jnp.zeros((LANES - blocks_per_batch, tm), F32)

    for hh in range(ATTN_HEADS):
        sl = slice(hh * HEAD_DIM, (hh + 1) * HEAD_DIM)
        qn = _rmsnorm(q[:, sl], gq_ref[...])
        kn = _rmsnorm(k[:, sl], gk_ref[...])

        kmean = km_scr[:, sl]
        for b in range(tm // MOBA_BLOCK):
            km = jnp.mean(kn[b * MOBA_BLOCK:(b + 1) * MOBA_BLOCK], axis=0, keepdims=True)
            kmean = jnp.where(blk_rows == n0 + b, km, kmean)
        km_scr[:, sl] = kmean

        gate = _dot_nt_3x(kmean, qn)
        cnt = jnp.zeros((blocks_per_batch, tm), jnp.int32)
        for m in range(blocks_per_batch - 1):
            gm = gate[m:m + 1, :]
            beats = (gm > gate) | ((gm == gate) & (m < n_iota))
            cnt = cnt + jnp.where(beats & (m < q_blk), 1, 0)
        sel = ((cnt < MOBA_TOPK) & valid) | (n_iota == q_blk)
        bias_t = jnp.concatenate([jnp.where(sel, 0.0, NEG_INF), pad_rows], axis=0)
        bias = bias_t.T

        a0 = hh * AUG_DIM
        qa_ref[:, a0:a0 + HEAD_DIM] = (qn * scale).astype(BF16)
        qa_ref[:, a0 + HEAD_DIM:a0 + AUG_DIM] = bias.astype(BF16)
        ka_ref[:, a0:a0 + HEAD_DIM] = kn.astype(BF16)
        ka_ref[:, a0 + HEAD_DIM:a0 + AUG_DIM] = onehot


def _spatial_gate(u, g, lng_ref, lnb_ref, ws_ref, bs_ref, og_ref, o_ref, sg_scr):
    tm = PROJ_TM
    r = lax.broadcasted_iota(jnp.int32, (SGU_CHUNK, SGU_CHUNK), 0)
    c = lax.broadcasted_iota(jnp.int32, (SGU_CHUNK, SGU_CHUNK), 1)
    for gi in range(SGU_GROUPS):
        sl = slice(gi * SGU_GROUP_DIM, (gi + 1) * SGU_GROUP_DIM)
        gg = g[:, sl]
        d = gg - jnp.mean(gg, axis=-1, keepdims=True)
        var = jnp.mean(d * d, axis=-1, keepdims=True)
        gn = (d * lax.rsqrt(var + EPS) * lng_ref[:, sl] + lnb_ref[:, sl]).astype(BF16)
        w_causal = jnp.where(r >= c, ws_ref[gi], 0.0).astype(BF16)
        b_col = bs_ref[:, gi:gi + 1]
        for ci in range(tm // SGU_CHUNK):
            rows = slice(ci * SGU_CHUNK, (ci + 1) * SGU_CHUNK)
            mixed = _dot(w_causal, gn[rows]) + b_col
            sg_scr[rows, sl] = u[rows, sl] * mixed
    o_ref[...] = _rmsnorm(sg_scr[...], og_ref[...]).astype(BF16)


def _mix_body(x_ref, gx_ref, w_ref, gq_ref, gk_ref, lng_ref, lnb_ref, ws_ref, bs_ref, og_ref,
              qa_ref, ka_ref, v_ref, o_ref, km_scr, sg_scr, *, tiles_per_batch):
    i = pl.program_id(0)
    n0 = (i % tiles_per_batch) * (PROJ_TM // MOBA_BLOCK)

    @pl.when(i == 0)
    def _init():
        km_scr[...] = jnp.zeros_like(km_scr)

    h = _rmsnorm(x_ref[...], gx_ref[...]).astype(BF16)
    col = lambda c0, width: w_ref[:, c0:c0 + width]
    q = _dot(h, col(0, D_ATTN))
    k = _dot(h, col(D_ATTN, D_ATTN))
    u = _gelu_tanh(_dot(h, col(3 * D_ATTN, D_SGU)))
    _moba_gate(n0, q, k, gq_ref, gk_ref, qa_ref, ka_ref, km_scr)
    g = _gelu_tanh(_dot(h, col(3 * D_ATTN + D_SGU, D_SGU)))
    v_ref[...] = _dot(h, col(2 * D_ATTN, D_ATTN)).astype(BF16)
    _spatial_gate(u, g, lng_ref, lnb_ref, ws_ref, bs_ref, og_ref, o_ref, sg_scr)


def _mix(x, x_gain, w_in, q_gain, k_gain, ln_gain, ln_bias, w_spatial, b_spatial_t, out_gain, seq):
    tokens = x.shape[0]
    tm = PROJ_TM
    tiles_per_batch = seq // tm
    blocks_per_batch = seq // MOBA_BLOCK
    assert blocks_per_batch <= LANES and blocks_per_batch % 8 == 0
    row = lambda width: pl.BlockSpec((tm, width), lambda i: (i, 0))
    return pl.pallas_call(
        functools.partial(_mix_body, tiles_per_batch=tiles_per_batch),
        grid=(tokens // tm,),
        in_specs=[row(D_MODEL), _resident((1, D_MODEL)), _resident((D_MODEL, 3 * D_ATTN + 2 * D_SGU)),
                  _resident((1, HEAD_DIM)), _resident((1, HEAD_DIM)),
                  _resident((1, D_SGU)), _resident((1, D_SGU)),
                  _resident((SGU_GROUPS, SGU_CHUNK, SGU_CHUNK)), _resident((SGU_CHUNK, SGU_GROUPS)),
                  _resident((1, D_SGU))],
        out_specs=(row(ATTN_HEADS * AUG_DIM), row(ATTN_HEADS * AUG_DIM), row(D_ATTN), row(D_SGU)),
        out_shape=(jax.ShapeDtypeStruct((tokens, ATTN_HEADS * AUG_DIM), BF16),
                   jax.ShapeDtypeStruct((tokens, ATTN_HEADS * AUG_DIM), BF16),
                   jax.ShapeDtypeStruct((tokens, D_ATTN), BF16),
                   jax.ShapeDtypeStruct((tokens, D_SGU), BF16)),
        scratch_shapes=[pltpu.VMEM((blocks_per_batch, D_ATTN), F32), pltpu.VMEM((tm, D_SGU), F32)],
        compiler_params=_params("arbitrary"),
        name="mix_proj",
    )(x, x_gain, w_in, q_gain, k_gain, ln_gain, ln_bias, w_spatial, b_spatial_t, out_gain)


def _attn_scores(t, q_ref, k_ref, s_ref, m_ref):
    tq, kc = ATTN_TQ, ATTN_KV_CHUNK
    half = MOBA_BLOCK
    m_part = None

    def fold(m_part, s):
        for j in range(s.shape[1] // LANES):
            sj = s[:, j * LANES:(j + 1) * LANES]
            m_part = sj if m_part is None else jnp.maximum(m_part, sj)
        return m_part

    for c in range(t):
        s = _dot_nt(q_ref[...], k_ref[c * kc:(c + 1) * kc, :])
        s_ref[:, c * kc:(c + 1) * kc] = s
        m_part = fold(m_part, s)
    k0 = t * kc
    r = lax.broadcasted_iota(jnp.int32, (half, half), 0)
    col = lax.broadcasted_iota(jnp.int32, (half, half), 1)
    s_top = jnp.where(col <= r, _dot_nt(q_ref[0:half, :], k_ref[k0:k0 + half, :]), NEG_INF)
    s_ref[0:half, k0:k0 + half] = s_top
    r2 = lax.broadcasted_iota(jnp.int32, (half, kc), 0) + half
    col2 = lax.broadcasted_iota(jnp.int32, (half, kc), 1)
    s_bot = jnp.where(col2 <= r2, _dot_nt(q_ref[half:tq, :], k_ref[k0:k0 + kc, :]), NEG_INF)
    s_ref[half:tq, k0:k0 + kc] = s_bot
    m_top = fold(None if m_part is None else m_part[0:half], s_top)
    m_bot = fold(None if m_part is None else m_part[half:tq], s_bot)
    m_part = jnp.concatenate([m_top, m_bot], axis=0)
    m_ref[...] = jnp.broadcast_to(jnp.max(m_part, axis=-1, keepdims=True), (tq, LANES))


def _attn_values(t, va_ref, o_ref, s_ref, m_ref):
    tq, kc = ATTN_TQ, ATTN_KV_CHUNK
    half = MOBA_BLOCK
    m = m_ref[...]

    def probs(rows, c0, width):
        return jnp.concatenate(
            [jnp.exp(s_ref[rows, c0 + j * LANES:c0 + (j + 1) * LANES] - m[rows]).astype(BF16)
             for j in range(width // LANES)], axis=1)

    acc = jnp.zeros((tq, AUG_DIM), F32)
    for c in range(t):
        acc = acc + _dot(probs(slice(0, tq), c * kc, kc), va_ref[c * kc:(c + 1) * kc, :])
    k0 = t * kc
    top = acc[0:half] + _dot(probs(slice(0, half), k0, half), va_ref[k0:k0 + half, :])
    bot = acc[half:tq] + _dot(probs(slice(half, tq), k0, kc), va_ref[k0:k0 + kc, :])
    acc = jnp.concatenate([top, bot], axis=0)
    o_ref[...] = acc[:, 0:HEAD_DIM] / acc[:, HEAD_DIM:AUG_DIM]


def _attn_body(q_ref, k_ref, v_ref, o_ref, s_scr, m_scr, va_scr):
    n_tiles = k_ref.shape[0] // ATTN_TQ
    tile = lambda t: slice(t * ATTN_TQ, (t + 1) * ATTN_TQ)
    va_scr[:, 0:HEAD_DIM] = v_ref[...]
    va_scr[:, HEAD_DIM:AUG_DIM] = jnp.ones((va_scr.shape[0], HEAD_DIM), BF16)

    def sub_step(ts):
        if ts >= 1:
            _attn_values(ts - 1, va_scr, o_ref.at[tile(ts - 1), :], s_scr.at[(ts - 1) % 2], m_scr.at[(ts - 1) % 2])
        if ts < n_tiles:
            _attn_scores(ts, q_ref.at[tile(ts), :], k_ref, s_scr.at[ts % 2], m_scr.at[ts % 2])

    for ts in range(n_tiles + 1):
        pl.when(pl.program_id(1) + ts >= 0)(functools.partial(sub_step, ts))


def _attention(qa, ka, v, batch, seq):
    tokens = batch * seq
    return pl.pallas_call(
        _attn_body,
        grid=(batch, ATTN_HEADS),
        in_specs=[pl.BlockSpec((seq, AUG_DIM), lambda b, h: (b, h)),
                  pl.BlockSpec((seq, AUG_DIM), lambda b, h: (b, h)),
                  pl.BlockSpec((seq, HEAD_DIM), lambda b, h: (b, h))],
        out_specs=pl.BlockSpec((seq, HEAD_DIM), lambda b, h: (b, h)),
        out_shape=jax.ShapeDtypeStruct((tokens, D_ATTN), F32),
        scratch_shapes=[pltpu.VMEM((2, ATTN_TQ, seq), F32), pltpu.VMEM((2, ATTN_TQ, LANES), F32),
                        pltpu.VMEM((seq, AUG_DIM), BF16)],
        compiler_params=_params("parallel", "parallel"),
        name="moba_attn",
    )(qa, ka, v)


def _outproj_body(a_ref, s_ref, x_ref, ga_ref, w_ref, o_ref):
    from_sgu = _dot(s_ref[...], w_ref[D_ATTN:D_ATTN + D_SGU, :])
    an = _rmsnorm(a_ref[...], ga_ref[...]).astype(BF16)
    o_ref[...] = x_ref[...] + _dot(an, w_ref[0:D_ATTN, :]) + from_sgu


def _outproj(attn, sgu, x, attn_gain, w_out):
    tokens = x.shape[0]
    tm = PROJ_TM
    row = lambda width: pl.BlockSpec((tm, width), lambda i: (i, 0))
    return pl.pallas_call(
        _outproj_body,
        grid=(tokens // tm,),
        in_specs=[row(D_ATTN), row(D_SGU), row(D_MODEL), _resident((1, D_ATTN)),
                  _resident((D_ATTN + D_SGU, D_MODEL))],
        out_specs=row(D_MODEL),
        out_shape=jax.ShapeDtypeStruct((tokens, D_MODEL), F32),
        compiler_params=_params("parallel"),
        name="outproj",
    )(attn, sgu, x, attn_gain, w_out)


def kernel(x, ffn1_norm, ffn1_w_gate, ffn1_w_up, ffn1_w_down, mix_norm, w_in, q_norm, k_norm, sgu_ln_gain, sgu_ln_bias, sgu_w_spatial, sgu_b_spatial, attn_out_gain, sgu_out_gain, w_out, ffn2_norm, ffn2_w_gate, ffn2_w_up, ffn2_w_down):
    batch, seq, d_model = x.shape
    depth = ffn1_norm.shape[0]
    assert d_model == D_MODEL and seq % PROJ_TM == 0 and (batch * seq) % FFN_TM == 0
    xt = x.reshape(batch * seq, d_model)
    vec = lambda a: a.reshape(1, -1)
    for l in range(depth):
        xt, w_in_bf, w_out_bf = _ffn(xt, vec(ffn1_norm[l]), ffn1_w_gate[l], ffn1_w_up[l], ffn1_w_down[l],
                                     side=(w_in[l], w_out[l]))
        qa, ka, v, sgu = _mix(xt, vec(mix_norm[l]), w_in_bf, vec(q_norm[l]), vec(k_norm[l]),
                              vec(sgu_ln_gain[l]), vec(sgu_ln_bias[l]), sgu_w_spatial[l], sgu_b_spatial[l].T,
                              vec(sgu_out_gain[l]), seq)
        attn = _attention(qa, ka, v, batch, seq)
        xt = _outproj(attn, sgu, xt, vec(attn_out_gain[l]), w_out_bf)
        xt = _ffn(xt, vec(ffn2_norm[l]), ffn2_w_gate[l], ffn2_w_up[l], ffn2_w_down[l])
    return xt.reshape(batch, seq, d_model)
```
